```python
import math
import jax
import jax.numpy as jnp
from jax import lax
import numpy as np

D_MODEL = 2048
BATCH = 4
SEQ = 8192
DEPTH = 4

A_HEADS = 4
A_QK_DIM = 64
A_V_DIM = 128
B_HEADS = 4
B_Q_LORA = 512
B_KV_LORA = 256
B_NOPE = 128
B_ROPE = 64
B_V_DIM = 128
ROPE_THETA = 10000.0
C_HEADS = 16
C_KV_HEADS = 4
C_GROUP = C_HEADS // C_KV_HEADS
C_HEAD_DIM = 128
WINDOW = 128
BLOCK = 128
D_FF = 5632
N_EXPERTS = 8
TOP_K = 2
EXPERT_FF = 2816
MOE_BLOCK = 512
ALPHA = (2 * DEPTH) ** 0.25
BETA = (8 * DEPTH) ** -0.25
LN_EPS = 1e-5
RMS_EPS = 1e-6
ADA_STD = 0.5
N_EVEN = (DEPTH + 1) // 2
N_ODD = DEPTH // 2

A_Q_COLS = A_HEADS * 2 * A_QK_DIM
A_K_COLS = A_HEADS * 2 * A_QK_DIM
A_V_COLS = A_HEADS * A_V_DIM
AB_SPLITS = (A_Q_COLS, A_K_COLS, A_V_COLS, B_Q_LORA, B_KV_LORA, B_ROPE)
AB_IN_COLS = A_Q_COLS + A_K_COLS + A_V_COLS + B_Q_LORA + B_KV_LORA + B_ROPE
AB_OUT_IN = A_HEADS * A_V_DIM + B_HEADS * B_V_DIM
C_Q_COLS = C_HEADS * C_HEAD_DIM
C_KV_COLS = C_KV_HEADS * C_HEAD_DIM
C_SPLITS = (C_Q_COLS, C_KV_COLS, C_KV_COLS)
C_IN_COLS = C_Q_COLS + 2 * C_KV_COLS
C_OUT_IN = C_Q_COLS

kernel_name = 'hybrid_diff_mla_swa_moe_encoder'


def split_cols(t, sizes):
    idx = np.cumsum(sizes)[:-1].tolist()
    return jnp.split(t, idx, axis=-1)


def layer_norm(x, g, b):
    xf = x.astype(jnp.float32)
    mu = jnp.mean(xf, axis=-1, keepdims=True)
    var = jnp.mean(jnp.square(xf - mu), axis=-1, keepdims=True)
    return ((xf - mu) * lax.rsqrt(var + LN_EPS)).astype(x.dtype) * g + b


def rms_norm(x, g):
    xf = x.astype(jnp.float32)
    return (xf * lax.rsqrt(jnp.mean(xf * xf, axis=-1, keepdims=True) + RMS_EPS)).astype(x.dtype) * g


def alibi_slopes(n_heads):
    return jnp.asarray(2.0 ** (-8.0 * np.arange(1, n_heads + 1) / n_heads), dtype=jnp.float32)


def rope_tables(seq):
    inv_freq = ROPE_THETA ** (-jnp.arange(0, B_ROPE, 2, dtype=jnp.float32) / B_ROPE)
    ang = jnp.arange(seq, dtype=jnp.float32)[:, None] * inv_freq[None, :]
    return jnp.cos(ang), jnp.sin(ang)


def apply_rope(x, cos, sin):
    x1, x2 = jnp.split(x, 2, axis=-1)
    return jnp.concatenate([x1 * cos - x2 * sin, x2 * cos + x1 * sin], axis=-1).astype(x.dtype)


def sweep_query_blocks(block_fn, *qs):
    bsz, seq = qs[0].shape[:2]
    nb = seq // BLOCK
    blocked = tuple(jnp.swapaxes(q.reshape(bsz, nb, BLOCK, *q.shape[2:]), 0, 1) for q in qs)
    out = lax.map(lambda args: block_fn(*args), (jnp.arange(nb),) + blocked)
    out = jnp.swapaxes(out, 0, 1)
    return out.reshape(bsz, seq, *out.shape[3:])


def mixer_ab(h, w_in, lam, diff_g, q_norm_g, kv_norm_g, w_uq, w_ukv, w_out, lam_init, cos, sin):
    bsz, seq, _ = h.shape
    pos = jnp.arange(seq)
    aq, ak, av, cq, ckv, kr = split_cols(h @ w_in, AB_SPLITS)

    aq = aq.reshape(bsz, seq, A_HEADS, 2, A_QK_DIM)
    ak = ak.reshape(bsz, seq, A_HEADS, 2, A_QK_DIM)
    av = av.reshape(bsz, seq, A_HEADS, A_V_DIM)
    lam32 = lam.astype(jnp.float32)
    lam_full = (jnp.exp(jnp.sum(lam32[0] * lam32[1])) - jnp.exp(jnp.sum(lam32[2] * lam32[3])) + lam_init)
    slopes_a = alibi_slopes(A_HEADS)[:, None, None]
    scale_a = A_QK_DIM ** -0.5

    def diff_block(j, qb):
        t = j * BLOCK + jnp.arange(BLOCK)
        dist = jnp.abs(t[:, None] - pos[None, :]).astype(jnp.float32)
        s = jnp.einsum('bqhmd,bkhmd->bmhqk', qb, ak, preferred_element_type=jnp.float32) * scale_a - slopes_a * dist
        e = jnp.exp(s - jnp.max(s, axis=-1, keepdims=True))
        inv_l = jnp.transpose(1.0 / jnp.sum(e, axis=-1), (0, 3, 1, 2))[..., None]
        o = jnp.einsum('bmhqk,bkhd->bqmhd', e.astype(av.dtype), av, preferred_element_type=jnp.float32) * inv_l
        return (o[:, :, 0] - lam_full * o[:, :, 1]).astype(av.dtype)

    o_a = sweep_query_blocks(diff_block, aq)
    o_a = (rms_norm(o_a, diff_g) * (1.0 - lam_init)).reshape(bsz, seq, A_V_COLS)

    q = (rms_norm(cq, q_norm_g) @ w_uq).reshape(bsz, seq, B_HEADS, B_NOPE + B_ROPE)
    q_nope = q[..., :B_NOPE]
    q_rope = apply_rope(q[..., B_NOPE:], cos[:, None, :], sin[:, None, :])
    kv = (rms_norm(ckv, kv_norm_g) @ w_ukv).reshape(bsz, seq, B_HEADS, B_NOPE + B_V_DIM)
    k_nope = kv[..., :B_NOPE]
    v_b = kv[..., B_NOPE:]
    k_rope = apply_rope(kr, cos, sin)
    scale_b = (B_NOPE + B_ROPE) ** -0.5

    def mla_block(j, qn, qr):
        s = (jnp.einsum('bqhd,bkhd->bhqk', qn, k_nope, preferred_element_type=jnp.float32)
             + jnp.einsum('bqhr,bkr->bhqk', qr, k_rope, preferred_element_type=jnp.float32)) * scale_b
        e = jnp.exp(s - jnp.max(s, axis=-1, keepdims=True))
        inv_l = jnp.transpose(1.0 / jnp.sum(e, axis=-1), (0, 2, 1))[..., None]
        o = jnp.einsum('bhqk,bkhd->bqhd', e.astype(v_b.dtype), v_b, preferred_element_type=jnp.float32) * inv_l
        return o.astype(v_b.dtype)

    o_b = sweep_query_blocks(mla_block, q_nope, q_rope).reshape(bsz, seq, B_HEADS * B_V_DIM)
    return jnp.concatenate([o_a, o_b], axis=-1) @ w_out


def mixer_c(h, w_in, sink, w_out):
    bsz, seq, _ = h.shape
    q, k, v = split_cols(h @ w_in, C_SPLITS)
    q = q.reshape(bsz, seq, C_KV_HEADS, C_GROUP, C_HEAD_DIM)
    k = k.reshape(bsz, seq, C_KV_HEADS, C_HEAD_DIM)
    v = v.reshape(bsz, seq, C_KV_HEADS, C_HEAD_DIM)
    pad = ((0, 0), (BLOCK, BLOCK), (0, 0), (0, 0))
    kp = jnp.pad(k, pad)
    vp = jnp.pad(v, pad)
    slopes = alibi_slopes(C_HEADS).reshape(C_KV_HEADS, C_GROUP)[:, :, None, None]
    sink_l = sink.astype(jnp.float32).reshape(C_KV_HEADS, C_GROUP)[None, :, :, None, None]
    scale = C_HEAD_DIM ** -0.5

    def band_block(j, qb):
        kb = lax.dynamic_slice_in_dim(kp, j * BLOCK, 3 * BLOCK, axis=1)
        vb = lax.dynamic_slice_in_dim(vp, j * BLOCK, 3 * BLOCK, axis=1)
        t = j * BLOCK + jnp.arange(BLOCK)
        s_pos = j * BLOCK - BLOCK + jnp.arange(3 * BLOCK)
        dist = jnp.abs(t[:, None] - s_pos[None, :])
        valid = (dist <= WINDOW) & (s_pos >= 0)[None, :] & (s_pos < seq)[None, :]
        s = (jnp.einsum('bqngd,bknd->bngqk', qb, kb, preferred_element_type=jnp.float32) * scale
             - slopes * dist.astype(jnp.float32))
        s = jnp.where(valid, s, -jnp.inf)
        m = jnp.maximum(jnp.max(s, axis=-1, keepdims=True), sink_l)
        e = jnp.exp(s - m)
        p = e / (jnp.sum(e, axis=-1, keepdims=True) + jnp.exp(sink_l - m))
        return jnp.einsum('bngqk,bknd->bqngd', p.astype(vb.dtype), vb)

    o = sweep_query_blocks(band_block, q).reshape(bsz, seq, C_OUT_IN)
    return o @ w_out


def swiglu(h, w_gate, w_up, w_down):
    return (jax.nn.silu(h @ w_gate) * (h @ w_up)) @ w_down


def moe_swiglu(h, w_router, b_router, w_gate, w_up, w_down):
    bsz, seq, d = h.shape
    xf = h.reshape(-1, d)
    n_assign = xf.shape[0] * TOP_K
    logits = (xf @ w_router).astype(jnp.float32) + b_router.astype(jnp.float32)
    top_logits, top_idx = lax.top_k(logits, TOP_K)
    gates = jax.nn.softmax(top_logits, axis=-1)
    flat_e = top_idx.reshape(-1).astype(jnp.int32)
    flat_g = gates.reshape(-1)
    order = jnp.argsort(flat_e)
    sorted_e = flat_e[order]
    sizes = jnp.bincount(flat_e, length=N_EXPERTS).astype(jnp.int32)
    starts = jnp.cumsum(sizes) - sizes
    padded = ((sizes + MOE_BLOCK - 1) // MOE_BLOCK) * MOE_BLOCK
    pad_end = jnp.cumsum(padded)
    pad_start = pad_end - padded
    dest = pad_start[sorted_e] + (jnp.arange(n_assign, dtype=jnp.int32) - starts[sorted_e])
    n_blocks = -(-n_assign // MOE_BLOCK) + N_EXPERTS
    n_rows = n_blocks * MOE_BLOCK
    row_tok = jnp.zeros((n_rows,), jnp.int32).at[dest].set((order // TOP_K).astype(jnp.int32))
    row_gate = jnp.zeros((n_rows,), flat_g.dtype).at[dest].set(flat_g[order])
    block_start = jnp.arange(n_blocks, dtype=jnp.int32) * MOE_BLOCK
    block_e = jnp.minimum(jnp.searchsorted(pad_end, block_start, side='right'), N_EXPERTS - 1)

    def expert_block(args):
        e, tok_b, g_b = args
        xb = xf[tok_b]
        hb = jax.nn.silu(xb @ w_gate[e]) * (xb @ w_up[e])
        return (hb @ w_down[e]) * g_b[:, None].astype(xb.dtype)

    y = lax.map(expert_block, (block_e, row_tok.reshape(n_blocks, MOE_BLOCK), row_gate.reshape(n_blocks, MOE_BLOCK)))
    out = jnp.zeros_like(xf).at[row_tok].add(y.reshape(n_rows, d).astype(xf.dtype))
    return out.reshape(bsz, seq, d)


def modulate(x, c_act, w, b):
    m = (c_act @ w + b)[:, None, :]
    shift, scale, gate = jnp.split(m, 3, axis=-1)
    return x * (1.0 + scale) + shift, gate


def post_norm(x, gate, y, g, b):
    return layer_norm(ALPHA * x + (1.0 + gate) * y, g, b)


def setup_inputs(seed: int = 0) -> dict:
    key = jax.random.key(seed)
    ks = iter(jax.random.split(key, 32))

    def nrm(shape, std):
        return jax.random.normal(next(ks), shape, jnp.float32) * std

    D = D_MODEL
    ab_col_scale = jnp.concatenate([jnp.ones((A_Q_COLS + A_K_COLS,), jnp.float32),
                                    jnp.full((A_V_COLS,), BETA, jnp.float32),
                                    jnp.ones((B_Q_LORA + B_KV_LORA + B_ROPE,), jnp.float32)])
    ukv_col_scale = jnp.tile(jnp.concatenate([jnp.ones((B_NOPE,), jnp.float32),
                                              jnp.full((B_V_DIM,), BETA, jnp.float32)]), B_HEADS)
    c_col_scale = jnp.concatenate([jnp.ones((C_Q_COLS + C_KV_COLS,), jnp.float32),
                                   jnp.full((C_KV_COLS,), BETA, jnp.float32)])
    return {
        'x': nrm((BATCH, SEQ, D), 1.0),
        'c': nrm((BATCH, D), 1.0),
        'ada_w': nrm((DEPTH, 2, D, 3 * D), ADA_STD * D ** -0.5),
        'ada_b': nrm((DEPTH, 2, 3 * D), 0.02),
        'ln_g': 1.0 + nrm((DEPTH, 2, D), 0.02),
        'ln_b': nrm((DEPTH, 2, D), 0.02),
        'ab_w_in': nrm((N_EVEN, D, AB_IN_COLS), D ** -0.5) * ab_col_scale,
        'ab_lam': nrm((N_EVEN, 4, A_QK_DIM), 0.1),
        'ab_diff_g': 1.0 + nrm((N_EVEN, A_V_DIM), 0.02),
        'ab_q_norm_g': 1.0 + nrm((N_EVEN, B_Q_LORA), 0.02),
        'ab_kv_norm_g': 1.0 + nrm((N_EVEN, B_KV_LORA), 0.02),
        'ab_w_uq': nrm((N_EVEN, B_Q_LORA, B_HEADS * (B_NOPE + B_ROPE)), B_Q_LORA ** -0.5),
        'ab_w_ukv': nrm((N_EVEN, B_KV_LORA, B_HEADS * (B_NOPE + B_V_DIM)), B_KV_LORA ** -0.5) * ukv_col_scale,
        'ab_w_out': nrm((N_EVEN, AB_OUT_IN, D), BETA * AB_OUT_IN ** -0.5),
        'c_w_in': nrm((N_ODD, D, C_IN_COLS), D ** -0.5) * c_col_scale,
        'c_sink': nrm((N_ODD, C_HEADS), 0.5),
        'c_w_out': nrm((N_ODD, C_OUT_IN, D), BETA * C_OUT_IN ** -0.5),
        'ffn_w_gate': nrm((N_EVEN, D, D_FF), D ** -0.5),
        'ffn_w_up': nrm((N_EVEN, D, D_FF), D ** -0.5),
        'ffn_w_down': nrm((N_EVEN, D_FF, D), BETA * D_FF ** -0.5),
        'moe_w_router': nrm((N_ODD, D, N_EXPERTS), D ** -0.5),
        'moe_b_router': nrm((N_ODD, N_EXPERTS), 0.01),
        'moe_w_gate': nrm((N_ODD, N_EXPERTS, D, EXPERT_FF), D ** -0.5),
        'moe_w_up': nrm((N_ODD, N_EXPERTS, D, EXPERT_FF), D ** -0.5),
        'moe_w_down': nrm((N_ODD, N_EXPERTS, EXPERT_FF, D), BETA * EXPERT_FF ** -0.5),
    }


def reference(x, c, ada_w, ada_b, ln_g, ln_b, ab_w_in, ab_lam, ab_diff_g, ab_q_norm_g, ab_kv_norm_g,
              ab_w_uq, ab_w_ukv, ab_w_out, c_w_in, c_sink, c_w_out, ffn_w_gate, ffn_w_up, ffn_w_down,
              moe_w_router, moe_b_router, moe_w_gate, moe_w_up, moe_w_down):
    cos, sin = rope_tables(x.shape[1])
    c_act = jax.nn.silu(c)
    for layer in range(DEPTH):
        i = layer // 2
        h, gate = modulate(x, c_act, ada_w[layer, 0], ada_b[layer, 0])
        if layer % 2 == 0:
            lam_init = 0.8 - 0.6 * math.exp(-0.3 * layer)
            y = mixer_ab(h, ab_w_in[i], ab_lam[i], ab_diff_g[i], ab_q_norm_g[i], ab_kv_norm_g[i],
                         ab_w_uq[i], ab_w_ukv[i], ab_w_out[i], lam_init, cos, sin)
        else:
            y = mixer_c(h, c_w_in[i], c_sink[i], c_w_out[i])
        x = post_norm(x, gate, y, ln_g[layer, 0], ln_b[layer, 0])
        h, gate = modulate(x, c_act, ada_w[layer, 1], ada_b[layer, 1])
        if layer % 2 == 0:
            y = swiglu(h, ffn_w_gate[i], ffn_w_up[i], ffn_w_down[i])
        else:
            y = moe_swiglu(h, moe_w_router[i], moe_b_router[i], moe_w_gate[i], moe_w_up[i], moe_w_down[i])
        x = post_norm(x, gate, y, ln_g[layer, 1], ln_b[layer, 1])
    return x
```

```python
import functools
import math

import numpy as np
import jax
import jax.numpy as jnp
from jax import lax
from jax.experimental import pallas as pl
from jax.experimental.pallas import tpu as pltpu

F32 = jnp.float32
BF16 = jnp.bfloat16

A_HEADS = 4
A_QK_DIM = 64
A_V_DIM = 128
B_HEADS = 4
B_Q_LORA = 512
B_KV_LORA = 256
B_NOPE = 128
B_ROPE = 64
B_V_DIM = 128
ROPE_THETA = 10000.0
C_HEADS = 16
C_KV_HEADS = 4
C_GROUP = C_HEADS // C_KV_HEADS
C_HEAD_DIM = 128
WINDOW = 128
N_EXPERTS = 8
TOP_K = 2
MOE_BLOCK = 512
LN_EPS = 1e-5
RMS_EPS = 1e-6

A_COLS = A_HEADS * 2 * A_QK_DIM
LANES = 128
VMEM_LIMIT = 56 * 1024 * 1024
NEG_INF = float("-inf")


def _cparams(sem, vmem=VMEM_LIMIT):
    return pltpu.CompilerParams(dimension_semantics=sem, vmem_limit_bytes=vmem)


def _adaln_kernel(c_ref, w_ref, b_ref, o_ref):
    c = c_ref[...]
    ca = c * jax.nn.sigmoid(c)
    o_ref[...] = jnp.dot(ca, w_ref[...], preferred_element_type=F32,
                         precision=lax.Precision.HIGHEST) + b_ref[...]


def adaln_all(c_pad, ada_w, ada_b):
    l2, d, n = ada_w.shape
    tn = 768
    return pl.pallas_call(
        _adaln_kernel,
        grid=(l2, n // tn),
        in_specs=[pl.BlockSpec((8, d), lambda l, j: (0, 0)),
                  pl.BlockSpec((None, d, tn), lambda l, j: (l, 0, j)),
                  pl.BlockSpec((None, 1, tn), lambda l, j: (l, 0, j))],
        out_specs=pl.BlockSpec((None, 8, tn), lambda l, j: (l, 0, j)),
        out_shape=jax.ShapeDtypeStruct((l2, 8, n), F32),
        compiler_params=_cparams(("parallel", "parallel")),
        name="adaln",
    )(c_pad, ada_w, ada_b)


def _modulate_kernel(x_ref, mod_ref, h_ref):
    x = x_ref[...]
    h_ref[...] = (x * (1.0 + mod_ref[1:2, :]) + mod_ref[0:1, :]).astype(h_ref.dtype)


def modulate(x, mods, l2, out_dtype):
    b, s, d = x.shape
    ts = 512
    return pl.pallas_call(
        _modulate_kernel,
        grid=(b, s // ts),
        in_specs=[pl.BlockSpec((None, ts, d), lambda i, j: (i, j, 0)),
                  pl.BlockSpec((None, None, 3, d), lambda i, j: (l2, i, 0, 0))],
        out_specs=pl.BlockSpec((None, ts, d), lambda i, j: (i, j, 0)),
        out_shape=jax.ShapeDtypeStruct((b, s, d), out_dtype),
        compiler_params=_cparams(("parallel", "parallel")),
        name="modulate",
    )(x, mods)


def _postnorm_kernel(*refs, alpha, pair_sum, has_next):
    if has_next:
        x_ref, y_ref, modc_ref, g_ref, b_ref, modn_ref, xo_ref, h_ref = refs
    else:
        x_ref, y_ref, modc_ref, g_ref, b_ref, xo_ref = refs
    x = x_ref[...]
    d = x.shape[-1]
    if pair_sum:
        y = y_ref[:, :d].astype(F32) + y_ref[:, d:].astype(F32)
    else:
        y = y_ref[...].astype(F32)
    z = alpha * x + (1.0 + modc_ref[2:3, :]) * y
    mu = jnp.mean(z, axis=-1, keepdims=True)
    zc = z - mu
    var = jnp.mean(zc * zc, axis=-1, keepdims=True)
    xn = zc * lax.rsqrt(var + LN_EPS) * g_ref[...] + b_ref[...]
    xo_ref[...] = xn
    if has_next:
        h_ref[...] = (xn * (1.0 + modn_ref[1:2, :]) + modn_ref[0:1, :]).astype(h_ref.dtype)


def postnorm(x, y, mods, l2, ln_g, ln_b, alpha, *, pair_sum=False, next_dtype=None):
    b, s, d = x.shape
    ts = 256
    has_next = next_dtype is not None
    yw = y.shape[-1]
    in_specs = [pl.BlockSpec((None, ts, d), lambda i, j: (i, j, 0)),
                pl.BlockSpec((None, ts, yw), lambda i, j: (i, j, 0)),
                pl.BlockSpec((None, None, 3, d), lambda i, j: (l2, i, 0, 0)),
                pl.BlockSpec((1, d), lambda i, j: (0, 0)),
                pl.BlockSpec((1, d), lambda i, j: (0, 0))]
    args = [x, y, mods, ln_g.reshape(1, d), ln_b.reshape(1, d)]
    out_specs = [pl.BlockSpec((None, ts, d), lambda i, j: (i, j, 0))]
    out_shape = [jax.ShapeDtypeStruct((b, s, d), F32)]
    if has_next:
        in_specs.append(pl.BlockSpec((None, None, 3, d), lambda i, j: (l2 + 1, i, 0, 0)))
        args.append(mods)
        out_specs.append(pl.BlockSpec((None, ts, d), lambda i, j: (i, j, 0)))
        out_shape.append(jax.ShapeDtypeStruct((b, s, d), next_dtype))
    out = pl.pallas_call(
        functools.partial(_postnorm_kernel, alpha=alpha, pair_sum=pair_sum, has_next=has_next),
        grid=(b, s // ts),
        in_specs=in_specs,
        out_specs=out_specs,
        out_shape=out_shape,
        compiler_params=_cparams(("parallel", "parallel")),
        name="postnorm",
    )(*args)
    return (out[0], out[1]) if has_next else (out[0], None)


def _rms_rows(a_ref, g_ref):
    a = a_ref[...].astype(F32)
    return (a * lax.rsqrt(jnp.mean(a * a, axis=-1, keepdims=True) + RMS_EPS) * g_ref[...]).astype(BF16)


def _matmul_kernel(a_ref, w_ref, o_ref):
    o_ref[...] = jnp.dot(a_ref[...], w_ref[...], preferred_element_type=F32).astype(o_ref.dtype)


def matmul(a, w, out_dtype, tm=512, tn=None):
    m, k = a.shape
    n = w.shape[1]
    tn = n if tn is None else tn
    return pl.pallas_call(
        _matmul_kernel,
        grid=(m // tm, n // tn),
        in_specs=[pl.BlockSpec((tm, k), lambda i, j: (i, 0)),
                  pl.BlockSpec((k, tn), lambda i, j: (0, j))],
        out_specs=pl.BlockSpec((tm, tn), lambda i, j: (i, j)),
        out_shape=jax.ShapeDtypeStruct((m, n), out_dtype),
        compiler_params=_cparams(("parallel", "parallel")),
        name="matmul",
    )(a, w)


def _inproj_ab_kernel(a_ref, w_ref, ra_ref, rb_ref, o_ref):
    acc = jnp.dot(a_ref[...], w_ref[...], preferred_element_type=F32)
    n = acc.shape[1]
    o_ref[:, :n - LANES] = acc[:, :n - LANES].astype(o_ref.dtype)
    last = acc[:, n - LANES:]
    roped = last * ra_ref[...] + pltpu.roll(last, LANES // 2, 1) * rb_ref[...]
    o_ref[:, n - LANES:] = roped.astype(o_ref.dtype)


def inproj_ab(h, w, rope_a, rope_b, seq):
    m, k = h.shape
    n = w.shape[1]
    tm = 512
    nsb = seq // tm
    return pl.pallas_call(
        _inproj_ab_kernel,
        grid=(m // tm,),
        in_specs=[pl.BlockSpec((tm, k), lambda i: (i, 0)),
                  pl.BlockSpec((k, n), lambda i: (0, 0)),
                  pl.BlockSpec((tm, LANES), lambda i: (i % nsb, 0)),
                  pl.BlockSpec((tm, LANES), lambda i: (i % nsb, 0))],
        out_specs=pl.BlockSpec((tm, n), lambda i: (i, 0)),
        out_shape=jax.ShapeDtypeStruct((m, n), BF16),
        compiler_params=_cparams(("parallel",)),
        name="inproj_ab",
    )(h, w, rope_a, rope_b)


def _uq_kernel(a_ref, g_ref, w_ref, ra_ref, rb_ref, o_ref):
    an = _rms_rows(a_ref, g_ref)
    acc = jnp.dot(an, w_ref[...], preferred_element_type=F32)
    n = acc.shape[1]
    o_ref[...] = (acc * ra_ref[...] + pltpu.roll(acc, n - B_ROPE, 1) * rb_ref[...]).astype(o_ref.dtype)


def uq_proj(p_ab, col_block, g, w, rope_a, rope_b, seq):
    m = p_ab.shape[0]
    k, n = w.shape
    tm = 512
    nsb = seq // tm
    return pl.pallas_call(
        _uq_kernel,
        grid=(m // tm,),
        in_specs=[pl.BlockSpec((tm, k), lambda i: (i, col_block)),
                  pl.BlockSpec((1, k), lambda i: (0, 0)),
                  pl.BlockSpec((k, n), lambda i: (0, 0)),
                  pl.BlockSpec((tm, n), lambda i: (i % nsb, 0)),
                  pl.BlockSpec((tm, n), lambda i: (i % nsb, 0))],
        out_specs=pl.BlockSpec((tm, n), lambda i: (i, 0)),
        out_shape=jax.ShapeDtypeStruct((m, n), BF16),
        compiler_params=_cparams(("parallel",)),
        name="uq_proj",
    )(p_ab, g, w, rope_a, rope_b)


def _ukv_kernel(a_ref, g_ref, w_ref, kr_ref, k_ref, v_ref):
    an = _rms_rows(a_ref, g_ref)
    acc = jnp.dot(an, w_ref[...], preferred_element_type=F32)
    kr = kr_ref[...]
    for h in range(B_HEADS):
        base = h * (B_NOPE + B_V_DIM)
        k_ref[:, 2 * h * LANES:(2 * h + 1) * LANES] = acc[:, base:base + B_NOPE].astype(k_ref.dtype)
        k_ref[:, (2 * h + 1) * LANES:(2 * h + 2) * LANES] = kr
        v_ref[:, h * B_V_DIM:(h + 1) * B_V_DIM] = acc[:, base + B_NOPE:base + B_NOPE + B_V_DIM].astype(v_ref.dtype)


def ukv_proj(p_ab, ckv_block, kr_block, g, w):
    m = p_ab.shape[0]
    k, n = w.shape
    tm = 512
    return pl.pallas_call(
        _ukv_kernel,
        grid=(m // tm,),
        in_specs=[pl.BlockSpec((tm, k), lambda i: (i, ckv_block)),
                  pl.BlockSpec((1, k), lambda i: (0, 0)),
                  pl.BlockSpec((k, n), lambda i: (0, 0)),
                  pl.BlockSpec((tm, LANES), lambda i: (i, kr_block))],
        out_specs=[pl.BlockSpec((tm, B_HEADS * 2 * LANES), lambda i: (i, 0)),
                   pl.BlockSpec((tm, B_HEADS * B_V_DIM), lambda i: (i, 0))],
        out_shape=[jax.ShapeDtypeStruct((m, B_HEADS * 2 * LANES), BF16),
                   jax.ShapeDtypeStruct((m, B_HEADS * B_V_DIM), BF16)],
        compiler_params=_cparams(("parallel",)),
        name="ukv_proj",
    )(p_ab, g, w, p_ab)


_NT = (((1,), (1,)), ((), ()))


def _online_softmax_step(s, v, m_ref, l_ref, acc_ref):
    m_prev = m_ref[...]
    m_new = jnp.maximum(m_prev, jnp.max(s, axis=1, keepdims=True))
    alpha = jnp.exp(m_prev - m_new)
    p = jnp.exp(s - m_new)
    l_ref[...] = alpha * l_ref[...] + jnp.sum(p, axis=1, keepdims=True)
    acc_ref[...] = alpha * acc_ref[...] + jnp.dot(p.astype(v.dtype), v, preferred_element_type=F32)
    m_ref[...] = m_new


def _flash_diff_kernel(slopes_ref, q_ref, k_ref, v_ref, lam_ref, g_ref, o_ref, m_sc, l_sc, acc_sc,
                       *, tk, lam_init):
    tq = q_ref.shape[0]
    seq = k_ref.shape[0]
    slope = slopes_ref[pl.program_id(1)]
    q0 = pl.program_id(2) * tq
    q = q_ref[...]
    lane = lax.broadcasted_iota(jnp.int32, q.shape, 1)
    zero = jnp.zeros_like(q)
    qm = (jnp.where(lane < A_QK_DIM, q, zero), jnp.where(lane >= A_QK_DIM, q, zero))
    m_sc[...] = jnp.full(m_sc.shape, NEG_INF, F32)
    l_sc[...] = jnp.zeros(l_sc.shape, F32)
    acc_sc[...] = jnp.zeros(acc_sc.shape, F32)
    rel = (lax.broadcasted_iota(jnp.int32, (tq, tk), 0) - lax.broadcasted_iota(jnp.int32, (tq, tk), 1))

    def body(c, carry):
        k0 = pl.multiple_of(c * tk, tk)
        kc = k_ref[pl.ds(k0, tk), :]
        vc = v_ref[pl.ds(k0, tk), :]
        bias = slope * jnp.abs(rel + (q0 - k0)).astype(F32)
        for mp in range(2):
            s = lax.dot_general(qm[mp], kc, _NT, preferred_element_type=F32) - bias
            _online_softmax_step(s, vc, m_sc.at[mp], l_sc.at[mp], acc_sc.at[mp])
        return carry

    lax.fori_loop(0, seq // tk, body, 0)
    o0 = acc_sc[0] / l_sc[0]
    o1 = acc_sc[1] / l_sc[1]
    lam = lam_ref[...]
    lam_full = (jnp.exp(jnp.sum(lam[0:1] * lam[1:2], axis=1, keepdims=True))
                - jnp.exp(jnp.sum(lam[2:3] * lam[3:4], axis=1, keepdims=True)) + lam_init)
    o = o0 - lam_full * o1
    on = o * lax.rsqrt(jnp.mean(o * o, axis=-1, keepdims=True) + RMS_EPS) * g_ref[...]
    o_ref[...] = (on * (1.0 - lam_init)).astype(o_ref.dtype)


def flash_diff(p_ab, slopes, lam, diff_g, lam_init, bsz, seq):
    tq, tk = 512, 512
    nh = A_HEADS
    return pl.pallas_call(
        functools.partial(_flash_diff_kernel, tk=tk, lam_init=lam_init),
        grid=(bsz, nh, seq // tq),
        in_specs=[pl.BlockSpec(memory_space=pltpu.SMEM),
                  pl.BlockSpec((None, tq, LANES), lambda b, h, i: (b, i, h)),
                  pl.BlockSpec((None, seq, LANES), lambda b, h, i: (b, 0, nh + h)),
                  pl.BlockSpec((None, seq, LANES), lambda b, h, i: (b, 0, 2 * nh + h)),
                  pl.BlockSpec((4, A_QK_DIM), lambda b, h, i: (0, 0)),
                  pl.BlockSpec((1, A_V_DIM), lambda b, h, i: (0, 0))],
        out_specs=pl.BlockSpec((None, tq, A_V_DIM), lambda b, h, i: (b, i, h)),
        out_shape=jax.ShapeDtypeStruct((bsz, seq, nh * A_V_DIM), BF16),
        scratch_shapes=[pltpu.VMEM((2, tq, 1), F32), pltpu.VMEM((2, tq, 1), F32),
                        pltpu.VMEM((2, tq, A_V_DIM), F32)],
        compiler_params=_cparams(("parallel", "parallel", "parallel")),
        name="flash_diff",
    )(slopes, p_ab, p_ab, p_ab, lam, diff_g)


def _flash_mla_kernel(q_ref, k_ref, v_ref, o_ref, m_sc, l_sc, acc_sc, *, tk):
    seq = k_ref.shape[0]
    q = q_ref[...]
    m_sc[...] = jnp.full(m_sc.shape, NEG_INF, F32)
    l_sc[...] = jnp.zeros(l_sc.shape, F32)
    acc_sc[...] = jnp.zeros(acc_sc.shape, F32)

    def body(c, carry):
        k0 = pl.multiple_of(c * tk, tk)
        s = lax.dot_general(q, k_ref[pl.ds(k0, tk), :], _NT, preferred_element_type=F32)
        _online_softmax_step(s, v_ref[pl.ds(k0, tk), :], m_sc, l_sc, acc_sc)
        return carry

    lax.fori_loop(0, seq // tk, body, 0)
    o_ref[...] = (acc_sc[...] / l_sc[...]).astype(o_ref.dtype)


def flash_mla(q_cat, k_cat, v_b, bsz, seq):
    tq, tk = 512, 512
    kd = 2 * LANES
    return pl.pallas_call(
        functools.partial(_flash_mla_kernel, tk=tk),
        grid=(bsz, B_HEADS, seq // tq),
        in_specs=[pl.BlockSpec((None, tq, kd), lambda b, h, i: (b, i, h)),
                  pl.BlockSpec((None, seq, kd), lambda b, h, i: (b, 0, h)),
                  pl.BlockSpec((None, seq, B_V_DIM), lambda b, h, i: (b, 0, h))],
        out_specs=pl.BlockSpec((None, tq, B_V_DIM), lambda b, h, i: (b, i, h)),
        out_shape=jax.ShapeDtypeStruct((bsz, seq, B_HEADS * B_V_DIM), BF16),
        scratch_shapes=[pltpu.VMEM((tq, 1), F32), pltpu.VMEM((tq, 1), F32),
                        pltpu.VMEM((tq, B_V_DIM), F32)],
        compiler_params=_cparams(("parallel", "parallel", "parallel")),
        name="flash_mla",
    )(q_cat, k_cat, v_b)


def _window_kernel(slopes_ref, sink_ref, q_ref, k_ref, v_ref, o_ref):
    tq = q_ref.shape[0]
    seq = k_ref.shape[0]
    blk = WINDOW
    kw = 3 * blk
    n = pl.program_id(1)
    q0 = pl.program_id(2) * tq
    rel = (lax.broadcasted_iota(jnp.int32, (blk, kw), 1) - lax.broadcasted_iota(jnp.int32, (blk, kw), 0))
    for sb in range(tq // blk):
        t0 = q0 + sb * blk
        start = pl.multiple_of(jnp.clip(t0 - blk, 0, seq - kw), blk)
        kc = k_ref[pl.ds(start, kw), :]
        vc = v_ref[pl.ds(start, kw), :]
        dist = jnp.abs(rel + (start - t0))
        valid = dist <= WINDOW
        distf = dist.astype(F32)
        for g in range(C_GROUP):
            head = n * C_GROUP + g
            qg = q_ref[sb * blk:(sb + 1) * blk, g * C_HEAD_DIM:(g + 1) * C_HEAD_DIM]
            s = lax.dot_general(qg, kc, _NT, preferred_element_type=F32) - slopes_ref[head] * distf
            s = jnp.where(valid, s, NEG_INF)
            sink = sink_ref[head]
            m = jnp.maximum(jnp.max(s, axis=1, keepdims=True), sink)
            e = jnp.exp(s - m)
            p = e / (jnp.sum(e, axis=1, keepdims=True) + jnp.exp(sink - m))
            o = jnp.dot(p.astype(vc.dtype), vc, preferred_element_type=F32)
            o_ref[sb * blk:(sb + 1) * blk, g * C_HEAD_DIM:(g + 1) * C_HEAD_DIM] = o.astype(o_ref.dtype)


def window_attn(p_c, slopes, sink, bsz, seq):
    tq = 512
    gw = C_GROUP * C_HEAD_DIM
    nq = C_HEADS * C_HEAD_DIM // LANES
    return pl.pallas_call(
        _window_kernel,
        grid=(bsz, C_KV_HEADS, seq // tq),
        in_specs=[pl.BlockSpec(memory_space=pltpu.SMEM),
                  pl.BlockSpec(memory_space=pltpu.SMEM),
                  pl.BlockSpec((None, tq, gw), lambda b, n, i: (b, i, n)),
                  pl.BlockSpec((None, seq, C_HEAD_DIM), lambda b, n, i: (b, 0, nq + n)),
                  pl.BlockSpec((None, seq, C_HEAD_DIM), lambda b, n, i: (b, 0, nq + C_KV_HEADS + n))],
        out_specs=pl.BlockSpec((None, tq, gw), lambda b, n, i: (b, i, n)),
        out_shape=jax.ShapeDtypeStruct((bsz, seq, C_HEADS * C_HEAD_DIM), BF16),
        compiler_params=_cparams(("parallel", "parallel", "parallel")),
        name="window_attn",
    )(slopes, sink, p_c, p_c, p_c)


def _swiglu_step(x, wg_ref, wu_ref, wd_ref, acc_ref):
    g = jnp.dot(x, wg_ref[...], preferred_element_type=F32)
    u = jnp.dot(x, wu_ref[...], preferred_element_type=F32)
    a = (g * jax.nn.sigmoid(g) * u).astype(BF16)
    acc_ref[...] += jnp.dot(a, wd_ref[...], preferred_element_type=F32)


def _ffn_kernel(x_ref, wg_ref, wu_ref, wd_ref, o_ref, acc_ref):
    j = pl.program_id(1)

    @pl.when(j == 0)
    def _():
        acc_ref[...] = jnp.zeros(acc_ref.shape, F32)

    _swiglu_step(x_ref[...], wg_ref, wu_ref, wd_ref, acc_ref)

    @pl.when(j == pl.num_programs(1) - 1)
    def _():
        o_ref[...] = acc_ref[...].astype(o_ref.dtype)


def ffn(h, wg, wu, wd):
    m, d = h.shape
    f = wg.shape[1]
    tm, tf = 512, 512
    return pl.pallas_call(
        _ffn_kernel,
        grid=(m // tm, f // tf),
        in_specs=[pl.BlockSpec((tm, d), lambda i, j: (i, 0)),
                  pl.BlockSpec((d, tf), lambda i, j: (0, j)),
                  pl.BlockSpec((d, tf), lambda i, j: (0, j)),
                  pl.BlockSpec((tf, d), lambda i, j: (j, 0))],
        out_specs=pl.BlockSpec((tm, d), lambda i, j: (i, 0)),
        out_shape=jax.ShapeDtypeStruct((m, d), F32),
        scratch_shapes=[pltpu.VMEM((tm, d), F32)],
        compiler_params=_cparams(("parallel", "arbitrary")),
        name="ffn",
    )(h, wg, wu, wd)


def _moe_ffn_kernel(be_ref, x_ref, gate_ref, wg_ref, wu_ref, wd_ref, o_ref, xb_ref, acc_ref):
    del be_ref
    j = pl.program_id(1)

    @pl.when(j == 0)
    def _():
        acc_ref[...] = jnp.zeros(acc_ref.shape, F32)
        xb_ref[...] = x_ref[...].astype(BF16)

    _swiglu_step(xb_ref[...], wg_ref, wu_ref, wd_ref, acc_ref)

    @pl.when(j == pl.num_programs(1) - 1)
    def _():
        o_ref[...] = acc_ref[...] * gate_ref[:, 0:1]


def moe_ffn(block_e, xs, row_gate, wg, wu, wd):
    r, d = xs.shape
    f = wg.shape[2]
    tm, tf = MOE_BLOCK, 256
    grid_spec = pltpu.PrefetchScalarGridSpec(
        num_scalar_prefetch=1,
        grid=(r // tm, f // tf),
        in_specs=[pl.BlockSpec((tm, d), lambda i, j, be: (i, 0)),
                  pl.BlockSpec((tm, LANES), lambda i, j, be: (i, 0)),
                  pl.BlockSpec((None, d, tf), lambda i, j, be: (be[i], 0, j)),
                  pl.BlockSpec((None, d, tf), lambda i, j, be: (be[i], 0, j)),
                  pl.BlockSpec((None, tf, d), lambda i, j, be: (be[i], j, 0))],
        out_specs=pl.BlockSpec((tm, d), lambda i, j, be: (i, 0)),
        scratch_shapes=[pltpu.VMEM((tm, d), BF16), pltpu.VMEM((tm, d), F32)],
    )
    return pl.pallas_call(
        _moe_ffn_kernel,
        grid_spec=grid_spec,
        out_shape=jax.ShapeDtypeStruct((r, d), F32),
        compiler_params=_cparams(("parallel", "arbitrary")),
        name="moe_ffn",
    )(block_e, xs, row_gate, wg, wu, wd)


def _router_kernel(h_ref, w_ref, b_ref, o_ref):
    logits = jnp.dot(h_ref[...], w_ref[...], preferred_element_type=F32,
                     precision=lax.Precision.HIGHEST) + b_ref[...]
    lane = lax.broadcasted_iota(jnp.int32, logits.shape, 1)
    logits = jnp.where(lane < N_EXPERTS, logits, NEG_INF)
    m1 = jnp.max(logits, axis=1, keepdims=True)
    i1 = jnp.min(jnp.where(logits == m1, lane, LANES), axis=1, keepdims=True)
    rest = jnp.where(lane == i1, NEG_INF, logits)
    m2 = jnp.max(rest, axis=1, keepdims=True)
    i2 = jnp.min(jnp.where(rest == m2, lane, LANES), axis=1, keepdims=True)
    e2 = jnp.exp(m2 - m1)
    g1 = 1.0 / (1.0 + e2)
    g2 = e2 / (1.0 + e2)
    out = jnp.where(lane == 0, i1.astype(F32),
                    jnp.where(lane == 1, i2.astype(F32),
                              jnp.where(lane == 2, g1, jnp.where(lane == 3, g2, 0.0))))
    o_ref[...] = out


def router(h, w_pad, b_pad):
    m, d = h.shape
    tm = 512
    return pl.pallas_call(
        _router_kernel,
        grid=(m // tm,),
        in_specs=[pl.BlockSpec((tm, d), lambda i: (i, 0)),
                  pl.BlockSpec((d, LANES), lambda i: (0, 0)),
                  pl.BlockSpec((1, LANES), lambda i: (0, 0))],
        out_specs=pl.BlockSpec((tm, LANES), lambda i: (i, 0)),
        out_shape=jax.ShapeDtypeStruct((m, LANES), F32),
        compiler_params=_cparams(("parallel",)),
        name="router",
    )(h, w_pad, b_pad)


def _gather_rows_kernel(idx_ref, src_ref, dst_ref, sem, *, rows):
    base = pl.program_id(0) * rows

    def issue(r, carry):
        pltpu.make_async_copy(src_ref.at[pl.ds(idx_ref[0, r], 1)],
                              dst_ref.at[pl.ds(base + r, 1)], sem).start()
        return carry

    lax.fori_loop(0, rows, issue, 0)
    pltpu.make_async_copy(src_ref.at[pl.ds(0, rows)], dst_ref.at[pl.ds(base, rows)], sem).wait()


def gather_rows(src, idx):
    rows = 512
    r = idx.shape[0]
    d = src.shape[1]
    idx3 = idx.reshape(r // rows, 1, rows)
    return pl.pallas_call(
        functools.partial(_gather_rows_kernel, rows=rows),
        grid=(r // rows,),
        in_specs=[pl.BlockSpec((None, 1, rows), lambda i: (i, 0, 0), memory_space=pltpu.SMEM),
                  pl.BlockSpec(memory_space=pl.ANY)],
        out_specs=pl.BlockSpec(memory_space=pl.ANY),
        out_shape=jax.ShapeDtypeStruct((r, d), src.dtype),
        scratch_shapes=[pltpu.SemaphoreType.DMA(())],
        compiler_params=_cparams(("arbitrary",)),
        name="gather_rows",
    )(idx3, src)


def _alibi_slopes(n_heads):
    return jnp.asarray(2.0 ** (-8.0 * np.arange(1, n_heads + 1) / n_heads), dtype=F32)


def _rope_tables(seq):
    inv_freq = ROPE_THETA ** (-jnp.arange(0, B_ROPE, 2, dtype=F32) / B_ROPE)
    ang = jnp.arange(seq, dtype=F32)[:, None] * inv_freq[None, :]
    cos, sin = jnp.cos(ang), jnp.sin(ang)
    cos2 = jnp.concatenate([cos, cos], axis=-1)
    sin2 = jnp.concatenate([sin, sin], axis=-1)
    z64 = jnp.zeros((seq, B_ROPE), F32)
    ka = jnp.concatenate([cos2, z64], axis=-1)
    kb = jnp.concatenate([sin2, z64], axis=-1)
    ones = jnp.ones((seq, B_NOPE), F32)
    z128 = jnp.zeros((seq, B_NOPE), F32)
    qa = jnp.tile(jnp.concatenate([ones, cos2, z64], axis=-1), (1, B_HEADS))
    qb = jnp.tile(jnp.concatenate([z128, sin2, z64], axis=-1), (1, B_HEADS))
    return ka, kb, qa, qb


def _rot_cols(w):
    half = w.shape[-1] // 2
    return jnp.concatenate([-w[..., half:], w[..., :half]], axis=-1)


def _routing_tables(flat_e, flat_g, n_tok):
    n_assign = flat_e.shape[0]
    onehot = (flat_e[:, None] == jnp.arange(N_EXPERTS, dtype=jnp.int32)[None, :]).astype(jnp.int32)
    csum = jnp.cumsum(onehot, axis=0)
    rank = jnp.sum((csum - onehot) * onehot, axis=1)
    sizes = csum[-1]
    padded = ((sizes + MOE_BLOCK - 1) // MOE_BLOCK) * MOE_BLOCK
    pad_end = jnp.cumsum(padded)
    pad_start = pad_end - padded
    dest = (pad_start[flat_e] + rank).astype(jnp.int32)
    n_blocks = -(-n_assign // MOE_BLOCK) + N_EXPERTS
    n_rows = n_blocks * MOE_BLOCK
    tok = jnp.arange(n_assign, dtype=jnp.int32) // TOP_K
    row_tok = jnp.zeros((n_rows,), jnp.int32).at[dest].set(tok)
    row_gate = jnp.zeros((n_rows,), F32).at[dest].set(flat_g)
    block_start = jnp.arange(n_blocks, dtype=jnp.int32) * MOE_BLOCK
    block_e = jnp.minimum(jnp.searchsorted(pad_end, block_start, side='right'), N_EXPERTS - 1).astype(jnp.int32)
    return dest, row_tok, row_gate, block_e


def kernel(x, c, ada_w, ada_b, ln_g, ln_b, ab_w_in, ab_lam, ab_diff_g, ab_q_norm_g, ab_kv_norm_g, ab_w_uq,
           ab_w_ukv, ab_w_out, c_w_in, c_sink, c_w_out, ffn_w_gate, ffn_w_up, ffn_w_down, moe_w_router,
           moe_b_router, moe_w_gate, moe_w_up, moe_w_down):
    bsz, seq, d = x.shape
    depth = ada_w.shape[0]
    n_tok = bsz * seq
    alpha = (2 * depth) ** 0.25

    c_pad = jnp.zeros((8, d), F32).at[:bsz].set(c)
    mods = adaln_all(c_pad, ada_w.reshape(2 * depth, d, 3 * d), ada_b.reshape(2 * depth, 1, 3 * d))
    mods = mods.reshape(2 * depth, 8, 3, d)

    ka, kb, qa, qb = _rope_tables(seq)
    slopes_a = _alibi_slopes(A_HEADS)
    slopes_c = _alibi_slopes(C_HEADS)
    scale_a = A_QK_DIM ** -0.5
    scale_b = (B_NOPE + B_ROPE) ** -0.5
    scale_c = C_HEAD_DIM ** -0.5

    h = modulate(x, mods, 0, BF16)
    for layer in range(depth):
        i = layer // 2
        l2 = 2 * layer
        last = layer == depth - 1
        if layer % 2 == 0:
            lam_init = 0.8 - 0.6 * math.exp(-0.3 * layer)
            w_in = ab_w_in[i]
            w_kr = w_in[:, 3 * A_COLS + B_Q_LORA + B_KV_LORA:]
            w_in2 = jnp.concatenate([w_in[:, :A_COLS] * scale_a, w_in[:, A_COLS:], _rot_cols(w_kr)],
                                    axis=1).astype(BF16)
            p_ab = inproj_ab(h.reshape(n_tok, d), w_in2, ka, kb, seq)
            o_a = flash_diff(p_ab.reshape(bsz, seq, -1), slopes_a, ab_lam[i], ab_diff_g[i].reshape(1, -1),
                             lam_init, bsz, seq)
            w_uq = ab_w_uq[i].reshape(B_Q_LORA, B_HEADS, B_NOPE + B_ROPE) * scale_b
            w_uq2 = jnp.concatenate([w_uq, _rot_cols(w_uq[..., B_NOPE:])], axis=-1)
            w_uq2 = w_uq2.reshape(B_Q_LORA, B_HEADS * 2 * LANES).astype(BF16)
            q_cat = uq_proj(p_ab, 3 * A_COLS // B_Q_LORA, ab_q_norm_g[i].reshape(1, -1), w_uq2, qa, qb, seq)
            k_cat, v_b = ukv_proj(p_ab, (3 * A_COLS + B_Q_LORA) // B_KV_LORA,
                                  (3 * A_COLS + B_Q_LORA + B_KV_LORA) // LANES,
                                  ab_kv_norm_g[i].reshape(1, -1), ab_w_ukv[i].astype(BF16))
            o_b = flash_mla(q_cat.reshape(bsz, seq, -1), k_cat.reshape(bsz, seq, -1),
                            v_b.reshape(bsz, seq, -1), bsz, seq)
            o_cat = jnp.concatenate([o_a, o_b], axis=-1).reshape(n_tok, -1)
            y = matmul(o_cat, ab_w_out[i].astype(BF16), F32)
        else:
            w_in = c_w_in[i]
            nqc = C_HEADS * C_HEAD_DIM
            w_in2 = jnp.concatenate([w_in[:, :nqc] * scale_c, w_in[:, nqc:]], axis=1).astype(BF16)
            p_c = matmul(h.reshape(n_tok, d), w_in2, BF16, tn=1024)
            o_c = window_attn(p_c.reshape(bsz, seq, -1), slopes_c, c_sink[i].astype(F32), bsz, seq)
            y = matmul(o_c.reshape(n_tok, -1), c_w_out[i].astype(BF16), F32)
        moe_next = layer % 2 == 1
        x, h = postnorm(x, y.reshape(bsz, seq, d), mods, l2, ln_g[layer, 0], ln_b[layer, 0], alpha,
                        next_dtype=F32 if moe_next else BF16)

        if not moe_next:
            y = ffn(h.reshape(n_tok, d), ffn_w_gate[i].astype(BF16), ffn_w_up[i].astype(BF16),
                    ffn_w_down[i].astype(BF16))
            pair = False
        else:
            hf = h.reshape(n_tok, d)
            w_r = jnp.zeros((d, LANES), F32).at[:, :N_EXPERTS].set(moe_w_router[i])
            b_r = jnp.zeros((1, LANES), F32).at[0, :N_EXPERTS].set(moe_b_router[i])
            route = router(hf, w_r, b_r)
            flat_e = route[:, :TOP_K].astype(jnp.int32).reshape(-1)
            flat_g = route[:, TOP_K:2 * TOP_K].reshape(-1)
            dest, row_tok, row_gate, block_e = _routing_tables(flat_e, flat_g, n_tok)
            xs = gather_rows(hf, row_tok)
            gate_rep = jnp.broadcast_to(row_gate[:, None], (row_gate.shape[0], LANES))
            ys = moe_ffn(block_e, xs, gate_rep, moe_w_gate[i].astype(BF16), moe_w_up[i].astype(BF16),
                         moe_w_down[i].astype(BF16))
            y = gather_rows(ys, dest).reshape(n_tok, TOP_K * d)
            pair = True
        x, h = postnorm(x, y.reshape(bsz, seq, -1), mods, l2 + 1, ln_g[layer, 1], ln_b[layer, 1], alpha,
                        pair_sum=pair, next_dtype=None if last else BF16)
    return x
```

```python
import functools
import math

import numpy as np
import jax
import jax.numpy as jnp
from jax import lax
from jax.experimental import pallas as pl
from jax.experimental.pallas import tpu as pltpu

F32 = jnp.float32
BF16 = jnp.bfloat16

A_HEADS = 4
A_QK_DIM = 64
A_V_DIM = 128
B_HEADS = 4
B_Q_LORA = 512
B_KV_LORA = 256
B_NOPE = 128
B_ROPE = 64
B_V_DIM = 128
ROPE_THETA = 10000.0
C_HEADS = 16
C_KV_HEADS = 4
C_GROUP = C_HEADS // C_KV_HEADS
C_HEAD_DIM = 128
WINDOW = 128
N_EXPERTS = 8
TOP_K = 2
MOE_BLOCK = 512
LN_EPS = 1e-5
RMS_EPS = 1e-6

A_COLS = A_HEADS * 2 * A_QK_DIM
LANES = 128
VMEM_LIMIT = 56 * 1024 * 1024
NEG_INF = float("-inf")
LOG2E = math.log2(math.e)


def _cparams(sem, vmem=VMEM_LIMIT):
    return pltpu.CompilerParams(dimension_semantics=sem, vmem_limit_bytes=vmem)


def _adaln_kernel(c_ref, w_ref, b_ref, o_ref):
    c = c_ref[...]
    ca = c * jax.nn.sigmoid(c)
    o_ref[...] = jnp.dot(ca, w_ref[...], preferred_element_type=F32,
                         precision=lax.Precision.HIGHEST) + b_ref[...]


def adaln_all(c_pad, ada_w, ada_b):
    l2, d, n = ada_w.shape
    tn = 768
    return pl.pallas_call(
        _adaln_kernel,
        grid=(l2, n // tn),
        in_specs=[pl.BlockSpec((8, d), lambda l, j: (0, 0)),
                  pl.BlockSpec((None, d, tn), lambda l, j: (l, 0, j)),
                  pl.BlockSpec((None, 1, tn), lambda l, j: (l, 0, j))],
        out_specs=pl.BlockSpec((None, 8, tn), lambda l, j: (l, 0, j)),
        out_shape=jax.ShapeDtypeStruct((l2, 8, n), F32),
        compiler_params=_cparams(("parallel", "parallel")),
        name="adaln",
    )(c_pad, ada_w, ada_b)


def _modulate_kernel(x_ref, mod_ref, h_ref):
    x = x_ref[...]
    h_ref[...] = (x * (1.0 + mod_ref[1:2, :]) + mod_ref[0:1, :]).astype(h_ref.dtype)


def modulate(x, mods, l2, out_dtype):
    b, s, d = x.shape
    ts = 512
    return pl.pallas_call(
        _modulate_kernel,
        grid=(b, s // ts),
        in_specs=[pl.BlockSpec((None, ts, d), lambda i, j: (i, j, 0)),
                  pl.BlockSpec((None, None, 3, d), lambda i, j: (l2, i, 0, 0))],
        out_specs=pl.BlockSpec((None, ts, d), lambda i, j: (i, j, 0)),
        out_shape=jax.ShapeDtypeStruct((b, s, d), out_dtype),
        compiler_params=_cparams(("parallel", "parallel")),
        name="modulate",
    )(x, mods)


def _postnorm_kernel(*refs, alpha, pair_sum, has_next):
    if has_next:
        x_ref, y_ref, modc_ref, g_ref, b_ref, modn_ref, xo_ref, h_ref = refs
    else:
        x_ref, y_ref, modc_ref, g_ref, b_ref, xo_ref = refs
    x = x_ref[...]
    d = x.shape[-1]
    if pair_sum:
        y = y_ref[:, :d].astype(F32) + y_ref[:, d:].astype(F32)
    else:
        y = y_ref[...].astype(F32)
    z = alpha * x + (1.0 + modc_ref[2:3, :]) * y
    mu = jnp.mean(z, axis=-1, keepdims=True)
    zc = z - mu
    var = jnp.mean(zc * zc, axis=-1, keepdims=True)
    xn = zc * lax.rsqrt(var + LN_EPS) * g_ref[...] + b_ref[...]
    xo_ref[...] = xn
    if has_next:
        h_ref[...] = (xn * (1.0 + modn_ref[1:2, :]) + modn_ref[0:1, :]).astype(h_ref.dtype)


def postnorm(x, y, mods, l2, ln_g, ln_b, alpha, *, pair_sum=False, next_dtype=None):
    b, s, d = x.shape
    ts = 256
    has_next = next_dtype is not None
    yw = y.shape[-1]
    in_specs = [pl.BlockSpec((None, ts, d), lambda i, j: (i, j, 0)),
                pl.BlockSpec((None, ts, yw), lambda i, j: (i, j, 0)),
                pl.BlockSpec((None, None, 3, d), lambda i, j: (l2, i, 0, 0)),
                pl.BlockSpec((1, d), lambda i, j: (0, 0)),
                pl.BlockSpec((1, d), lambda i, j: (0, 0))]
    args = [x, y, mods, ln_g.reshape(1, d), ln_b.reshape(1, d)]
    out_specs = [pl.BlockSpec((None, ts, d), lambda i, j: (i, j, 0))]
    out_shape = [jax.ShapeDtypeStruct((b, s, d), F32)]
    if has_next:
        in_specs.append(pl.BlockSpec((None, None, 3, d), lambda i, j: (l2 + 1, i, 0, 0)))
        args.append(mods)
        out_specs.append(pl.BlockSpec((None, ts, d), lambda i, j: (i, j, 0)))
        out_shape.append(jax.ShapeDtypeStruct((b, s, d), next_dtype))
    out = pl.pallas_call(
        functools.partial(_postnorm_kernel, alpha=alpha, pair_sum=pair_sum, has_next=has_next),
        grid=(b, s // ts),
        in_specs=in_specs,
        out_specs=out_specs,
        out_shape=out_shape,
        compiler_params=_cparams(("parallel", "parallel")),
        name="postnorm",
    )(*args)
    return (out[0], out[1]) if has_next else (out[0], None)


def _rms_rows(a_ref, g_ref):
    a = a_ref[...].astype(F32)
    return (a * lax.rsqrt(jnp.mean(a * a, axis=-1, keepdims=True) + RMS_EPS) * g_ref[...]).astype(BF16)


def _matmul_kernel(a_ref, w_ref, o_ref):
    o_ref[...] = jnp.dot(a_ref[...], w_ref[...], preferred_element_type=F32).astype(o_ref.dtype)


def matmul(a, w, out_dtype, tm=512, tn=None):
    m, k = a.shape
    n = w.shape[1]
    tn = n if tn is None else tn
    return pl.pallas_call(
        _matmul_kernel,
        grid=(m // tm, n // tn),
        in_specs=[pl.BlockSpec((tm, k), lambda i, j: (i, 0)),
                  pl.BlockSpec((k, tn), lambda i, j: (0, j))],
        out_specs=pl.BlockSpec((tm, tn), lambda i, j: (i, j)),
        out_shape=jax.ShapeDtypeStruct((m, n), out_dtype),
        compiler_params=_cparams(("parallel", "parallel")),
        name="matmul",
    )(a, w)


def _inproj_ab_kernel(a_ref, w_ref, ra_ref, rb_ref, o_ref):
    acc = jnp.dot(a_ref[...], w_ref[...], preferred_element_type=F32)
    n = acc.shape[1]
    o_ref[:, :n - LANES] = acc[:, :n - LANES].astype(o_ref.dtype)
    last = acc[:, n - LANES:]
    roped = last * ra_ref[...] + pltpu.roll(last, LANES // 2, 1) * rb_ref[...]
    o_ref[:, n - LANES:] = roped.astype(o_ref.dtype)


def inproj_ab(h, w, rope_a, rope_b, seq):
    m, k = h.shape
    n = w.shape[1]
    tm = 512
    nsb = seq // tm
    return pl.pallas_call(
        _inproj_ab_kernel,
        grid=(m // tm,),
        in_specs=[pl.BlockSpec((tm, k), lambda i: (i, 0)),
                  pl.BlockSpec((k, n), lambda i: (0, 0)),
                  pl.BlockSpec((tm, LANES), lambda i: (i % nsb, 0)),
                  pl.BlockSpec((tm, LANES), lambda i: (i % nsb, 0))],
        out_specs=pl.BlockSpec((tm, n), lambda i: (i, 0)),
        out_shape=jax.ShapeDtypeStruct((m, n), BF16),
        compiler_params=_cparams(("parallel",)),
        name="inproj_ab",
    )(h, w, rope_a, rope_b)


def _uq_kernel(a_ref, g_ref, w_ref, ra_ref, rb_ref, o_ref):
    an = _rms_rows(a_ref, g_ref)
    acc = jnp.dot(an, w_ref[...], preferred_element_type=F32)
    n = acc.shape[1]
    o_ref[...] = (acc * ra_ref[...] + pltpu.roll(acc, n - B_ROPE, 1) * rb_ref[...]).astype(o_ref.dtype)


def uq_proj(p_ab, col_block, g, w, rope_a, rope_b, seq):
    m = p_ab.shape[0]
    k, n = w.shape
    tm = 512
    nsb = seq // tm
    return pl.pallas_call(
        _uq_kernel,
        grid=(m // tm,),
        in_specs=[pl.BlockSpec((tm, k), lambda i: (i, col_block)),
                  pl.BlockSpec((1, k), lambda i: (0, 0)),
                  pl.BlockSpec((k, n), lambda i: (0, 0)),
                  pl.BlockSpec((tm, n), lambda i: (i % nsb, 0)),
                  pl.BlockSpec((tm, n), lambda i: (i % nsb, 0))],
        out_specs=pl.BlockSpec((tm, n), lambda i: (i, 0)),
        out_shape=jax.ShapeDtypeStruct((m, n), BF16),
        compiler_params=_cparams(("parallel",)),
        name="uq_proj",
    )(p_ab, g, w, rope_a, rope_b)


def _ukv_kernel(a_ref, g_ref, w_ref, kr_ref, k_ref, v_ref):
    an = _rms_rows(a_ref, g_ref)
    acc = jnp.dot(an, w_ref[...], preferred_element_type=F32)
    kr = kr_ref[...]
    for h in range(B_HEADS):
        base = h * (B_NOPE + B_V_DIM)
        k_ref[:, 2 * h * LANES:(2 * h + 1) * LANES] = acc[:, base:base + B_NOPE].astype(k_ref.dtype)
        k_ref[:, (2 * h + 1) * LANES:(2 * h + 2) * LANES] = kr
        v_ref[:, h * B_V_DIM:(h + 1) * B_V_DIM] = acc[:, base + B_NOPE:base + B_NOPE + B_V_DIM].astype(v_ref.dtype)


def ukv_proj(p_ab, ckv_block, kr_block, g, w):
    m = p_ab.shape[0]
    k, n = w.shape
    tm = 512
    return pl.pallas_call(
        _ukv_kernel,
        grid=(m // tm,),
        in_specs=[pl.BlockSpec((tm, k), lambda i: (i, ckv_block)),
                  pl.BlockSpec((1, k), lambda i: (0, 0)),
                  pl.BlockSpec((k, n), lambda i: (0, 0)),
                  pl.BlockSpec((tm, LANES), lambda i: (i, kr_block))],
        out_specs=[pl.BlockSpec((tm, B_HEADS * 2 * LANES), lambda i: (i, 0)),
                   pl.BlockSpec((tm, B_HEADS * B_V_DIM), lambda i: (i, 0))],
        out_shape=[jax.ShapeDtypeStruct((m, B_HEADS * 2 * LANES), BF16),
                   jax.ShapeDtypeStruct((m, B_HEADS * B_V_DIM), BF16)],
        compiler_params=_cparams(("parallel",)),
        name="ukv_proj",
    )(p_ab, g, w, p_ab)


_NT = (((1,), (1,)), ((), ()))


def _ones_column(rows):
    lane = lax.broadcasted_iota(jnp.int32, (rows, LANES), 1)
    return jnp.where(lane == 0, 1.0, 0.0).astype(BF16)


def _online_softmax_step(s, v_aug, m_ref, acc_ref):
    tk = s.shape[1]
    m_prev = m_ref[...]
    m_new = jnp.maximum(m_prev, jnp.max(s, axis=1, keepdims=True))
    alpha = jnp.exp2(m_prev - m_new)
    p = jnp.exp2(s - jnp.concatenate([m_new] * (tk // LANES), axis=1))
    pv = jnp.dot(p.astype(v_aug.dtype), v_aug, preferred_element_type=F32)
    acc_ref[...] = jnp.concatenate([alpha, alpha], axis=1) * acc_ref[...] + pv
    m_ref[...] = m_new


def _softmax_result(acc_ref):
    acc = acc_ref[...]
    return acc[:, :LANES] / acc[:, LANES:LANES + 1]


def _flash_diff_kernel(slopes_ref, q_ref, eq_ref, k_ref, ek_ref, v_ref, lam_ref, g_ref, o_ref,
                       qaug_sc, m_sc, acc_sc, *, tk, lam_init):
    tq = q_ref.shape[0]
    seq = k_ref.shape[0]
    qi = pl.program_id(2)
    q0 = qi * tq
    q = q_ref[...]
    eq = eq_ref[...]
    lane = lax.broadcasted_iota(jnp.int32, q.shape, 1)
    zero = jnp.zeros_like(q)
    for mp in range(2):
        qm = jnp.where((lane < A_QK_DIM) == (mp == 0), q, zero)
        qaug_sc[0, mp] = jnp.concatenate([qm, -eq], axis=1)
        qaug_sc[1, mp] = jnp.concatenate([qm, eq], axis=1)
    m_sc[...] = jnp.full(m_sc.shape, NEG_INF, F32)
    acc_sc[...] = jnp.zeros(acc_sc.shape, F32)
    ones_col = _ones_column(tk)

    def chunk(c, side, diagonal):
        k0 = pl.multiple_of(c * tk, tk)
        kc = jnp.concatenate([k_ref[pl.ds(k0, tk), :], ek_ref[pl.ds(k0, tk), :]], axis=1)
        v_aug = jnp.concatenate([v_ref[pl.ds(k0, tk), :], ones_col], axis=1)
        if diagonal:
            rel = (lax.broadcasted_iota(jnp.int32, (tq, tk), 0)
                   - lax.broadcasted_iota(jnp.int32, (tq, tk), 1)) + (q0 - k0)
            corr = (2.0 * slopes_ref[pl.program_id(1)]) * jnp.maximum(rel, 0).astype(F32)
        for mp in range(2):
            s = lax.dot_general(qaug_sc[side, mp], kc, _NT, preferred_element_type=F32)
            if diagonal:
                s = s - corr
            _online_softmax_step(s, v_aug, m_sc.at[mp], acc_sc.at[mp])

    def left(c, carry):
        chunk(c, 0, False)
        return carry

    def right(c, carry):
        chunk(c, 1, False)
        return carry

    cd = q0 // tk
    lax.fori_loop(0, cd, left, 0)
    chunk(cd, 1, True)
    lax.fori_loop(cd + 1, seq // tk, right, 0)

    o0 = _softmax_result(acc_sc.at[0])
    o1 = _softmax_result(acc_sc.at[1])
    lam = lam_ref[...]
    lam_full = (jnp.exp(jnp.sum(lam[0:1] * lam[1:2], axis=1, keepdims=True))
                - jnp.exp(jnp.sum(lam[2:3] * lam[3:4], axis=1, keepdims=True)) + lam_init)
    o = o0 - lam_full * o1
    on = o * lax.rsqrt(jnp.mean(o * o, axis=-1, keepdims=True) + RMS_EPS) * g_ref[...]
    o_ref[...] = (on * (1.0 - lam_init)).astype(o_ref.dtype)


def flash_diff(p_ab, eq, ek, slopes2, lam, diff_g, lam_init, bsz, seq):
    tq, tk = 512, 1024
    assert tk % tq == 0
    nh = A_HEADS
    return pl.pallas_call(
        functools.partial(_flash_diff_kernel, tk=tk, lam_init=lam_init),
        grid=(bsz, nh, seq // tq),
        in_specs=[pl.BlockSpec(memory_space=pltpu.SMEM),
                  pl.BlockSpec((None, tq, LANES), lambda b, h, i: (b, i, h)),
                  pl.BlockSpec((None, tq, LANES), lambda b, h, i: (h, i, 0)),
                  pl.BlockSpec((None, seq, LANES), lambda b, h, i: (b, 0, nh + h)),
                  pl.BlockSpec((None, seq, LANES), lambda b, h, i: (h, 0, 0)),
                  pl.BlockSpec((None, seq, LANES), lambda b, h, i: (b, 0, 2 * nh + h)),
                  pl.BlockSpec((4, A_QK_DIM), lambda b, h, i: (0, 0)),
                  pl.BlockSpec((1, A_V_DIM), lambda b, h, i: (0, 0))],
        out_specs=pl.BlockSpec((None, tq, A_V_DIM), lambda b, h, i: (b, i, h)),
        out_shape=jax.ShapeDtypeStruct((bsz, seq, nh * A_V_DIM), BF16),
        scratch_shapes=[pltpu.VMEM((2, 2, tq, 2 * LANES), BF16),
                        pltpu.VMEM((2, tq, LANES), F32),
                        pltpu.VMEM((2, tq, 2 * LANES), F32)],
        compiler_params=_cparams(("parallel", "parallel", "parallel")),
        name="flash_diff",
    )(slopes2, p_ab, eq, p_ab, ek, p_ab, lam, diff_g)


def _flash_mla_kernel(q_ref, k_ref, v_ref, o_ref, m_sc, acc_sc, *, tk):
    seq = k_ref.shape[0]
    q = q_ref[...]
    m_sc[...] = jnp.full(m_sc.shape, NEG_INF, F32)
    acc_sc[...] = jnp.zeros(acc_sc.shape, F32)
    ones_col = _ones_column(tk)

    def body(c, carry):
        k0 = pl.multiple_of(c * tk, tk)
        s = lax.dot_general(q, k_ref[pl.ds(k0, tk), :], _NT, preferred_element_type=F32)
        v_aug = jnp.concatenate([v_ref[pl.ds(k0, tk), :], ones_col], axis=1)
        _online_softmax_step(s, v_aug, m_sc, acc_sc)
        return carry

    lax.fori_loop(0, seq // tk, body, 0)
    o_ref[...] = _softmax_result(acc_sc).astype(o_ref.dtype)


def flash_mla(q_cat, k_cat, v_b, bsz, seq):
    tq, tk = 512, 1024
    kd = 2 * LANES
    return pl.pallas_call(
        functools.partial(_flash_mla_kernel, tk=tk),
        grid=(bsz, B_HEADS, seq // tq),
        in_specs=[pl.BlockSpec((None, tq, kd), lambda b, h, i: (b, i, h)),
                  pl.BlockSpec((None, seq, kd), lambda b, h, i: (b, 0, h)),
                  pl.BlockSpec((None, seq, B_V_DIM), lambda b, h, i: (b, 0, h))],
        out_specs=pl.BlockSpec((None, tq, B_V_DIM), lambda b, h, i: (b, i, h)),
        out_shape=jax.ShapeDtypeStruct((bsz, seq, B_HEADS * B_V_DIM), BF16),
        scratch_shapes=[pltpu.VMEM((tq, LANES), F32), pltpu.VMEM((tq, 2 * LANES), F32)],
        compiler_params=_cparams(("parallel", "parallel", "parallel")),
        name="flash_mla",
    )(q_cat, k_cat, v_b)


def _window_kernel(slopes_ref, sink_ref, q_ref, k_ref, v_ref, o_ref):
    tq = q_ref.shape[0]
    seq = k_ref.shape[0]
    blk = WINDOW
    kw = 3 * blk
    n = pl.program_id(1)
    q0 = pl.program_id(2) * tq
    rel = (lax.broadcasted_iota(jnp.int32, (blk, kw), 1) - lax.broadcasted_iota(jnp.int32, (blk, kw), 0))
    for sb in range(tq // blk):
        t0 = q0 + sb * blk
        start = pl.multiple_of(jnp.clip(t0 - blk, 0, seq - kw), blk)
        kc = k_ref[pl.ds(start, kw), :]
        vc = v_ref[pl.ds(start, kw), :]
        dist = jnp.abs(rel + (start - t0))
        valid = dist <= WINDOW
        distf = dist.astype(F32)
        for g in range(C_GROUP):
            head = n * C_GROUP + g
            qg = q_ref[sb * blk:(sb + 1) * blk, g * C_HEAD_DIM:(g + 1) * C_HEAD_DIM]
            s = lax.dot_general(qg, kc, _NT, preferred_element_type=F32) - slopes_ref[head] * distf
            s = jnp.where(valid, s, NEG_INF)
            sink = sink_ref[head]
            m = jnp.maximum(jnp.max(s, axis=1, keepdims=True), sink)
            e = jnp.exp(s - m)
            p = e / (jnp.sum(e, axis=1, keepdims=True) + jnp.exp(sink - m))
            o = jnp.dot(p.astype(vc.dtype), vc, preferred_element_type=F32)
            o_ref[sb * blk:(sb + 1) * blk, g * C_HEAD_DIM:(g + 1) * C_HEAD_DIM] = o.astype(o_ref.dtype)


def window_attn(p_c, slopes, sink, bsz, seq):
    tq = 512
    gw = C_GROUP * C_HEAD_DIM
    nq = C_HEADS * C_HEAD_DIM // LANES
    return pl.pallas_call(
        _window_kernel,
        grid=(bsz, C_KV_HEADS, seq // tq),
        in_specs=[pl.BlockSpec(memory_space=pltpu.SMEM),
                  pl.BlockSpec(memory_space=pltpu.SMEM),
                  pl.BlockSpec((None, tq, gw), lambda b, n, i: (b, i, n)),
                  pl.BlockSpec((None, seq, C_HEAD_DIM), lambda b, n, i: (b, 0, nq + n)),
                  pl.BlockSpec((None, seq, C_HEAD_DIM), lambda b, n, i: (b, 0, nq + C_KV_HEADS + n))],
        out_specs=pl.BlockSpec((None, tq, gw), lambda b, n, i: (b, i, n)),
        out_shape=jax.ShapeDtypeStruct((bsz, seq, C_HEADS * C_HEAD_DIM), BF16),
        compiler_params=_cparams(("parallel", "parallel", "parallel")),
        name="window_attn",
    )(slopes, sink, p_c, p_c, p_c)


def _swiglu_step(x, wg_ref, wu_ref, wd_ref, acc_ref):
    g = jnp.dot(x, wg_ref[...], preferred_element_type=F32)
    u = jnp.dot(x, wu_ref[...], preferred_element_type=F32)
    a = (g * jax.nn.sigmoid(g) * u).astype(BF16)
    acc_ref[...] += jnp.dot(a, wd_ref[...], preferred_element_type=F32)


def _ffn_kernel(x_ref, wg_ref, wu_ref, wd_ref, o_ref, acc_ref):
    j = pl.program_id(1)

    @pl.when(j == 0)
    def _():
        acc_ref[...] = jnp.zeros(acc_ref.shape, F32)

    _swiglu_step(x_ref[...], wg_ref, wu_ref, wd_ref, acc_ref)

    @pl.when(j == pl.num_programs(1) - 1)
    def _():
        o_ref[...] = acc_ref[...].astype(o_ref.dtype)


def ffn(h, wg, wu, wd):
    m, d = h.shape
    f = wg.shape[1]
    tm, tf = 512, 512
    return pl.pallas_call(
        _ffn_kernel,
        grid=(m // tm, f // tf),
        in_specs=[pl.BlockSpec((tm, d), lambda i, j: (i, 0)),
                  pl.BlockSpec((d, tf), lambda i, j: (0, j)),
                  pl.BlockSpec((d, tf), lambda i, j: (0, j)),
                  pl.BlockSpec((tf, d), lambda i, j: (j, 0))],
        out_specs=pl.BlockSpec((tm, d), lambda i, j: (i, 0)),
        out_shape=jax.ShapeDtypeStruct((m, d), F32),
        scratch_shapes=[pltpu.VMEM((tm, d), F32)],
        compiler_params=_cparams(("parallel", "arbitrary")),
        name="ffn",
    )(h, wg, wu, wd)


def _moe_ffn_kernel(be_ref, x_ref, gate_ref, wg_ref, wu_ref, wd_ref, o_ref, xb_ref, acc_ref):
    del be_ref
    j = pl.program_id(1)

    @pl.when(j == 0)
    def _():
        acc_ref[...] = jnp.zeros(acc_ref.shape, F32)
        xb_ref[...] = x_ref[...].astype(BF16)

    _swiglu_step(xb_ref[...], wg_ref, wu_ref, wd_ref, acc_ref)

    @pl.when(j == pl.num_programs(1) - 1)
    def _():
        o_ref[...] = acc_ref[...] * gate_ref[:, 0:1]


def moe_ffn(block_e, xs, row_gate, wg, wu, wd):
    r, d = xs.shape
    f = wg.shape[2]
    tm, tf = MOE_BLOCK, 256
    grid_spec = pltpu.PrefetchScalarGridSpec(
        num_scalar_prefetch=1,
        grid=(r // tm, f // tf),
        in_specs=[pl.BlockSpec((tm, d), lambda i, j, be: (i, 0)),
                  pl.BlockSpec((tm, LANES), lambda i, j, be: (i, 0)),
                  pl.BlockSpec((None, d, tf), lambda i, j, be: (be[i], 0, j)),
                  pl.BlockSpec((None, d, tf), lambda i, j, be: (be[i], 0, j)),
                  pl.BlockSpec((None, tf, d), lambda i, j, be: (be[i], j, 0))],
        out_specs=pl.BlockSpec((tm, d), lambda i, j, be: (i, 0)),
        scratch_shapes=[pltpu.VMEM((tm, d), BF16), pltpu.VMEM((tm, d), F32)],
    )
    return pl.pallas_call(
        _moe_ffn_kernel,
        grid_spec=grid_spec,
        out_shape=jax.ShapeDtypeStruct((r, d), F32),
        compiler_params=_cparams(("parallel", "arbitrary")),
        name="moe_ffn",
    )(block_e, xs, row_gate, wg, wu, wd)


def _router_kernel(h_ref, w_ref, b_ref, o_ref):
    logits = jnp.dot(h_ref[...], w_ref[...], preferred_element_type=F32,
                     precision=lax.Precision.HIGHEST) + b_ref[...]
    lane = lax.broadcasted_iota(jnp.int32, logits.shape, 1)
    logits = jnp.where(lane < N_EXPERTS, logits, NEG_INF)
    m1 = jnp.max(logits, axis=1, keepdims=True)
    i1 = jnp.min(jnp.where(logits == m1, lane, LANES), axis=1, keepdims=True)
    rest = jnp.where(lane == i1, NEG_INF, logits)
    m2 = jnp.max(rest, axis=1, keepdims=True)
    i2 = jnp.min(jnp.where(rest == m2, lane, LANES), axis=1, keepdims=True)
    e2 = jnp.exp(m2 - m1)
    g1 = 1.0 / (1.0 + e2)
    g2 = e2 / (1.0 + e2)
    out = jnp.where(lane == 0, i1.astype(F32),
                    jnp.where(lane == 1, i2.astype(F32),
                              jnp.where(lane == 2, g1, jnp.where(lane == 3, g2, 0.0))))
    o_ref[...] = out


def router(h, w_pad, b_pad):
    m, d = h.shape
    tm = 512
    return pl.pallas_call(
        _router_kernel,
        grid=(m // tm,),
        in_specs=[pl.BlockSpec((tm, d), lambda i: (i, 0)),
                  pl.BlockSpec((d, LANES), lambda i: (0, 0)),
                  pl.BlockSpec((1, LANES), lambda i: (0, 0))],
        out_specs=pl.BlockSpec((tm, LANES), lambda i: (i, 0)),
        out_shape=jax.ShapeDtypeStruct((m, LANES), F32),
        compiler_params=_cparams(("parallel",)),
        name="router",
    )(h, w_pad, b_pad)


def _gather_rows_kernel(idx_ref, src_ref, dst_ref, sem, *, rows):
    base = pl.program_id(0) * rows

    def issue(r, carry):
        pltpu.make_async_copy(src_ref.at[pl.ds(idx_ref[0, r], 1)],
                              dst_ref.at[pl.ds(base + r, 1)], sem).start()
        return carry

    lax.fori_loop(0, rows, issue, 0)
    pltpu.make_async_copy(src_ref.at[pl.ds(0, rows)], dst_ref.at[pl.ds(base, rows)], sem).wait()


def gather_rows(src, idx):
    rows = 512
    r = idx.shape[0]
    idx3 = idx.reshape(r // rows, 1, rows)
    return pl.pallas_call(
        functools.partial(_gather_rows_kernel, rows=rows),
        grid=(r // rows,),
        in_specs=[pl.BlockSpec((None, 1, rows), lambda i: (i, 0, 0), memory_space=pltpu.SMEM),
                  pl.BlockSpec(memory_space=pl.ANY)],
        out_specs=pl.BlockSpec(memory_space=pl.ANY),
        out_shape=jax.ShapeDtypeStruct((r,) + src.shape[1:], src.dtype),
        scratch_shapes=[pltpu.SemaphoreType.DMA(())],
        compiler_params=_cparams(("arbitrary",)),
        name="gather_rows",
    )(idx3, src)


def _alibi_slopes(n_heads):
    return jnp.asarray(2.0 ** (-8.0 * np.arange(1, n_heads + 1) / n_heads), dtype=F32)


def _bf16_parts(v):
    a = v.astype(BF16).astype(F32)
    b = (v - a).astype(BF16).astype(F32)
    c = (v - a - b).astype(BF16).astype(F32)
    return a, b, c


def _alibi_tables(seq, slopes2):
    nh = slopes2.shape[0]
    pos = jnp.arange(seq, dtype=jnp.int32)
    pos_parts = (((pos // LANES) * LANES).astype(F32), (pos % LANES).astype(F32))
    cols_q, cols_k = [], []
    for sp in _bf16_parts(slopes2):
        sp_col = jnp.broadcast_to(sp[:, None], (nh, seq))
        for pp in pos_parts:
            pp_row = jnp.broadcast_to(pp[None, :], (nh, seq))
            cols_q += [pp_row, sp_col]
            cols_k += [sp_col, -pp_row]
    pad = jnp.zeros((nh, seq, LANES - len(cols_q)), F32)
    eq = jnp.concatenate([jnp.stack(cols_q, axis=-1), pad], axis=-1).astype(BF16)
    ek = jnp.concatenate([jnp.stack(cols_k, axis=-1), pad], axis=-1).astype(BF16)
    return eq, ek


def _rope_tables(seq):
    inv_freq = ROPE_THETA ** (-jnp.arange(0, B_ROPE, 2, dtype=F32) / B_ROPE)
    ang = jnp.arange(seq, dtype=F32)[:, None] * inv_freq[None, :]
    cos, sin = jnp.cos(ang), jnp.sin(ang)
    cos2 = jnp.concatenate([cos, cos], axis=-1)
    sin2 = jnp.concatenate([sin, sin], axis=-1)
    z64 = jnp.zeros((seq, B_ROPE), F32)
    ka = jnp.concatenate([cos2, z64], axis=-1)
    kb = jnp.concatenate([sin2, z64], axis=-1)
    ones = jnp.ones((seq, B_NOPE), F32)
    z128 = jnp.zeros((seq, B_NOPE), F32)
    qa = jnp.tile(jnp.concatenate([ones, cos2, z64], axis=-1), (1, B_HEADS))
    qb = jnp.tile(jnp.concatenate([z128, sin2, z64], axis=-1), (1, B_HEADS))
    return ka, kb, qa, qb


def _rot_cols(w):
    half = w.shape[-1] // 2
    return jnp.concatenate([-w[..., half:], w[..., :half]], axis=-1)


def _routing_tables(flat_e, flat_g, n_tok):
    n_assign = flat_e.shape[0]
    onehot = (flat_e[:, None] == jnp.arange(N_EXPERTS, dtype=jnp.int32)[None, :]).astype(jnp.int32)
    csum = jnp.cumsum(onehot, axis=0)
    rank = jnp.sum((csum - onehot) * onehot, axis=1)
    sizes = csum[-1]
    padded = ((sizes + MOE_BLOCK - 1) // MOE_BLOCK) * MOE_BLOCK
    pad_end = jnp.cumsum(padded)
    pad_start = pad_end - padded
    dest = (pad_start[flat_e] + rank).astype(jnp.int32)
    n_blocks = -(-n_assign // MOE_BLOCK) + N_EXPERTS
    n_rows = n_blocks * MOE_BLOCK
    tok = jnp.arange(n_assign, dtype=jnp.int32) // TOP_K
    row_tok = jnp.zeros((n_rows,), jnp.int32).at[dest].set(tok)
    row_gate = jnp.zeros((n_rows,), F32).at[dest].set(flat_g)
    block_start = jnp.arange(n_blocks, dtype=jnp.int32) * MOE_BLOCK
    block_e = jnp.minimum(jnp.searchsorted(pad_end, block_start, side='right'), N_EXPERTS - 1).astype(jnp.int32)
    return dest, row_tok, row_gate, block_e


def kernel(x, c, ada_w, ada_b, ln_g, ln_b, ab_w_in, ab_lam, ab_diff_g, ab_q_norm_g, ab_kv_norm_g, ab_w_uq,
           ab_w_ukv, ab_w_out, c_w_in, c_sink, c_w_out, ffn_w_gate, ffn_w_up, ffn_w_down, moe_w_router,
           moe_b_router, moe_w_gate, moe_w_up, moe_w_down):
    bsz, seq, d = x.shape
    depth = ada_w.shape[0]
    n_tok = bsz * seq
    alpha = (2 * depth) ** 0.25

    c_pad = jnp.zeros((8, d), F32).at[:bsz].set(c)
    mods = adaln_all(c_pad, ada_w.reshape(2 * depth, d, 3 * d), ada_b.reshape(2 * depth, 1, 3 * d))
    mods = mods.reshape(2 * depth, 8, 3, d)

    ka, kb, qa, qb = _rope_tables(seq)
    slopes_a2 = _alibi_slopes(A_HEADS) * LOG2E
    eq_a, ek_a = _alibi_tables(seq, slopes_a2)
    slopes_c = _alibi_slopes(C_HEADS)
    scale_a = A_QK_DIM ** -0.5 * LOG2E
    scale_b = (B_NOPE + B_ROPE) ** -0.5 * LOG2E
    scale_c = C_HEAD_DIM ** -0.5

    h = modulate(x, mods, 0, BF16)
    for layer in range(depth):
        i = layer // 2
        l2 = 2 * layer
        last = layer == depth - 1
        if layer % 2 == 0:
            lam_init = 0.8 - 0.6 * math.exp(-0.3 * layer)
            w_in = ab_w_in[i]
            w_kr = w_in[:, 3 * A_COLS + B_Q_LORA + B_KV_LORA:]
            w_in2 = jnp.concatenate([w_in[:, :A_COLS] * scale_a, w_in[:, A_COLS:], _rot_cols(w_kr)],
                                    axis=1).astype(BF16)
            p_ab = inproj_ab(h.reshape(n_tok, d), w_in2, ka, kb, seq)
            o_a = flash_diff(p_ab.reshape(bsz, seq, -1), eq_a, ek_a, slopes_a2, ab_lam[i],
                             ab_diff_g[i].reshape(1, -1), lam_init, bsz, seq)
            w_uq = ab_w_uq[i].reshape(B_Q_LORA, B_HEADS, B_NOPE + B_ROPE) * scale_b
            w_uq2 = jnp.concatenate([w_uq, _rot_cols(w_uq[..., B_NOPE:])], axis=-1)
            w_uq2 = w_uq2.reshape(B_Q_LORA, B_HEADS * 2 * LANES).astype(BF16)
            q_cat = uq_proj(p_ab, 3 * A_COLS // B_Q_LORA, ab_q_norm_g[i].reshape(1, -1), w_uq2, qa, qb, seq)
            k_cat, v_b = ukv_proj(p_ab, (3 * A_COLS + B_Q_LORA) // B_KV_LORA,
                                  (3 * A_COLS + B_Q_LORA + B_KV_LORA) // LANES,
                                  ab_kv_norm_g[i].reshape(1, -1), ab_w_ukv[i].astype(BF16))
            o_b = flash_mla(q_cat.reshape(bsz, seq, -1), k_cat.reshape(bsz, seq, -1),
                            v_b.reshape(bsz, seq, -1), bsz, seq)
            o_cat = jnp.concatenate([o_a, o_b], axis=-1).reshape(n_tok, -1)
            y = matmul(o_cat, ab_w_out[i].astype(BF16), F32)
        else:
            w_in = c_w_in[i]
            nqc = C_HEADS * C_HEAD_DIM
            w_in2 = jnp.concatenate([w_in[:, :nqc] * scale_c, w_in[:, nqc:]], axis=1).astype(BF16)
            p_c = matmul(h.reshape(n_tok, d), w_in2, BF16, tn=1024)
            o_c = window_attn(p_c.reshape(bsz, seq, -1), slopes_c, c_sink[i].astype(F32), bsz, seq)
            y = matmul(o_c.reshape(n_tok, -1), c_w_out[i].astype(BF16), F32)
        moe_next = layer % 2 == 1
        x, h = postnorm(x, y.reshape(bsz, seq, d), mods, l2, ln_g[layer, 0], ln_b[layer, 0], alpha,
                        next_dtype=F32 if moe_next else BF16)

        if not moe_next:
            y = ffn(h.reshape(n_tok, d), ffn_w_gate[i].astype(BF16), ffn_w_up[i].astype(BF16),
                    ffn_w_down[i].astype(BF16))
            pair = False
        else:
            hf = h.reshape(n_tok, d)
            w_r = jnp.zeros((d, LANES), F32).at[:, :N_EXPERTS].set(moe_w_router[i])
            b_r = jnp.zeros((1, LANES), F32).at[0, :N_EXPERTS].set(moe_b_router[i])
            route = router(hf, w_r, b_r)
            flat_e = route[:, :TOP_K].astype(jnp.int32).reshape(-1)
            flat_g = route[:, TOP_K:2 * TOP_K].reshape(-1)
            dest, row_tok, row_gate, block_e = _routing_tables(flat_e, flat_g, n_tok)
            xs = gather_rows(hf.reshape(n_tok, d // LANES, LANES), row_tok).reshape(-1, d)
            gate_rep = jnp.broadcast_to(row_gate[:, None], (row_gate.shape[0], LANES))
            ys = moe_ffn(block_e, xs, gate_rep, moe_w_gate[i].astype(BF16), moe_w_up[i].astype(BF16),
                         moe_w_down[i].astype(BF16))
            y = gather_rows(ys.reshape(-1, d // LANES, LANES), dest).reshape(n_tok, TOP_K * d)
            pair = True
        x, h = postnorm(x, y.reshape(bsz, seq, -1), mods, l2 + 1, ln_g[layer, 1], ln_b[layer, 1], alpha,
                        pair_sum=pair, next_dtype=None if last else BF16)
    return x
```

```python
import functools
import math

import numpy as np
import jax
import jax.numpy as jnp
from jax import lax
from jax.experimental import pallas as pl
from jax.experimental.pallas import tpu as pltpu

F32 = jnp.float32
BF16 = jnp.bfloat16

A_HEADS = 4
A_QK_DIM = 64
A_V_DIM = 128
B_HEADS = 4
B_Q_LORA = 512
B_KV_LORA = 256
B_NOPE = 128
B_ROPE = 64
B_V_DIM = 128
ROPE_THETA = 10000.0
C_HEADS = 16
C_KV_HEADS = 4
C_GROUP = C_HEADS // C_KV_HEADS
C_HEAD_DIM = 128
WINDOW = 128
N_EXPERTS = 8
TOP_K = 2
MOE_BLOCK = 512
LN_EPS = 1e-5
RMS_EPS = 1e-6

A_COLS = A_HEADS * 2 * A_QK_DIM
LANES = 128
VMEM_LIMIT = 56 * 1024 * 1024
NEG_INF = float("-inf")
LOG2E = math.log2(math.e)


def _cparams(sem, vmem=VMEM_LIMIT):
    return pltpu.CompilerParams(dimension_semantics=sem, vmem_limit_bytes=vmem)


def _adaln_kernel(c_ref, w_ref, b_ref, o_ref):
    c = c_ref[...]
    ca = c * jax.nn.sigmoid(c)
    o_ref[...] = jnp.dot(ca, w_ref[...], preferred_element_type=F32,
                         precision=lax.Precision.HIGHEST) + b_ref[...]


def adaln_all(c_pad, ada_w, ada_b):
    l2, d, n = ada_w.shape
    tn = 768
    return pl.pallas_call(
        _adaln_kernel,
        grid=(l2, n // tn),
        in_specs=[pl.BlockSpec((8, d), lambda l, j: (0, 0)),
                  pl.BlockSpec((None, d, tn), lambda l, j: (l, 0, j)),
                  pl.BlockSpec((None, 1, tn), lambda l, j: (l, 0, j))],
        out_specs=pl.BlockSpec((None, 8, tn), lambda l, j: (l, 0, j)),
        out_shape=jax.ShapeDtypeStruct((l2, 8, n), F32),
        compiler_params=_cparams(("parallel", "parallel")),
        name="adaln",
    )(c_pad, ada_w, ada_b)


def _modulate_kernel(x_ref, mod_ref, h_ref):
    x = x_ref[...]
    h_ref[...] = (x * (1.0 + mod_ref[1:2, :]) + mod_ref[0:1, :]).astype(h_ref.dtype)


def modulate(x, mods, l2, out_dtype):
    b, s, d = x.shape
    ts = 512
    return pl.pallas_call(
        _modulate_kernel,
        grid=(b, s // ts),
        in_specs=[pl.BlockSpec((None, ts, d), lambda i, j: (i, j, 0)),
                  pl.BlockSpec((None, None, 3, d), lambda i, j: (l2, i, 0, 0))],
        out_specs=pl.BlockSpec((None, ts, d), lambda i, j: (i, j, 0)),
        out_shape=jax.ShapeDtypeStruct((b, s, d), out_dtype),
        compiler_params=_cparams(("parallel", "parallel")),
        name="modulate",
    )(x, mods)


def _postnorm_kernel(*refs, alpha, pair_sum, has_next):
    if has_next:
        x_ref, y_ref, modc_ref, g_ref, b_ref, modn_ref, xo_ref, h_ref = refs
    else:
        x_ref, y_ref, modc_ref, g_ref, b_ref, xo_ref = refs
    x = x_ref[...]
    d = x.shape[-1]
    if pair_sum:
        y = y_ref[:, :d].astype(F32) + y_ref[:, d:].astype(F32)
    else:
        y = y_ref[...].astype(F32)
    z = alpha * x + (1.0 + modc_ref[2:3, :]) * y
    mu = jnp.mean(z, axis=-1, keepdims=True)
    zc = z - mu
    var = jnp.mean(zc * zc, axis=-1, keepdims=True)
    xn = zc * lax.rsqrt(var + LN_EPS) * g_ref[...] + b_ref[...]
    xo_ref[...] = xn
    if has_next:
        h_ref[...] = (xn * (1.0 + modn_ref[1:2, :]) + modn_ref[0:1, :]).astype(h_ref.dtype)


def postnorm(x, y, mods, l2, ln_g, ln_b, alpha, *, pair_sum=False, next_dtype=None):
    b, s, d = x.shape
    ts = 256
    has_next = next_dtype is not None
    yw = y.shape[-1]
    in_specs = [pl.BlockSpec((None, ts, d), lambda i, j: (i, j, 0)),
                pl.BlockSpec((None, ts, yw), lambda i, j: (i, j, 0)),
                pl.BlockSpec((None, None, 3, d), lambda i, j: (l2, i, 0, 0)),
                pl.BlockSpec((1, d), lambda i, j: (0, 0)),
                pl.BlockSpec((1, d), lambda i, j: (0, 0))]
    args = [x, y, mods, ln_g.reshape(1, d), ln_b.reshape(1, d)]
    out_specs = [pl.BlockSpec((None, ts, d), lambda i, j: (i, j, 0))]
    out_shape = [jax.ShapeDtypeStruct((b, s, d), F32)]
    if has_next:
        in_specs.append(pl.BlockSpec((None, None, 3, d), lambda i, j: (l2 + 1, i, 0, 0)))
        args.append(mods)
        out_specs.append(pl.BlockSpec((None, ts, d), lambda i, j: (i, j, 0)))
        out_shape.append(jax.ShapeDtypeStruct((b, s, d), next_dtype))
    out = pl.pallas_call(
        functools.partial(_postnorm_kernel, alpha=alpha, pair_sum=pair_sum, has_next=has_next),
        grid=(b, s // ts),
        in_specs=in_specs,
        out_specs=out_specs,
        out_shape=out_shape,
        compiler_params=_cparams(("parallel", "parallel")),
        name="postnorm",
    )(*args)
    return (out[0], out[1]) if has_next else (out[0], None)


def _rms_rows(a_ref, g_ref):
    a = a_ref[...].astype(F32)
    return (a * lax.rsqrt(jnp.mean(a * a, axis=-1, keepdims=True) + RMS_EPS) * g_ref[...]).astype(BF16)


def _matmul_kernel(a_ref, w_ref, o_ref):
    o_ref[...] = jnp.dot(a_ref[...], w_ref[...], preferred_element_type=F32).astype(o_ref.dtype)


def matmul(a, w, out_dtype, tm=512, tn=None):
    m, k = a.shape
    n = w.shape[1]
    tn = n if tn is None else tn
    return pl.pallas_call(
        _matmul_kernel,
        grid=(m // tm, n // tn),
        in_specs=[pl.BlockSpec((tm, k), lambda i, j: (i, 0)),
                  pl.BlockSpec((k, tn), lambda i, j: (0, j))],
        out_specs=pl.BlockSpec((tm, tn), lambda i, j: (i, j)),
        out_shape=jax.ShapeDtypeStruct((m, n), out_dtype),
        compiler_params=_cparams(("parallel", "parallel")),
        name="matmul",
    )(a, w)


def _inproj_ab_kernel(a_ref, w_ref, ra_ref, rb_ref, o_ref):
    acc = jnp.dot(a_ref[...], w_ref[...], preferred_element_type=F32)
    n = acc.shape[1]
    o_ref[:, :n - LANES] = acc[:, :n - LANES].astype(o_ref.dtype)
    last = acc[:, n - LANES:]
    roped = last * ra_ref[...] + pltpu.roll(last, LANES // 2, 1) * rb_ref[...]
    o_ref[:, n - LANES:] = roped.astype(o_ref.dtype)


def inproj_ab(h, w, rope_a, rope_b, seq):
    m, k = h.shape
    n = w.shape[1]
    tm = 512
    nsb = seq // tm
    return pl.pallas_call(
        _inproj_ab_kernel,
        grid=(m // tm,),
        in_specs=[pl.BlockSpec((tm, k), lambda i: (i, 0)),
                  pl.BlockSpec((k, n), lambda i: (0, 0)),
                  pl.BlockSpec((tm, LANES), lambda i: (i % nsb, 0)),
                  pl.BlockSpec((tm, LANES), lambda i: (i % nsb, 0))],
        out_specs=pl.BlockSpec((tm, n), lambda i: (i, 0)),
        out_shape=jax.ShapeDtypeStruct((m, n), BF16),
        compiler_params=_cparams(("parallel",)),
        name="inproj_ab",
    )(h, w, rope_a, rope_b)


def _uq_kernel(a_ref, g_ref, w_ref, ra_ref, rb_ref, o_ref):
    an = _rms_rows(a_ref, g_ref)
    acc = jnp.dot(an, w_ref[...], preferred_element_type=F32)
    n = acc.shape[1]
    o_ref[...] = (acc * ra_ref[...] + pltpu.roll(acc, n - B_ROPE, 1) * rb_ref[...]).astype(o_ref.dtype)


def uq_proj(p_ab, col_block, g, w, rope_a, rope_b, seq):
    m = p_ab.shape[0]
    k, n = w.shape
    tm = 512
    nsb = seq // tm
    return pl.pallas_call(
        _uq_kernel,
        grid=(m // tm,),
        in_specs=[pl.BlockSpec((tm, k), lambda i: (i, col_block)),
                  pl.BlockSpec((1, k), lambda i: (0, 0)),
                  pl.BlockSpec((k, n), lambda i: (0, 0)),
                  pl.BlockSpec((tm, n), lambda i: (i % nsb, 0)),
                  pl.BlockSpec((tm, n), lambda i: (i % nsb, 0))],
        out_specs=pl.BlockSpec((tm, n), lambda i: (i, 0)),
        out_shape=jax.ShapeDtypeStruct((m, n), BF16),
        compiler_params=_cparams(("parallel",)),
        name="uq_proj",
    )(p_ab, g, w, rope_a, rope_b)


def _ukv_kernel(a_ref, g_ref, w_ref, kr_ref, k_ref, v_ref):
    an = _rms_rows(a_ref, g_ref)
    acc = jnp.dot(an, w_ref[...], preferred_element_type=F32)
    kr = kr_ref[...]
    for h in range(B_HEADS):
        base = h * (B_NOPE + B_V_DIM)
        k_ref[:, 2 * h * LANES:(2 * h + 1) * LANES] = acc[:, base:base + B_NOPE].astype(k_ref.dtype)
        k_ref[:, (2 * h + 1) * LANES:(2 * h + 2) * LANES] = kr
        v_ref[:, h * B_V_DIM:(h + 1) * B_V_DIM] = acc[:, base + B_NOPE:base + B_NOPE + B_V_DIM].astype(v_ref.dtype)


def ukv_proj(p_ab, ckv_block, kr_block, g, w):
    m = p_ab.shape[0]
    k, n = w.shape
    tm = 512
    return pl.pallas_call(
        _ukv_kernel,
        grid=(m // tm,),
        in_specs=[pl.BlockSpec((tm, k), lambda i: (i, ckv_block)),
                  pl.BlockSpec((1, k), lambda i: (0, 0)),
                  pl.BlockSpec((k, n), lambda i: (0, 0)),
                  pl.BlockSpec((tm, LANES), lambda i: (i, kr_block))],
        out_specs=[pl.BlockSpec((tm, B_HEADS * 2 * LANES), lambda i: (i, 0)),
                   pl.BlockSpec((tm, B_HEADS * B_V_DIM), lambda i: (i, 0))],
        out_shape=[jax.ShapeDtypeStruct((m, B_HEADS * 2 * LANES), BF16),
                   jax.ShapeDtypeStruct((m, B_HEADS * B_V_DIM), BF16)],
        compiler_params=_cparams(("parallel",)),
        name="ukv_proj",
    )(p_ab, g, w, p_ab)


_NT = (((1,), (1,)), ((), ()))


def _ones_block(rows):
    return jnp.ones((rows, LANES), BF16)


def _online_softmax_step(s, v_aug, m_ref, acc_ref):
    tk = s.shape[1]
    m_prev = m_ref[...]
    m_new = jnp.maximum(m_prev, jnp.max(s, axis=1, keepdims=True))
    alpha = jnp.exp2(m_prev - m_new)
    p = jnp.exp2(s - jnp.concatenate([m_new] * (tk // LANES), axis=1))
    pv = jnp.dot(p.astype(v_aug.dtype), v_aug, preferred_element_type=F32)
    acc_ref[...] = jnp.concatenate([alpha, alpha], axis=1) * acc_ref[...] + pv
    m_ref[...] = m_new


def _softmax_result(acc_ref):
    acc = acc_ref[...]
    return acc[:, :LANES] / acc[:, LANES:]


def _flash_diff_kernel(slopes_ref, q_ref, eq_ref, k_ref, ek_ref, v_ref, lam_ref, g_ref, o_ref,
                       qaug_sc, m_sc, acc_sc, *, tk, lam_init):
    tq = q_ref.shape[0]
    seq = k_ref.shape[0]
    qi = pl.program_id(2)
    q0 = qi * tq
    q = q_ref[...]
    eq = eq_ref[...]
    lane = lax.broadcasted_iota(jnp.int32, q.shape, 1)
    zero = jnp.zeros_like(q)
    for mp in range(2):
        qm = jnp.where((lane < A_QK_DIM) == (mp == 0), q, zero)
        qaug_sc[0, mp] = jnp.concatenate([qm, -eq], axis=1)
        qaug_sc[1, mp] = jnp.concatenate([qm, eq], axis=1)
    m_sc[...] = jnp.full(m_sc.shape, NEG_INF, F32)
    acc_sc[...] = jnp.zeros(acc_sc.shape, F32)
    ones_col = _ones_block(tk)

    def chunk(c, side, diagonal):
        k0 = pl.multiple_of(c * tk, tk)
        kc = jnp.concatenate([k_ref[pl.ds(k0, tk), :], ek_ref[pl.ds(k0, tk), :]], axis=1)
        v_aug = jnp.concatenate([v_ref[pl.ds(k0, tk), :], ones_col], axis=1)
        if diagonal:
            rel = (lax.broadcasted_iota(jnp.int32, (tq, tk), 0)
                   - lax.broadcasted_iota(jnp.int32, (tq, tk), 1)) + (q0 - k0)
            corr = (2.0 * slopes_ref[pl.program_id(1)]) * jnp.maximum(rel, 0).astype(F32)
        for mp in range(2):
            s = lax.dot_general(qaug_sc[side, mp], kc, _NT, preferred_element_type=F32)
            if diagonal:
                s = s - corr
            _online_softmax_step(s, v_aug, m_sc.at[mp], acc_sc.at[mp])

    def left(c, carry):
        chunk(c, 0, False)
        return carry

    def right(c, carry):
        chunk(c, 1, False)
        return carry

    cd = q0 // tk
    lax.fori_loop(0, cd, left, 0)
    chunk(cd, 1, True)
    lax.fori_loop(cd + 1, seq // tk, right, 0)

    o0 = _softmax_result(acc_sc.at[0])
    o1 = _softmax_result(acc_sc.at[1])
    lam = lam_ref[...]
    lam_full = (jnp.exp(jnp.sum(lam[0:1] * lam[1:2], axis=1, keepdims=True))
                - jnp.exp(jnp.sum(lam[2:3] * lam[3:4], axis=1, keepdims=True)) + lam_init)
    o = o0 - lam_full * o1
    on = o * lax.rsqrt(jnp.mean(o * o, axis=-1, keepdims=True) + RMS_EPS) * g_ref[...]
    o_ref[...] = (on * (1.0 - lam_init)).astype(o_ref.dtype)


def flash_diff(p_ab, eq, ek, slopes2, lam, diff_g, lam_init, bsz, seq):
    tq, tk = 512, 1024
    assert tk % tq == 0
    nh = A_HEADS
    return pl.pallas_call(
        functools.partial(_flash_diff_kernel, tk=tk, lam_init=lam_init),
        grid=(bsz, nh, seq // tq),
        in_specs=[pl.BlockSpec(memory_space=pltpu.SMEM),
                  pl.BlockSpec((None, tq, LANES), lambda b, h, i: (b, i, h)),
                  pl.BlockSpec((None, tq, LANES), lambda b, h, i: (h, i, 0)),
                  pl.BlockSpec((None, seq, LANES), lambda b, h, i: (b, 0, nh + h)),
                  pl.BlockSpec((None, seq, LANES), lambda b, h, i: (h, 0, 0)),
                  pl.BlockSpec((None, seq, LANES), lambda b, h, i: (b, 0, 2 * nh + h)),
                  pl.BlockSpec((4, A_QK_DIM), lambda b, h, i: (0, 0)),
                  pl.BlockSpec((1, A_V_DIM), lambda b, h, i: (0, 0))],
        out_specs=pl.BlockSpec((None, tq, A_V_DIM), lambda b, h, i: (b, i, h)),
        out_shape=jax.ShapeDtypeStruct((bsz, seq, nh * A_V_DIM), BF16),
        scratch_shapes=[pltpu.VMEM((2, 2, tq, 2 * LANES), BF16),
                        pltpu.VMEM((2, tq, LANES), F32),
                        pltpu.VMEM((2, tq, 2 * LANES), F32)],
        compiler_params=_cparams(("parallel", "parallel", "parallel")),
        name="flash_diff",
    )(slopes2, p_ab, eq, p_ab, ek, p_ab, lam, diff_g)


def _flash_mla_kernel(q_ref, k_ref, v_ref, o_ref, m_sc, acc_sc, *, tk):
    seq = k_ref.shape[0]
    q = q_ref[...]
    m_sc[...] = jnp.full(m_sc.shape, NEG_INF, F32)
    acc_sc[...] = jnp.zeros(acc_sc.shape, F32)
    ones_col = _ones_block(tk)

    def body(c, carry):
        k0 = pl.multiple_of(c * tk, tk)
        s = lax.dot_general(q, k_ref[pl.ds(k0, tk), :], _NT, preferred_element_type=F32)
        v_aug = jnp.concatenate([v_ref[pl.ds(k0, tk), :], ones_col], axis=1)
        _online_softmax_step(s, v_aug, m_sc, acc_sc)
        return carry

    lax.fori_loop(0, seq // tk, body, 0)
    o_ref[...] = _softmax_result(acc_sc).astype(o_ref.dtype)


def flash_mla(q_cat, k_cat, v_b, bsz, seq):
    tq, tk = 1024, 1024
    kd = 2 * LANES
    return pl.pallas_call(
        functools.partial(_flash_mla_kernel, tk=tk),
        grid=(bsz, B_HEADS, seq // tq),
        in_specs=[pl.BlockSpec((None, tq, kd), lambda b, h, i: (b, i, h)),
                  pl.BlockSpec((None, seq, kd), lambda b, h, i: (b, 0, h)),
                  pl.BlockSpec((None, seq, B_V_DIM), lambda b, h, i: (b, 0, h))],
        out_specs=pl.BlockSpec((None, tq, B_V_DIM), lambda b, h, i: (b, i, h)),
        out_shape=jax.ShapeDtypeStruct((bsz, seq, B_HEADS * B_V_DIM), BF16),
        scratch_shapes=[pltpu.VMEM((tq, LANES), F32), pltpu.VMEM((tq, 2 * LANES), F32)],
        compiler_params=_cparams(("parallel", "parallel", "parallel")),
        name="flash_mla",
    )(q_cat, k_cat, v_b)


def _window_kernel(slopes_ref, sink_ref, q_ref, k_ref, v_ref, o_ref):
    tq = q_ref.shape[0]
    seq = k_ref.shape[0]
    blk = WINDOW
    kw = 3 * blk
    n = pl.program_id(1)
    q0 = pl.program_id(2) * tq
    rel = (lax.broadcasted_iota(jnp.int32, (blk, kw), 1) - lax.broadcasted_iota(jnp.int32, (blk, kw), 0))
    ones_blk = _ones_block(kw)
    heads = [n * C_GROUP + g for g in range(C_GROUP)]
    slope_rows = jnp.concatenate([jnp.full((blk, kw), slopes_ref[hd], F32) for hd in heads], axis=0)
    sink_rows = jnp.concatenate([jnp.full((blk, LANES), sink_ref[hd], F32) for hd in heads], axis=0)
    for sb in range(tq // blk):
        t0 = q0 + sb * blk
        start = pl.multiple_of(jnp.clip(t0 - blk, 0, seq - kw), blk)
        kc = k_ref[pl.ds(start, kw), :]
        v_aug = jnp.concatenate([v_ref[pl.ds(start, kw), :], ones_blk], axis=1)
        dist = jnp.abs(rel + (start - t0))
        dist_rows = jnp.concatenate([dist] * C_GROUP, axis=0)
        qs = jnp.concatenate([q_ref[sb * blk:(sb + 1) * blk, g * C_HEAD_DIM:(g + 1) * C_HEAD_DIM]
                              for g in range(C_GROUP)], axis=0)
        s = lax.dot_general(qs, kc, _NT, preferred_element_type=F32) - slope_rows * dist_rows.astype(F32)
        s = jnp.where(dist_rows <= WINDOW, s, NEG_INF)
        m = jnp.maximum(jnp.max(s, axis=1, keepdims=True), sink_rows)
        e = jnp.exp(s - jnp.concatenate([m] * (kw // LANES), axis=1))
        pv = jnp.dot(e.astype(v_aug.dtype), v_aug, preferred_element_type=F32)
        o = pv[:, :C_HEAD_DIM] / (pv[:, C_HEAD_DIM:] + jnp.exp(sink_rows - m))
        for g in range(C_GROUP):
            o_ref[sb * blk:(sb + 1) * blk, g * C_HEAD_DIM:(g + 1) * C_HEAD_DIM] = (
                o[g * blk:(g + 1) * blk].astype(o_ref.dtype))


def window_attn(p_c, slopes, sink, bsz, seq):
    tq = 512
    gw = C_GROUP * C_HEAD_DIM
    nq = C_HEADS * C_HEAD_DIM // LANES
    return pl.pallas_call(
        _window_kernel,
        grid=(bsz, C_KV_HEADS, seq // tq),
        in_specs=[pl.BlockSpec(memory_space=pltpu.SMEM),
                  pl.BlockSpec(memory_space=pltpu.SMEM),
                  pl.BlockSpec((None, tq, gw), lambda b, n, i: (b, i, n)),
                  pl.BlockSpec((None, seq, C_HEAD_DIM), lambda b, n, i: (b, 0, nq + n)),
                  pl.BlockSpec((None, seq, C_HEAD_DIM), lambda b, n, i: (b, 0, nq + C_KV_HEADS + n))],
        out_specs=pl.BlockSpec((None, tq, gw), lambda b, n, i: (b, i, n)),
        out_shape=jax.ShapeDtypeStruct((bsz, seq, C_HEADS * C_HEAD_DIM), BF16),
        compiler_params=_cparams(("parallel", "parallel", "parallel")),
        name="window_attn",
    )(slopes, sink, p_c, p_c, p_c)


def _swiglu_step(x, wg_ref, wu_ref, wd_ref, acc_ref):
    g = jnp.dot(x, wg_ref[...], preferred_element_type=F32)
    u = jnp.dot(x, wu_ref[...], preferred_element_type=F32)
    a = (g * jax.nn.sigmoid(g) * u).astype(BF16)
    acc_ref[...] += jnp.dot(a, wd_ref[...], preferred_element_type=F32)


def _ffn_kernel(x_ref, wg_ref, wu_ref, wd_ref, o_ref, acc_ref):
    j = pl.program_id(1)

    @pl.when(j == 0)
    def _():
        acc_ref[...] = jnp.zeros(acc_ref.shape, F32)

    _swiglu_step(x_ref[...], wg_ref, wu_ref, wd_ref, acc_ref)

    @pl.when(j == pl.num_programs(1) - 1)
    def _():
        o_ref[...] = acc_ref[...].astype(o_ref.dtype)


def ffn(h, wg, wu, wd):
    m, d = h.shape
    f = wg.shape[1]
    tm, tf = 512, 512
    return pl.pallas_call(
        _ffn_kernel,
        grid=(m // tm, f // tf),
        in_specs=[pl.BlockSpec((tm, d), lambda i, j: (i, 0)),
                  pl.BlockSpec((d, tf), lambda i, j: (0, j)),
                  pl.BlockSpec((d, tf), lambda i, j: (0, j)),
                  pl.BlockSpec((tf, d), lambda i, j: (j, 0))],
        out_specs=pl.BlockSpec((tm, d), lambda i, j: (i, 0)),
        out_shape=jax.ShapeDtypeStruct((m, d), F32),
        scratch_shapes=[pltpu.VMEM((tm, d), F32)],
        compiler_params=_cparams(("parallel", "arbitrary")),
        name="ffn",
    )(h, wg, wu, wd)


def _moe_ffn_kernel(be_ref, x_ref, gate_ref, wg_ref, wu_ref, wd_ref, o_ref, xb_ref, acc_ref):
    del be_ref
    j = pl.program_id(1)

    @pl.when(j == 0)
    def _():
        acc_ref[...] = jnp.zeros(acc_ref.shape, F32)
        xb_ref[...] = x_ref[...].astype(BF16)

    _swiglu_step(xb_ref[...], wg_ref, wu_ref, wd_ref, acc_ref)

    @pl.when(j == pl.num_programs(1) - 1)
    def _():
        o_ref[...] = acc_ref[...] * gate_ref[:, 0:1]


def moe_ffn(block_e, xs, row_gate, wg, wu, wd):
    r, d = xs.shape
    f = wg.shape[2]
    tm, tf = MOE_BLOCK, 256
    grid_spec = pltpu.PrefetchScalarGridSpec(
        num_scalar_prefetch=1,
        grid=(r // tm, f // tf),
        in_specs=[pl.BlockSpec((tm, d), lambda i, j, be: (i, 0)),
                  pl.BlockSpec((tm, LANES), lambda i, j, be: (i, 0)),
                  pl.BlockSpec((None, d, tf), lambda i, j, be: (be[i], 0, j)),
                  pl.BlockSpec((None, d, tf), lambda i, j, be: (be[i], 0, j)),
                  pl.BlockSpec((None, tf, d), lambda i, j, be: (be[i], j, 0))],
        out_specs=pl.BlockSpec((tm, d), lambda i, j, be: (i, 0)),
        scratch_shapes=[pltpu.VMEM((tm, d), BF16), pltpu.VMEM((tm, d), F32)],
    )
    return pl.pallas_call(
        _moe_ffn_kernel,
        grid_spec=grid_spec,
        out_shape=jax.ShapeDtypeStruct((r, d), F32),
        compiler_params=_cparams(("parallel", "arbitrary")),
        name="moe_ffn",
    )(block_e, xs, row_gate, wg, wu, wd)


def _router_kernel(h_ref, w_ref, b_ref, o_ref):
    logits = jnp.dot(h_ref[...], w_ref[...], preferred_element_type=F32,
                     precision=lax.Precision.HIGHEST) + b_ref[...]
    lane = lax.broadcasted_iota(jnp.int32, logits.shape, 1)
    logits = jnp.where(lane < N_EXPERTS, logits, NEG_INF)
    m1 = jnp.max(logits, axis=1, keepdims=True)
    i1 = jnp.min(jnp.where(logits == m1, lane, LANES), axis=1, keepdims=True)
    rest = jnp.where(lane == i1, NEG_INF, logits)
    m2 = jnp.max(rest, axis=1, keepdims=True)
    i2 = jnp.min(jnp.where(rest == m2, lane, LANES), axis=1, keepdims=True)
    e2 = jnp.exp(m2 - m1)
    g1 = 1.0 / (1.0 + e2)
    g2 = e2 / (1.0 + e2)
    out = jnp.where(lane == 0, i1.astype(F32),
                    jnp.where(lane == 1, i2.astype(F32),
                              jnp.where(lane == 2, g1, jnp.where(lane == 3, g2, 0.0))))
    o_ref[...] = out


def router(h, w_pad, b_pad):
    m, d = h.shape
    tm = 512
    return pl.pallas_call(
        _router_kernel,
        grid=(m // tm,),
        in_specs=[pl.BlockSpec((tm, d), lambda i: (i, 0)),
                  pl.BlockSpec((d, LANES), lambda i: (0, 0)),
                  pl.BlockSpec((1, LANES), lambda i: (0, 0))],
        out_specs=pl.BlockSpec((tm, LANES), lambda i: (i, 0)),
        out_shape=jax.ShapeDtypeStruct((m, LANES), F32),
        compiler_params=_cparams(("parallel",)),
        name="router",
    )(h, w_pad, b_pad)


def _gather_rows_kernel(idx_ref, src_ref, dst_ref, sem):
    rows = dst_ref.shape[0]

    def issue(r, carry):
        pltpu.make_async_copy(src_ref.at[pl.ds(idx_ref[0, r], 1)], dst_ref.at[pl.ds(r, 1)], sem).start()
        return carry

    lax.fori_loop(0, rows, issue, 0)
    pltpu.make_async_copy(src_ref.at[pl.ds(0, rows)], dst_ref, sem).wait()


def gather_rows(src, idx):
    rows = 512
    r = idx.shape[0]
    idx3 = idx.reshape(r // rows, 1, rows)
    blk = (rows,) + src.shape[1:]
    return pl.pallas_call(
        _gather_rows_kernel,
        grid=(r // rows,),
        in_specs=[pl.BlockSpec((None, 1, rows), lambda i: (i, 0, 0), memory_space=pltpu.SMEM),
                  pl.BlockSpec(memory_space=pl.ANY)],
        out_specs=pl.BlockSpec(blk, lambda i: (i, 0, 0)),
        out_shape=jax.ShapeDtypeStruct((r,) + src.shape[1:], src.dtype),
        scratch_shapes=[pltpu.SemaphoreType.DMA(())],
        compiler_params=_cparams(("arbitrary",)),
        name="gather_rows",
    )(idx3, src)


def _alibi_slopes(n_heads):
    return jnp.asarray(2.0 ** (-8.0 * np.arange(1, n_heads + 1) / n_heads), dtype=F32)


def _bf16_parts(v):
    a = v.astype(BF16).astype(F32)
    b = (v - a).astype(BF16).astype(F32)
    c = (v - a - b).astype(BF16).astype(F32)
    return a, b, c


def _alibi_tables(seq, slopes2):
    nh = slopes2.shape[0]
    pos = jnp.arange(seq, dtype=jnp.int32)
    pos_parts = (((pos // LANES) * LANES).astype(F32), (pos % LANES).astype(F32))
    cols_q, cols_k = [], []
    for sp in _bf16_parts(slopes2):
        sp_col = jnp.broadcast_to(sp[:, None], (nh, seq))
        for pp in pos_parts:
            pp_row = jnp.broadcast_to(pp[None, :], (nh, seq))
            cols_q += [pp_row, sp_col]
            cols_k += [sp_col, -pp_row]
    pad = jnp.zeros((nh, seq, LANES - len(cols_q)), F32)
    eq = jnp.concatenate([jnp.stack(cols_q, axis=-1), pad], axis=-1).astype(BF16)
    ek = jnp.concatenate([jnp.stack(cols_k, axis=-1), pad], axis=-1).astype(BF16)
    return eq, ek


def _rope_tables(seq):
    inv_freq = ROPE_THETA ** (-jnp.arange(0, B_ROPE, 2, dtype=F32) / B_ROPE)
    ang = jnp.arange(seq, dtype=F32)[:, None] * inv_freq[None, :]
    cos, sin = jnp.cos(ang), jnp.sin(ang)
    cos2 = jnp.concatenate([cos, cos], axis=-1)
    sin2 = jnp.concatenate([sin, sin], axis=-1)
    z64 = jnp.zeros((seq, B_ROPE), F32)
    ka = jnp.concatenate([cos2, z64], axis=-1)
    kb = jnp.concatenate([sin2, z64], axis=-1)
    ones = jnp.ones((seq, B_NOPE), F32)
    z128 = jnp.zeros((seq, B_NOPE), F32)
    qa = jnp.tile(jnp.concatenate([ones, cos2, z64], axis=-1), (1, B_HEADS))
    qb = jnp.tile(jnp.concatenate([z128, sin2, z64], axis=-1), (1, B_HEADS))
    return ka, kb, qa, qb


def _rot_cols(w):
    half = w.shape[-1] // 2
    return jnp.concatenate([-w[..., half:], w[..., :half]], axis=-1)


def _routing_tables(flat_e, flat_g, n_tok):
    n_assign = flat_e.shape[0]
    onehot = (flat_e[:, None] == jnp.arange(N_EXPERTS, dtype=jnp.int32)[None, :]).astype(jnp.int32)
    csum = jnp.cumsum(onehot, axis=0)
    rank = jnp.sum((csum - onehot) * onehot, axis=1)
    sizes = csum[-1]
    padded = ((sizes + MOE_BLOCK - 1) // MOE_BLOCK) * MOE_BLOCK
    pad_end = jnp.cumsum(padded)
    pad_start = pad_end - padded
    dest = (pad_start[flat_e] + rank).astype(jnp.int32)
    n_blocks = -(-n_assign // MOE_BLOCK) + N_EXPERTS
    n_rows = n_blocks * MOE_BLOCK
    tok = jnp.arange(n_assign, dtype=jnp.int32) // TOP_K
    row_tok = jnp.zeros((n_rows,), jnp.int32).at[dest].set(tok)
    row_gate = jnp.zeros((n_rows,), F32).at[dest].set(flat_g)
    block_start = jnp.arange(n_blocks, dtype=jnp.int32) * MOE_BLOCK
    block_e = jnp.minimum(jnp.searchsorted(pad_end, block_start, side='right'), N_EXPERTS - 1).astype(jnp.int32)
    return dest, row_tok, row_gate, block_e


def kernel(x, c, ada_w, ada_b, ln_g, ln_b, ab_w_in, ab_lam, ab_diff_g, ab_q_norm_g, ab_kv_norm_g, ab_w_uq,
           ab_w_ukv, ab_w_out, c_w_in, c_sink, c_w_out, ffn_w_gate, ffn_w_up, ffn_w_down, moe_w_router,
           moe_b_router, moe_w_gate, moe_w_up, moe_w_down):
    bsz, seq, d = x.shape
    depth = ada_w.shape[0]
    n_tok = bsz * seq
    alpha = (2 * depth) ** 0.25

    c_pad = jnp.zeros((8, d), F32).at[:bsz].set(c)
    mods = adaln_all(c_pad, ada_w.reshape(2 * depth, d, 3 * d), ada_b.reshape(2 * depth, 1, 3 * d))
    mods = mods.reshape(2 * depth, 8, 3, d)

    ka, kb, qa, qb = _rope_tables(seq)
    slopes_a2 = _alibi_slopes(A_HEADS) * LOG2E
    eq_a, ek_a = _alibi_tables(seq, slopes_a2)
    slopes_c = _alibi_slopes(C_HEADS)
    scale_a = A_QK_DIM ** -0.5 * LOG2E
    scale_b = (B_NOPE + B_ROPE) ** -0.5 * LOG2E
    scale_c = C_HEAD_DIM ** -0.5

    h = modulate(x, mods, 0, BF16)
    for layer in range(depth):
        i = layer // 2
        l2 = 2 * layer
        last = layer == depth - 1
        if layer % 2 == 0:
            lam_init = 0.8 - 0.6 * math.exp(-0.3 * layer)
            w_in = ab_w_in[i]
            w_kr = w_in[:, 3 * A_COLS + B_Q_LORA + B_KV_LORA:]
            w_in2 = jnp.concatenate([w_in[:, :A_COLS] * scale_a, w_in[:, A_COLS:], _rot_cols(w_kr)],
                                    axis=1).astype(BF16)
            p_ab = inproj_ab(h.reshape(n_tok, d), w_in2, ka, kb, seq)
            o_a = flash_diff(p_ab.reshape(bsz, seq, -1), eq_a, ek_a, slopes_a2, ab_lam[i],
                             ab_diff_g[i].reshape(1, -1), lam_init, bsz, seq)
            w_uq = ab_w_uq[i].reshape(B_Q_LORA, B_HEADS, B_NOPE + B_ROPE) * scale_b
            w_uq2 = jnp.concatenate([w_uq, _rot_cols(w_uq[..., B_NOPE:])], axis=-1)
            w_uq2 = w_uq2.reshape(B_Q_LORA, B_HEADS * 2 * LANES).astype(BF16)
            q_cat = uq_proj(p_ab, 3 * A_COLS // B_Q_LORA, ab_q_norm_g[i].reshape(1, -1), w_uq2, qa, qb, seq)
            k_cat, v_b = ukv_proj(p_ab, (3 * A_COLS + B_Q_LORA) // B_KV_LORA,
                                  (3 * A_COLS + B_Q_LORA + B_KV_LORA) // LANES,
                                  ab_kv_norm_g[i].reshape(1, -1), ab_w_ukv[i].astype(BF16))
            o_b = flash_mla(q_cat.reshape(bsz, seq, -1), k_cat.reshape(bsz, seq, -1),
                            v_b.reshape(bsz, seq, -1), bsz, seq)
            o_cat = jnp.concatenate([o_a, o_b], axis=-1).reshape(n_tok, -1)
            y = matmul(o_cat, ab_w_out[i].astype(BF16), F32)
        else:
            w_in = c_w_in[i]
            nqc = C_HEADS * C_HEAD_DIM
            w_in2 = jnp.concatenate([w_in[:, :nqc] * scale_c, w_in[:, nqc:]], axis=1).astype(BF16)
            p_c = matmul(h.reshape(n_tok, d), w_in2, BF16, tn=1024)
            o_c = window_attn(p_c.reshape(bsz, seq, -1), slopes_c, c_sink[i].astype(F32), bsz, seq)
            y = matmul(o_c.reshape(n_tok, -1), c_w_out[i].astype(BF16), F32)
        moe_next = layer % 2 == 1
        x, h = postnorm(x, y.reshape(bsz, seq, d), mods, l2, ln_g[layer, 0], ln_b[layer, 0], alpha,
                        next_dtype=F32 if moe_next else BF16)

        if not moe_next:
            y = ffn(h.reshape(n_tok, d), ffn_w_gate[i].astype(BF16), ffn_w_up[i].astype(BF16),
                    ffn_w_down[i].astype(BF16))
            pair = False
        else:
            hf = h.reshape(n_tok, d)
            w_r = jnp.zeros((d, LANES), F32).at[:, :N_EXPERTS].set(moe_w_router[i])
            b_r = jnp.zeros((1, LANES), F32).at[0, :N_EXPERTS].set(moe_b_router[i])
            route = router(hf, w_r, b_r)
            flat_e = route[:, :TOP_K].astype(jnp.int32).reshape(-1)
            flat_g = route[:, TOP_K:2 * TOP_K].reshape(-1)
            dest, row_tok, row_gate, block_e = _routing_tables(flat_e, flat_g, n_tok)
            xs = gather_rows(hf.reshape(n_tok, d // LANES, LANES), row_tok).reshape(-1, d)
            gate_rep = jnp.broadcast_to(row_gate[:, None], (row_gate.shape[0], LANES))
            ys = moe_ffn(block_e, xs, gate_rep, moe_w_gate[i].astype(BF16), moe_w_up[i].astype(BF16),
                         moe_w_down[i].astype(BF16))
            y = gather_rows(ys.reshape(-1, d // LANES, LANES), dest).reshape(n_tok, TOP_K * d)
            pair = True
        x, h = postnorm(x, y.reshape(bsz, seq, -1), mods, l2 + 1, ln_g[layer, 1], ln_b[layer, 1], alpha,
                        pair_sum=pair, next_dtype=None if last else BF16)
    return x
```

```python
import functools
import math

import numpy as np
import jax
import jax.numpy as jnp
from jax import lax
from jax.experimental import pallas as pl
from jax.experimental.pallas import tpu as pltpu

F32 = jnp.float32
BF16 = jnp.bfloat16

A_HEADS = 4
A_QK_DIM = 64
A_V_DIM = 128
B_HEADS = 4
B_Q_LORA = 512
B_KV_LORA = 256
B_NOPE = 128
B_ROPE = 64
B_V_DIM = 128
ROPE_THETA = 10000.0
C_HEADS = 16
C_KV_HEADS = 4
C_GROUP = C_HEADS // C_KV_HEADS
C_HEAD_DIM = 128
WINDOW = 128
N_EXPERTS = 8
TOP_K = 2
MOE_BLOCK = 512
LN_EPS = 1e-5
RMS_EPS = 1e-6

A_COLS = A_HEADS * 2 * A_QK_DIM
LANES = 128
VMEM_LIMIT = 56 * 1024 * 1024
NEG_INF = float("-inf")
LOG2E = math.log2(math.e)


def _cparams(sem, vmem=VMEM_LIMIT):
    return pltpu.CompilerParams(dimension_semantics=sem, vmem_limit_bytes=vmem)


def _adaln_kernel(c_ref, w_ref, b_ref, o_ref):
    c = c_ref[...]
    ca = c * jax.nn.sigmoid(c)
    o_ref[...] = jnp.dot(ca, w_ref[...], preferred_element_type=F32,
                         precision=lax.Precision.HIGHEST) + b_ref[...]


def adaln_all(c_pad, ada_w, ada_b):
    l2, d, n = ada_w.shape
    tn = 768
    return pl.pallas_call(
        _adaln_kernel,
        grid=(l2, n // tn),
        in_specs=[pl.BlockSpec((8, d), lambda l, j: (0, 0)),
                  pl.BlockSpec((None, d, tn), lambda l, j: (l, 0, j)),
                  pl.BlockSpec((None, 1, tn), lambda l, j: (l, 0, j))],
        out_specs=pl.BlockSpec((None, 8, tn), lambda l, j: (l, 0, j)),
        out_shape=jax.ShapeDtypeStruct((l2, 8, n), F32),
        compiler_params=_cparams(("parallel", "parallel")),
        name="adaln",
    )(c_pad, ada_w, ada_b)


def _modulate_kernel(x_ref, mod_ref, h_ref):
    x = x_ref[...]
    h_ref[...] = (x * (1.0 + mod_ref[1:2, :]) + mod_ref[0:1, :]).astype(h_ref.dtype)


def modulate(x, mods, l2, out_dtype):
    b, s, d = x.shape
    ts = 512
    return pl.pallas_call(
        _modulate_kernel,
        grid=(b, s // ts),
        in_specs=[pl.BlockSpec((None, ts, d), lambda i, j: (i, j, 0)),
                  pl.BlockSpec((None, None, 3, d), lambda i, j: (l2, i, 0, 0))],
        out_specs=pl.BlockSpec((None, ts, d), lambda i, j: (i, j, 0)),
        out_shape=jax.ShapeDtypeStruct((b, s, d), out_dtype),
        compiler_params=_cparams(("parallel", "parallel")),
        name="modulate",
    )(x, mods)


def _rows_to_tiles(ref, x):
    for s in range(x.shape[1] // LANES):
        ref[:, s, :] = x[:, s * LANES:(s + 1) * LANES].astype(ref.dtype)


def _tiles_to_rows(read_tile, n_sub):
    return jnp.concatenate([read_tile(s) for s in range(n_sub)], axis=1)


def _postnorm_kernel(*refs, alpha, routed, has_next, next_tiles):
    refs = list(refs)
    x_ref, y_ref, modc_ref, g_ref, b_ref = refs[:5]
    rest = refs[5:]
    route_ref = rest.pop(0) if routed else None
    modn_ref = rest.pop(0) if has_next else None
    xo_ref = rest.pop(0)
    h_ref = rest.pop(0) if has_next else None
    x = x_ref[...]
    d = x.shape[-1]
    if routed:
        g1 = route_ref[:, TOP_K:TOP_K + 1]
        g2 = route_ref[:, TOP_K + 1:TOP_K + 2]
        y = _tiles_to_rows(lambda s: g1 * y_ref[:, 0, s, :] + g2 * y_ref[:, 1, s, :], d // LANES)
    else:
        y = y_ref[...].astype(F32)
    z = alpha * x + (1.0 + modc_ref[2:3, :]) * y
    mu = jnp.mean(z, axis=-1, keepdims=True)
    zc = z - mu
    var = jnp.mean(zc * zc, axis=-1, keepdims=True)
    xn = zc * lax.rsqrt(var + LN_EPS) * g_ref[...] + b_ref[...]
    xo_ref[...] = xn
    if has_next:
        h = xn * (1.0 + modn_ref[1:2, :]) + modn_ref[0:1, :]
        if next_tiles:
            _rows_to_tiles(h_ref, h)
        else:
            h_ref[...] = h.astype(h_ref.dtype)


def postnorm(x, y, mods, l2, ln_g, ln_b, alpha, *, route=None, next_dtype=None, next_tiles=False):
    b, s, d = x.shape
    ts = 256
    has_next = next_dtype is not None
    routed = route is not None
    nsub = d // LANES
    if routed:
        y_spec = pl.BlockSpec((None, ts, TOP_K, nsub, LANES), lambda i, j: (i, j, 0, 0, 0))
    else:
        y_spec = pl.BlockSpec((None, ts, d), lambda i, j: (i, j, 0))
    in_specs = [pl.BlockSpec((None, ts, d), lambda i, j: (i, j, 0)),
                y_spec,
                pl.BlockSpec((None, None, 3, d), lambda i, j: (l2, i, 0, 0)),
                pl.BlockSpec((1, d), lambda i, j: (0, 0)),
                pl.BlockSpec((1, d), lambda i, j: (0, 0))]
    args = [x, y, mods, ln_g.reshape(1, d), ln_b.reshape(1, d)]
    if routed:
        in_specs.append(pl.BlockSpec((None, ts, LANES), lambda i, j: (i, j, 0)))
        args.append(route)
    out_specs = [pl.BlockSpec((None, ts, d), lambda i, j: (i, j, 0))]
    out_shape = [jax.ShapeDtypeStruct((b, s, d), F32)]
    if has_next:
        in_specs.append(pl.BlockSpec((None, None, 3, d), lambda i, j: (l2 + 1, i, 0, 0)))
        args.append(mods)
        if next_tiles:
            out_specs.append(pl.BlockSpec((None, ts, nsub, LANES), lambda i, j: (i, j, 0, 0)))
            out_shape.append(jax.ShapeDtypeStruct((b, s, nsub, LANES), next_dtype))
        else:
            out_specs.append(pl.BlockSpec((None, ts, d), lambda i, j: (i, j, 0)))
            out_shape.append(jax.ShapeDtypeStruct((b, s, d), next_dtype))
    out = pl.pallas_call(
        functools.partial(_postnorm_kernel, alpha=alpha, routed=routed, has_next=has_next,
                          next_tiles=next_tiles),
        grid=(b, s // ts),
        in_specs=in_specs,
        out_specs=out_specs,
        out_shape=out_shape,
        compiler_params=_cparams(("parallel", "parallel")),
        name="postnorm",
    )(*args)
    return (out[0], out[1]) if has_next else (out[0], None)


def _rms_rows(a_ref, g_ref):
    a = a_ref[...].astype(F32)
    return (a * lax.rsqrt(jnp.mean(a * a, axis=-1, keepdims=True) + RMS_EPS) * g_ref[...]).astype(BF16)


def _matmul_kernel(a_ref, w_ref, o_ref):
    o_ref[...] = jnp.dot(a_ref[...], w_ref[...], preferred_element_type=F32).astype(o_ref.dtype)


def matmul(a, w, out_dtype, tm=512, tn=None):
    m, k = a.shape
    n = w.shape[1]
    tn = n if tn is None else tn
    return pl.pallas_call(
        _matmul_kernel,
        grid=(m // tm, n // tn),
        in_specs=[pl.BlockSpec((tm, k), lambda i, j: (i, 0)),
                  pl.BlockSpec((k, tn), lambda i, j: (0, j))],
        out_specs=pl.BlockSpec((tm, tn), lambda i, j: (i, j)),
        out_shape=jax.ShapeDtypeStruct((m, n), out_dtype),
        compiler_params=_cparams(("parallel", "parallel")),
        name="matmul",
    )(a, w)


def _inproj_ab_kernel(a_ref, w_ref, ra_ref, rb_ref, o_ref):
    acc = jnp.dot(a_ref[...], w_ref[...], preferred_element_type=F32)
    n = acc.shape[1]
    o_ref[:, :n - LANES] = acc[:, :n - LANES].astype(o_ref.dtype)
    last = acc[:, n - LANES:]
    roped = last * ra_ref[...] + pltpu.roll(last, LANES // 2, 1) * rb_ref[...]
    o_ref[:, n - LANES:] = roped.astype(o_ref.dtype)


def inproj_ab(h, w, rope_a, rope_b, seq):
    m, k = h.shape
    n = w.shape[1]
    tm = 512
    nsb = seq // tm
    return pl.pallas_call(
        _inproj_ab_kernel,
        grid=(m // tm,),
        in_specs=[pl.BlockSpec((tm, k), lambda i: (i, 0)),
                  pl.BlockSpec((k, n), lambda i: (0, 0)),
                  pl.BlockSpec((tm, LANES), lambda i: (i % nsb, 0)),
                  pl.BlockSpec((tm, LANES), lambda i: (i % nsb, 0))],
        out_specs=pl.BlockSpec((tm, n), lambda i: (i, 0)),
        out_shape=jax.ShapeDtypeStruct((m, n), BF16),
        compiler_params=_cparams(("parallel",)),
        name="inproj_ab",
    )(h, w, rope_a, rope_b)


def _uq_kernel(a_ref, g_ref, w_ref, ra_ref, rb_ref, o_ref):
    an = _rms_rows(a_ref, g_ref)
    acc = jnp.dot(an, w_ref[...], preferred_element_type=F32)
    n = acc.shape[1]
    o_ref[...] = (acc * ra_ref[...] + pltpu.roll(acc, n - B_ROPE, 1) * rb_ref[...]).astype(o_ref.dtype)


def uq_proj(p_ab, col_block, g, w, rope_a, rope_b, seq):
    m = p_ab.shape[0]
    k, n = w.shape
    tm = 512
    nsb = seq // tm
    return pl.pallas_call(
        _uq_kernel,
        grid=(m // tm,),
        in_specs=[pl.BlockSpec((tm, k), lambda i: (i, col_block)),
                  pl.BlockSpec((1, k), lambda i: (0, 0)),
                  pl.BlockSpec((k, n), lambda i: (0, 0)),
                  pl.BlockSpec((tm, n), lambda i: (i % nsb, 0)),
                  pl.BlockSpec((tm, n), lambda i: (i % nsb, 0))],
        out_specs=pl.BlockSpec((tm, n), lambda i: (i, 0)),
        out_shape=jax.ShapeDtypeStruct((m, n), BF16),
        compiler_params=_cparams(("parallel",)),
        name="uq_proj",
    )(p_ab, g, w, rope_a, rope_b)


def _ukv_kernel(a_ref, g_ref, w_ref, kr_ref, k_ref, v_ref):
    an = _rms_rows(a_ref, g_ref)
    acc = jnp.dot(an, w_ref[...], preferred_element_type=F32)
    kr = kr_ref[...]
    for h in range(B_HEADS):
        base = h * (B_NOPE + B_V_DIM)
        k_ref[:, 2 * h * LANES:(2 * h + 1) * LANES] = acc[:, base:base + B_NOPE].astype(k_ref.dtype)
        k_ref[:, (2 * h + 1) * LANES:(2 * h + 2) * LANES] = kr
        v_ref[:, h * B_V_DIM:(h + 1) * B_V_DIM] = acc[:, base + B_NOPE:base + B_NOPE + B_V_DIM].astype(v_ref.dtype)


def ukv_proj(p_ab, ckv_block, kr_block, g, w):
    m = p_ab.shape[0]
    k, n = w.shape
    tm = 512
    return pl.pallas_call(
        _ukv_kernel,
        grid=(m // tm,),
        in_specs=[pl.BlockSpec((tm, k), lambda i: (i, ckv_block)),
                  pl.BlockSpec((1, k), lambda i: (0, 0)),
                  pl.BlockSpec((k, n), lambda i: (0, 0)),
                  pl.BlockSpec((tm, LANES), lambda i: (i, kr_block))],
        out_specs=[pl.BlockSpec((tm, B_HEADS * 2 * LANES), lambda i: (i, 0)),
                   pl.BlockSpec((tm, B_HEADS * B_V_DIM), lambda i: (i, 0))],
        out_shape=[jax.ShapeDtypeStruct((m, B_HEADS * 2 * LANES), BF16),
                   jax.ShapeDtypeStruct((m, B_HEADS * B_V_DIM), BF16)],
        compiler_params=_cparams(("parallel",)),
        name="ukv_proj",
    )(p_ab, g, w, p_ab)


_NT = (((1,), (1,)), ((), ()))


def _ones_block(rows):
    return jnp.ones((rows, LANES), BF16)


def _online_softmax_step(s, v_aug, m_ref, acc_ref):
    tk = s.shape[1]
    m_prev = m_ref[...]
    m_new = jnp.maximum(m_prev, jnp.max(s, axis=1, keepdims=True))
    alpha = jnp.exp2(m_prev - m_new)
    p = jnp.exp2(s - jnp.concatenate([m_new] * (tk // LANES), axis=1))
    pv = jnp.dot(p.astype(v_aug.dtype), v_aug, preferred_element_type=F32)
    acc_ref[...] = jnp.concatenate([alpha, alpha], axis=1) * acc_ref[...] + pv
    m_ref[...] = m_new


def _softmax_result(acc_ref):
    acc = acc_ref[...]
    return acc[:, :LANES] / acc[:, LANES:]


def _flash_diff_kernel(slopes_ref, q_ref, eq_ref, k_ref, ek_ref, v_ref, lam_ref, g_ref, o_ref,
                       qaug_sc, m_sc, acc_sc, *, tk, lam_init):
    tq = q_ref.shape[0]
    seq = k_ref.shape[0]
    qi = pl.program_id(2)
    q0 = qi * tq
    q = q_ref[...]
    eq = eq_ref[...]
    lane = lax.broadcasted_iota(jnp.int32, q.shape, 1)
    zero = jnp.zeros_like(q)
    for mp in range(2):
        qm = jnp.where((lane < A_QK_DIM) == (mp == 0), q, zero)
        qaug_sc[0, mp] = jnp.concatenate([qm, -eq], axis=1)
        qaug_sc[1, mp] = jnp.concatenate([qm, eq], axis=1)
    m_sc[...] = jnp.full(m_sc.shape, NEG_INF, F32)
    acc_sc[...] = jnp.zeros(acc_sc.shape, F32)
    ones_col = _ones_block(tk)

    def chunk(c, side, diagonal):
        k0 = pl.multiple_of(c * tk, tk)
        kc = jnp.concatenate([k_ref[pl.ds(k0, tk), :], ek_ref[pl.ds(k0, tk), :]], axis=1)
        v_aug = jnp.concatenate([v_ref[pl.ds(k0, tk), :], ones_col], axis=1)
        if diagonal:
            rel = (lax.broadcasted_iota(jnp.int32, (tq, tk), 0)
                   - lax.broadcasted_iota(jnp.int32, (tq, tk), 1)) + (q0 - k0)
            corr = (2.0 * slopes_ref[pl.program_id(1)]) * jnp.maximum(rel, 0).astype(F32)
        for mp in range(2):
            s = lax.dot_general(qaug_sc[side, mp], kc, _NT, preferred_element_type=F32)
            if diagonal:
                s = s - corr
            _online_softmax_step(s, v_aug, m_sc.at[mp], acc_sc.at[mp])

    def left(c, carry):
        chunk(c, 0, False)
        return carry

    def right(c, carry):
        chunk(c, 1, False)
        return carry

    cd = q0 // tk
    lax.fori_loop(0, cd, left, 0)
    chunk(cd, 1, True)
    lax.fori_loop(cd + 1, seq // tk, right, 0)

    o0 = _softmax_result(acc_sc.at[0])
    o1 = _softmax_result(acc_sc.at[1])
    lam = lam_ref[...]
    lam_full = (jnp.exp(jnp.sum(lam[0:1] * lam[1:2], axis=1, keepdims=True))
                - jnp.exp(jnp.sum(lam[2:3] * lam[3:4], axis=1, keepdims=True)) + lam_init)
    o = o0 - lam_full * o1
    on = o * lax.rsqrt(jnp.mean(o * o, axis=-1, keepdims=True) + RMS_EPS) * g_ref[...]
    o_ref[...] = (on * (1.0 - lam_init)).astype(o_ref.dtype)


def flash_diff(p_ab, eq, ek, slopes2, lam, diff_g, lam_init, bsz, seq):
    tq, tk = 1024, 1024
    assert tk % tq == 0
    nh = A_HEADS
    return pl.pallas_call(
        functools.partial(_flash_diff_kernel, tk=tk, lam_init=lam_init),
        grid=(bsz, nh, seq // tq),
        in_specs=[pl.BlockSpec(memory_space=pltpu.SMEM),
                  pl.BlockSpec((None, tq, LANES), lambda b, h, i: (b, i, h)),
                  pl.BlockSpec((None, tq, LANES), lambda b, h, i: (h, i, 0)),
                  pl.BlockSpec((None, seq, LANES), lambda b, h, i: (b, 0, nh + h)),
                  pl.BlockSpec((None, seq, LANES), lambda b, h, i: (h, 0, 0)),
                  pl.BlockSpec((None, seq, LANES), lambda b, h, i: (b, 0, 2 * nh + h)),
                  pl.BlockSpec((4, A_QK_DIM), lambda b, h, i: (0, 0)),
                  pl.BlockSpec((1, A_V_DIM), lambda b, h, i: (0, 0))],
        out_specs=pl.BlockSpec((None, tq, A_V_DIM), lambda b, h, i: (b, i, h)),
        out_shape=jax.ShapeDtypeStruct((bsz, seq, nh * A_V_DIM), BF16),
        scratch_shapes=[pltpu.VMEM((2, 2, tq, 2 * LANES), BF16),
                        pltpu.VMEM((2, tq, LANES), F32),
                        pltpu.VMEM((2, tq, 2 * LANES), F32)],
        compiler_params=_cparams(("parallel", "parallel", "parallel")),
        name="flash_diff",
    )(slopes2, p_ab, eq, p_ab, ek, p_ab, lam, diff_g)


def _flash_mla_kernel(q_ref, k_ref, v_ref, o_ref, m_sc, acc_sc, *, tk):
    seq = k_ref.shape[0]
    q = q_ref[...]
    m_sc[...] = jnp.full(m_sc.shape, NEG_INF, F32)
    acc_sc[...] = jnp.zeros(acc_sc.shape, F32)
    ones_col = _ones_block(tk)

    def body(c, carry):
        k0 = pl.multiple_of(c * tk, tk)
        s = lax.dot_general(q, k_ref[pl.ds(k0, tk), :], _NT, preferred_element_type=F32)
        v_aug = jnp.concatenate([v_ref[pl.ds(k0, tk), :], ones_col], axis=1)
        _online_softmax_step(s, v_aug, m_sc, acc_sc)
        return carry

    lax.fori_loop(0, seq // tk, body, 0)
    o_ref[...] = _softmax_result(acc_sc).astype(o_ref.dtype)


def flash_mla(q_cat, k_cat, v_b, bsz, seq):
    tq, tk = 1024, 1024
    kd = 2 * LANES
    return pl.pallas_call(
        functools.partial(_flash_mla_kernel, tk=tk),
        grid=(bsz, B_HEADS, seq // tq),
        in_specs=[pl.BlockSpec((None, tq, kd), lambda b, h, i: (b, i, h)),
                  pl.BlockSpec((None, seq, kd), lambda b, h, i: (b, 0, h)),
                  pl.BlockSpec((None, seq, B_V_DIM), lambda b, h, i: (b, 0, h))],
        out_specs=pl.BlockSpec((None, tq, B_V_DIM), lambda b, h, i: (b, i, h)),
        out_shape=jax.ShapeDtypeStruct((bsz, seq, B_HEADS * B_V_DIM), BF16),
        scratch_shapes=[pltpu.VMEM((tq, LANES), F32), pltpu.VMEM((tq, 2 * LANES), F32)],
        compiler_params=_cparams(("parallel", "parallel", "parallel")),
        name="flash_mla",
    )(q_cat, k_cat, v_b)


def _window_kernel(slopes_ref, sink_ref, q_ref, k_ref, v_ref, o_ref):
    tq = q_ref.shape[0]
    seq = k_ref.shape[0]
    blk = WINDOW
    kw = 3 * blk
    n = pl.program_id(1)
    q0 = pl.program_id(2) * tq
    rel = (lax.broadcasted_iota(jnp.int32, (blk, kw), 1) - lax.broadcasted_iota(jnp.int32, (blk, kw), 0))
    ones_blk = _ones_block(kw)
    heads = [n * C_GROUP + g for g in range(C_GROUP)]
    slope_rows = jnp.concatenate([jnp.full((blk, kw), slopes_ref[hd], F32) for hd in heads], axis=0)
    sink_rows = jnp.concatenate([jnp.full((blk, LANES), sink_ref[hd], F32) for hd in heads], axis=0)
    for sb in range(tq // blk):
        t0 = q0 + sb * blk
        start = pl.multiple_of(jnp.clip(t0 - blk, 0, seq - kw), blk)
        kc = k_ref[pl.ds(start, kw), :]
        v_aug = jnp.concatenate([v_ref[pl.ds(start, kw), :], ones_blk], axis=1)
        dist = jnp.abs(rel + (start - t0))
        dist_rows = jnp.concatenate([dist] * C_GROUP, axis=0)
        qs = jnp.concatenate([q_ref[sb * blk:(sb + 1) * blk, g * C_HEAD_DIM:(g + 1) * C_HEAD_DIM]
                              for g in range(C_GROUP)], axis=0)
        s = lax.dot_general(qs, kc, _NT, preferred_element_type=F32) - slope_rows * dist_rows.astype(F32)
        s = jnp.where(dist_rows <= WINDOW, s, NEG_INF)
        m = jnp.maximum(jnp.max(s, axis=1, keepdims=True), sink_rows)
        e = jnp.exp(s - jnp.concatenate([m] * (kw // LANES), axis=1))
        pv = jnp.dot(e.astype(v_aug.dtype), v_aug, preferred_element_type=F32)
        o = pv[:, :C_HEAD_DIM] / (pv[:, C_HEAD_DIM:] + jnp.exp(sink_rows - m))
        for g in range(C_GROUP):
            o_ref[sb * blk:(sb + 1) * blk, g * C_HEAD_DIM:(g + 1) * C_HEAD_DIM] = (
                o[g * blk:(g + 1) * blk].astype(o_ref.dtype))


def window_attn(p_c, slopes, sink, bsz, seq):
    tq = 512
    gw = C_GROUP * C_HEAD_DIM
    nq = C_HEADS * C_HEAD_DIM // LANES
    return pl.pallas_call(
        _window_kernel,
        grid=(bsz, C_KV_HEADS, seq // tq),
        in_specs=[pl.BlockSpec(memory_space=pltpu.SMEM),
                  pl.BlockSpec(memory_space=pltpu.SMEM),
                  pl.BlockSpec((None, tq, gw), lambda b, n, i: (b, i, n)),
                  pl.BlockSpec((None, seq, C_HEAD_DIM), lambda b, n, i: (b, 0, nq + n)),
                  pl.BlockSpec((None, seq, C_HEAD_DIM), lambda b, n, i: (b, 0, nq + C_KV_HEADS + n))],
        out_specs=pl.BlockSpec((None, tq, gw), lambda b, n, i: (b, i, n)),
        out_shape=jax.ShapeDtypeStruct((bsz, seq, C_HEADS * C_HEAD_DIM), BF16),
        compiler_params=_cparams(("parallel", "parallel", "parallel")),
        name="window_attn",
    )(slopes, sink, p_c, p_c, p_c)


def _swiglu_step(x, wg_ref, wu_ref, wd_ref, acc_ref):
    g = jnp.dot(x, wg_ref[...], preferred_element_type=F32)
    u = jnp.dot(x, wu_ref[...], preferred_element_type=F32)
    a = (g * jax.nn.sigmoid(g) * u).astype(BF16)
    acc_ref[...] += jnp.dot(a, wd_ref[...], preferred_element_type=F32)


def _ffn_kernel(x_ref, wg_ref, wu_ref, wd_ref, o_ref, acc_ref):
    j = pl.program_id(1)

    @pl.when(j == 0)
    def _():
        acc_ref[...] = jnp.zeros(acc_ref.shape, F32)

    _swiglu_step(x_ref[...], wg_ref, wu_ref, wd_ref, acc_ref)

    @pl.when(j == pl.num_programs(1) - 1)
    def _():
        o_ref[...] = acc_ref[...].astype(o_ref.dtype)


def ffn(h, wg, wu, wd):
    m, d = h.shape
    f = wg.shape[1]
    tm, tf = 512, 512
    return pl.pallas_call(
        _ffn_kernel,
        grid=(m // tm, f // tf),
        in_specs=[pl.BlockSpec((tm, d), lambda i, j: (i, 0)),
                  pl.BlockSpec((d, tf), lambda i, j: (0, j)),
                  pl.BlockSpec((d, tf), lambda i, j: (0, j)),
                  pl.BlockSpec((tf, d), lambda i, j: (j, 0))],
        out_specs=pl.BlockSpec((tm, d), lambda i, j: (i, 0)),
        out_shape=jax.ShapeDtypeStruct((m, d), F32),
        scratch_shapes=[pltpu.VMEM((tm, d), F32)],
        compiler_params=_cparams(("parallel", "arbitrary")),
        name="ffn",
    )(h, wg, wu, wd)


def _moe_ffn_kernel(be_ref, tok_ref, tokn_ref, h_ref, wg_ref, wu_ref, wd_ref, o_ref,
                    xg_ref, xb_ref, acc_ref, sem):
    del be_ref
    i = pl.program_id(0)
    j = pl.program_id(1)
    rows, nsub, _ = xg_ref.shape

    def issue_gather(idx_ref):
        def body(r, carry):
            pltpu.make_async_copy(h_ref.at[pl.ds(idx_ref[0, r], 1)], xg_ref.at[pl.ds(r, 1)], sem).start()
            return carry
        lax.fori_loop(0, rows, body, 0)

    @pl.when(j == 0)
    def _():
        @pl.when(i == 0)
        def _():
            issue_gather(tok_ref)
        pltpu.make_async_copy(h_ref.at[pl.ds(0, rows)], xg_ref, sem).wait()
        for s in range(nsub):
            xb_ref[:, s * LANES:(s + 1) * LANES] = xg_ref[:, s, :].astype(BF16)
        acc_ref[...] = jnp.zeros(acc_ref.shape, F32)

    @pl.when((j == 1) & (i + 1 < pl.num_programs(0)))
    def _():
        issue_gather(tokn_ref)

    _swiglu_step(xb_ref[...], wg_ref, wu_ref, wd_ref, acc_ref)

    @pl.when(j == pl.num_programs(1) - 1)
    def _():
        _rows_to_tiles(o_ref, acc_ref[...])


def moe_ffn(block_e, row_tok, h3, wg, wu, wd):
    nsub = h3.shape[1]
    d = nsub * LANES
    r = row_tok.shape[0]
    f = wg.shape[2]
    tm, tf = MOE_BLOCK, 256
    assert f // tf >= 2
    nblk = r // tm
    tok3 = row_tok.reshape(nblk, 1, tm)
    grid_spec = pltpu.PrefetchScalarGridSpec(
        num_scalar_prefetch=1,
        grid=(nblk, f // tf),
        in_specs=[pl.BlockSpec((None, 1, tm), lambda i, j, be: (i, 0, 0), memory_space=pltpu.SMEM),
                  pl.BlockSpec((None, 1, tm), lambda i, j, be: (jnp.minimum(i + 1, nblk - 1), 0, 0),
                               memory_space=pltpu.SMEM),
                  pl.BlockSpec(memory_space=pl.ANY),
                  pl.BlockSpec((None, d, tf), lambda i, j, be: (be[i], 0, j)),
                  pl.BlockSpec((None, d, tf), lambda i, j, be: (be[i], 0, j)),
                  pl.BlockSpec((None, tf, d), lambda i, j, be: (be[i], j, 0))],
        out_specs=pl.BlockSpec((tm, nsub, LANES), lambda i, j, be: (i, 0, 0)),
        scratch_shapes=[pltpu.VMEM((tm, nsub, LANES), F32), pltpu.VMEM((tm, d), BF16),
                        pltpu.VMEM((tm, d), F32), pltpu.SemaphoreType.DMA(())],
    )
    return pl.pallas_call(
        _moe_ffn_kernel,
        grid_spec=grid_spec,
        out_shape=jax.ShapeDtypeStruct((r, nsub, LANES), F32),
        compiler_params=_cparams(("arbitrary", "arbitrary")),
        name="moe_ffn",
    )(block_e, tok3, tok3, h3, wg, wu, wd)


def _router_kernel(h_ref, w_ref, b_ref, o_ref):
    h = _tiles_to_rows(lambda s: h_ref[:, s, :], h_ref.shape[1])
    logits = jnp.dot(h, w_ref[...], preferred_element_type=F32,
                     precision=lax.Precision.HIGHEST) + b_ref[...]
    lane = lax.broadcasted_iota(jnp.int32, logits.shape, 1)
    logits = jnp.where(lane < N_EXPERTS, logits, NEG_INF)
    m1 = jnp.max(logits, axis=1, keepdims=True)
    i1 = jnp.min(jnp.where(logits == m1, lane, LANES), axis=1, keepdims=True)
    rest = jnp.where(lane == i1, NEG_INF, logits)
    m2 = jnp.max(rest, axis=1, keepdims=True)
    i2 = jnp.min(jnp.where(rest == m2, lane, LANES), axis=1, keepdims=True)
    e2 = jnp.exp(m2 - m1)
    g1 = 1.0 / (1.0 + e2)
    g2 = e2 / (1.0 + e2)
    out = jnp.where(lane == 0, i1.astype(F32),
                    jnp.where(lane == 1, i2.astype(F32),
                              jnp.where(lane == 2, g1, jnp.where(lane == 3, g2, 0.0))))
    o_ref[...] = out


def router(h3, w_pad, b_pad):
    m, nsub, _ = h3.shape
    d = nsub * LANES
    tm = 512
    return pl.pallas_call(
        _router_kernel,
        grid=(m // tm,),
        in_specs=[pl.BlockSpec((tm, nsub, LANES), lambda i: (i, 0, 0)),
                  pl.BlockSpec((d, LANES), lambda i: (0, 0)),
                  pl.BlockSpec((1, LANES), lambda i: (0, 0))],
        out_specs=pl.BlockSpec((tm, LANES), lambda i: (i, 0)),
        out_shape=jax.ShapeDtypeStruct((m, LANES), F32),
        compiler_params=_cparams(("parallel",)),
        name="router",
    )(h3, w_pad, b_pad)


def _gather_rows_kernel(idx_ref, src_ref, dst_ref, sem):
    rows = dst_ref.shape[0]

    def issue(r, carry):
        pltpu.make_async_copy(src_ref.at[pl.ds(idx_ref[0, r], 1)], dst_ref.at[pl.ds(r, 1)], sem).start()
        return carry

    lax.fori_loop(0, rows, issue, 0)
    pltpu.make_async_copy(src_ref.at[pl.ds(0, rows)], dst_ref, sem).wait()


def gather_rows(src, idx):
    rows = 512
    r = idx.shape[0]
    idx3 = idx.reshape(r // rows, 1, rows)
    blk = (rows,) + src.shape[1:]
    return pl.pallas_call(
        _gather_rows_kernel,
        grid=(r // rows,),
        in_specs=[pl.BlockSpec((None, 1, rows), lambda i: (i, 0, 0), memory_space=pltpu.SMEM),
                  pl.BlockSpec(memory_space=pl.ANY)],
        out_specs=pl.BlockSpec(blk, lambda i: (i, 0, 0)),
        out_shape=jax.ShapeDtypeStruct((r,) + src.shape[1:], src.dtype),
        scratch_shapes=[pltpu.SemaphoreType.DMA(())],
        compiler_params=_cparams(("arbitrary",)),
        name="gather_rows",
    )(idx3, src)


def _alibi_slopes(n_heads):
    return jnp.asarray(2.0 ** (-8.0 * np.arange(1, n_heads + 1) / n_heads), dtype=F32)


def _bf16_parts(v):
    a = v.astype(BF16).astype(F32)
    b = (v - a).astype(BF16).astype(F32)
    c = (v - a - b).astype(BF16).astype(F32)
    return a, b, c


def _alibi_tables(seq, slopes2):
    nh = slopes2.shape[0]
    pos = jnp.arange(seq, dtype=jnp.int32)
    pos_parts = (((pos // LANES) * LANES).astype(F32), (pos % LANES).astype(F32))
    cols_q, cols_k = [], []
    for sp in _bf16_parts(slopes2):
        sp_col = jnp.broadcast_to(sp[:, None], (nh, seq))
        for pp in pos_parts:
            pp_row = jnp.broadcast_to(pp[None, :], (nh, seq))
            cols_q += [pp_row, sp_col]
            cols_k += [sp_col, -pp_row]
    pad = jnp.zeros((nh, seq, LANES - len(cols_q)), F32)
    eq = jnp.concatenate([jnp.stack(cols_q, axis=-1), pad], axis=-1).astype(BF16)
    ek = jnp.concatenate([jnp.stack(cols_k, axis=-1), pad], axis=-1).astype(BF16)
    return eq, ek


def _rope_tables(seq):
    inv_freq = ROPE_THETA ** (-jnp.arange(0, B_ROPE, 2, dtype=F32) / B_ROPE)
    ang = jnp.arange(seq, dtype=F32)[:, None] * inv_freq[None, :]
    cos, sin = jnp.cos(ang), jnp.sin(ang)
    cos2 = jnp.concatenate([cos, cos], axis=-1)
    sin2 = jnp.concatenate([sin, sin], axis=-1)
    z64 = jnp.zeros((seq, B_ROPE), F32)
    ka = jnp.concatenate([cos2, z64], axis=-1)
    kb = jnp.concatenate([sin2, z64], axis=-1)
    ones = jnp.ones((seq, B_NOPE), F32)
    z128 = jnp.zeros((seq, B_NOPE), F32)
    qa = jnp.tile(jnp.concatenate([ones, cos2, z64], axis=-1), (1, B_HEADS))
    qb = jnp.tile(jnp.concatenate([z128, sin2, z64], axis=-1), (1, B_HEADS))
    return ka, kb, qa, qb


def _rot_cols(w):
    half = w.shape[-1] // 2
    return jnp.concatenate([-w[..., half:], w[..., :half]], axis=-1)


def _routing_tables(flat_e):
    n_assign = flat_e.shape[0]
    onehot = (flat_e[:, None] == jnp.arange(N_EXPERTS, dtype=jnp.int32)[None, :]).astype(jnp.int32)
    csum = jnp.cumsum(onehot, axis=0)
    rank = jnp.sum((csum - onehot) * onehot, axis=1)
    sizes = csum[-1]
    padded = ((sizes + MOE_BLOCK - 1) // MOE_BLOCK) * MOE_BLOCK
    pad_end = jnp.cumsum(padded)
    pad_start = pad_end - padded
    dest = (pad_start[flat_e] + rank).astype(jnp.int32)
    n_blocks = -(-n_assign // MOE_BLOCK) + N_EXPERTS
    n_rows = n_blocks * MOE_BLOCK
    tok = jnp.arange(n_assign, dtype=jnp.int32) // TOP_K
    row_tok = jnp.zeros((n_rows,), jnp.int32).at[dest].set(tok)
    block_start = jnp.arange(n_blocks, dtype=jnp.int32) * MOE_BLOCK
    block_e = jnp.minimum(jnp.searchsorted(pad_end, block_start, side='right'), N_EXPERTS - 1).astype(jnp.int32)
    return dest, row_tok, block_e


def kernel(x, c, ada_w, ada_b, ln_g, ln_b, ab_w_in, ab_lam, ab_diff_g, ab_q_norm_g, ab_kv_norm_g, ab_w_uq,
           ab_w_ukv, ab_w_out, c_w_in, c_sink, c_w_out, ffn_w_gate, ffn_w_up, ffn_w_down, moe_w_router,
           moe_b_router, moe_w_gate, moe_w_up, moe_w_down):
    bsz, seq, d = x.shape
    depth = ada_w.shape[0]
    n_tok = bsz * seq
    alpha = (2 * depth) ** 0.25

    c_pad = jnp.zeros((8, d), F32).at[:bsz].set(c)
    mods = adaln_all(c_pad, ada_w.reshape(2 * depth, d, 3 * d), ada_b.reshape(2 * depth, 1, 3 * d))
    mods = mods.reshape(2 * depth, 8, 3, d)

    ka, kb, qa, qb = _rope_tables(seq)
    slopes_a2 = _alibi_slopes(A_HEADS) * LOG2E
    eq_a, ek_a = _alibi_tables(seq, slopes_a2)
    slopes_c = _alibi_slopes(C_HEADS)
    scale_a = A_QK_DIM ** -0.5 * LOG2E
    scale_b = (B_NOPE + B_ROPE) ** -0.5 * LOG2E
    scale_c = C_HEAD_DIM ** -0.5

    h = modulate(x, mods, 0, BF16)
    for layer in range(depth):
        i = layer // 2
        l2 = 2 * layer
        last = layer == depth - 1
        if layer % 2 == 0:
            lam_init = 0.8 - 0.6 * math.exp(-0.3 * layer)
            w_in = ab_w_in[i]
            w_kr = w_in[:, 3 * A_COLS + B_Q_LORA + B_KV_LORA:]
            w_in2 = jnp.concatenate([w_in[:, :A_COLS] * scale_a, w_in[:, A_COLS:], _rot_cols(w_kr)],
                                    axis=1).astype(BF16)
            p_ab = inproj_ab(h.reshape(n_tok, d), w_in2, ka, kb, seq)
            o_a = flash_diff(p_ab.reshape(bsz, seq, -1), eq_a, ek_a, slopes_a2, ab_lam[i],
                             ab_diff_g[i].reshape(1, -1), lam_init, bsz, seq)
            w_uq = ab_w_uq[i].reshape(B_Q_LORA, B_HEADS, B_NOPE + B_ROPE) * scale_b
            w_uq2 = jnp.concatenate([w_uq, _rot_cols(w_uq[..., B_NOPE:])], axis=-1)
            w_uq2 = w_uq2.reshape(B_Q_LORA, B_HEADS * 2 * LANES).astype(BF16)
            q_cat = uq_proj(p_ab, 3 * A_COLS // B_Q_LORA, ab_q_norm_g[i].reshape(1, -1), w_uq2, qa, qb, seq)
            k_cat, v_b = ukv_proj(p_ab, (3 * A_COLS + B_Q_LORA) // B_KV_LORA,
                                  (3 * A_COLS + B_Q_LORA + B_KV_LORA) // LANES,
                                  ab_kv_norm_g[i].reshape(1, -1), ab_w_ukv[i].astype(BF16))
            o_b = flash_mla(q_cat.reshape(bsz, seq, -1), k_cat.reshape(bsz, seq, -1),
                            v_b.reshape(bsz, seq, -1), bsz, seq)
            o_cat = jnp.concatenate([o_a, o_b], axis=-1).reshape(n_tok, -1)
            y = matmul(o_cat, ab_w_out[i].astype(BF16), F32)
        else:
            w_in = c_w_in[i]
            nqc = C_HEADS * C_HEAD_DIM
            w_in2 = jnp.concatenate([w_in[:, :nqc] * scale_c, w_in[:, nqc:]], axis=1).astype(BF16)
            p_c = matmul(h.reshape(n_tok, d), w_in2, BF16, tn=1024)
            o_c = window_attn(p_c.reshape(bsz, seq, -1), slopes_c, c_sink[i].astype(F32), bsz, seq)
            y = matmul(o_c.reshape(n_tok, -1), c_w_out[i].astype(BF16), F32)
        moe_next = layer % 2 == 1
        x, h = postnorm(x, y.reshape(bsz, seq, d), mods, l2, ln_g[layer, 0], ln_b[layer, 0], alpha,
                        next_dtype=F32 if moe_next else BF16, next_tiles=moe_next)

        if not moe_next:
            y = ffn(h.reshape(n_tok, d), ffn_w_gate[i].astype(BF16), ffn_w_up[i].astype(BF16),
                    ffn_w_down[i].astype(BF16)).reshape(bsz, seq, d)
            route = None
        else:
            h3 = h.reshape(n_tok, d // LANES, LANES)
            w_r = jnp.zeros((d, LANES), F32).at[:, :N_EXPERTS].set(moe_w_router[i])
            b_r = jnp.zeros((1, LANES), F32).at[0, :N_EXPERTS].set(moe_b_router[i])
            route = router(h3, w_r, b_r)
            flat_e = route[:, :TOP_K].astype(jnp.int32).reshape(-1)
            dest, row_tok, block_e = _routing_tables(flat_e)
            ys = moe_ffn(block_e, row_tok, h3, moe_w_gate[i].astype(BF16), moe_w_up[i].astype(BF16),
                         moe_w_down[i].astype(BF16))
            y = gather_rows(ys, dest).reshape(bsz, seq, TOP_K, d // LANES, LANES)
            route = route.reshape(bsz, seq, LANES)
        x, h = postnorm(x, y, mods, l2 + 1, ln_g[layer, 1], ln_b[layer, 1], alpha,
                        route=route, next_dtype=None if last else BF16)
    return x
```

```python
import functools
import math

import numpy as np
import jax
import jax.numpy as jnp
from jax import lax
from jax.experimental import pallas as pl
from jax.experimental.pallas import tpu as pltpu

F32 = jnp.float32
BF16 = jnp.bfloat16

A_HEADS = 4
A_QK_DIM = 64
A_V_DIM = 128
B_HEADS = 4
B_Q_LORA = 512
B_KV_LORA = 256
B_NOPE = 128
B_ROPE = 64
B_V_DIM = 128
ROPE_THETA = 10000.0
C_HEADS = 16
C_KV_HEADS = 4
C_GROUP = C_HEADS // C_KV_HEADS
C_HEAD_DIM = 128
WINDOW = 128
N_EXPERTS = 8
TOP_K = 2
MOE_BLOCK = 512
LN_EPS = 1e-5
RMS_EPS = 1e-6

A_COLS = A_HEADS * 2 * A_QK_DIM
LANES = 128
VMEM_LIMIT = 56 * 1024 * 1024
NEG_INF = float("-inf")
LOG2E = math.log2(math.e)


def _cparams(sem, vmem=VMEM_LIMIT):
    return pltpu.CompilerParams(dimension_semantics=sem, vmem_limit_bytes=vmem)


def _adaln_kernel(c_ref, w_ref, b_ref, o_ref):
    c = c_ref[...]
    ca = c * jax.nn.sigmoid(c)
    o_ref[...] = jnp.dot(ca, w_ref[...], preferred_element_type=F32,
                         precision=lax.Precision.HIGHEST) + b_ref[...]


def adaln_all(c_pad, ada_w, ada_b):
    l2, d, n = ada_w.shape
    tn = 768
    return pl.pallas_call(
        _adaln_kernel,
        grid=(l2, n // tn),
        in_specs=[pl.BlockSpec((8, d), lambda l, j: (0, 0)),
                  pl.BlockSpec((None, d, tn), lambda l, j: (l, 0, j)),
                  pl.BlockSpec((None, 1, tn), lambda l, j: (l, 0, j))],
        out_specs=pl.BlockSpec((None, 8, tn), lambda l, j: (l, 0, j)),
        out_shape=jax.ShapeDtypeStruct((l2, 8, n), F32),
        compiler_params=_cparams(("parallel", "parallel")),
        name="adaln",
    )(c_pad, ada_w, ada_b)


def _modulate_kernel(x_ref, mod_ref, h_ref):
    x = x_ref[...]
    h_ref[...] = (x * (1.0 + mod_ref[1:2, :]) + mod_ref[0:1, :]).astype(h_ref.dtype)


def modulate(x, mods, l2, out_dtype):
    b, s, d = x.shape
    ts = 512
    return pl.pallas_call(
        _modulate_kernel,
        grid=(b, s // ts),
        in_specs=[pl.BlockSpec((None, ts, d), lambda i, j: (i, j, 0)),
                  pl.BlockSpec((None, None, 3, d), lambda i, j: (l2, i, 0, 0))],
        out_specs=pl.BlockSpec((None, ts, d), lambda i, j: (i, j, 0)),
        out_shape=jax.ShapeDtypeStruct((b, s, d), out_dtype),
        compiler_params=_cparams(("parallel", "parallel")),
        name="modulate",
    )(x, mods)


def _postnorm_kernel(*refs, alpha, routed, has_next):
    refs = list(refs)
    x_ref, y_ref, modc_ref, g_ref, b_ref = refs[:5]
    rest = refs[5:]
    if routed:
        route_ref, idx_ref, idxn_ref = rest[:3]
        rest = rest[3:]
    modn_ref = rest.pop(0) if has_next else None
    xo_ref = rest.pop(0)
    h_ref = rest.pop(0) if has_next else None
    x = x_ref[...]
    ts = x.shape[0]
    if routed:
        yg_ref, sem = rest
        step = pl.program_id(0) * pl.num_programs(1) + pl.program_id(1)
        n_steps = pl.num_programs(0) * pl.num_programs(1)
        slot = step % 2

        def issue_gather(idx, dst_slot):
            def body(r, carry):
                pltpu.make_async_copy(y_ref.at[pl.ds(idx[0, r], 1)], yg_ref.at[dst_slot, pl.ds(r, 1)],
                                      sem.at[dst_slot]).start()
                return carry
            lax.fori_loop(0, TOP_K * ts, body, 0, unroll=8)

        @pl.when(step == 0)
        def _():
            issue_gather(idx_ref, 0)

        @pl.when(step + 1 < n_steps)
        def _():
            issue_gather(idxn_ref, 1 - slot)

        pltpu.make_async_copy(y_ref.at[pl.ds(0, TOP_K * ts)], yg_ref.at[slot], sem.at[slot]).wait()
        g1 = route_ref[:, TOP_K:TOP_K + 1]
        g2 = route_ref[:, TOP_K + 1:TOP_K + 2]
        y = g1 * yg_ref[slot, 0:ts, :] + g2 * yg_ref[slot, ts:2 * ts, :]
    else:
        y = y_ref[...].astype(F32)
    z = alpha * x + (1.0 + modc_ref[2:3, :]) * y
    mu = jnp.mean(z, axis=-1, keepdims=True)
    zc = z - mu
    var = jnp.mean(zc * zc, axis=-1, keepdims=True)
    xn = zc * lax.rsqrt(var + LN_EPS) * g_ref[...] + b_ref[...]
    xo_ref[...] = xn
    if has_next:
        h_ref[...] = (xn * (1.0 + modn_ref[1:2, :]) + modn_ref[0:1, :]).astype(h_ref.dtype)


def postnorm(x, y, mods, l2, ln_g, ln_b, alpha, *, route=None, dest=None, next_dtype=None):
    b, s, d = x.shape
    ts = 256
    has_next = next_dtype is not None
    routed = route is not None
    nj = s // ts
    if routed:
        y_spec = pl.BlockSpec(memory_space=pl.ANY)
    else:
        y_spec = pl.BlockSpec((None, ts, d), lambda i, j: (i, j, 0))
    in_specs = [pl.BlockSpec((None, ts, d), lambda i, j: (i, j, 0)),
                y_spec,
                pl.BlockSpec((None, None, 3, d), lambda i, j: (l2, i, 0, 0)),
                pl.BlockSpec((1, d), lambda i, j: (0, 0)),
                pl.BlockSpec((1, d), lambda i, j: (0, 0))]
    args = [x, y, mods, ln_g.reshape(1, d), ln_b.reshape(1, d)]
    scratch = []
    if routed:
        n_steps = b * nj
        idx = dest.reshape(n_steps, ts, TOP_K).transpose(0, 2, 1).reshape(n_steps, 1, TOP_K * ts)
        in_specs += [pl.BlockSpec((None, ts, LANES), lambda i, j: (i, j, 0)),
                     pl.BlockSpec((None, 1, TOP_K * ts), lambda i, j: (i * nj + j, 0, 0),
                                  memory_space=pltpu.SMEM),
                     pl.BlockSpec((None, 1, TOP_K * ts),
                                  lambda i, j: (jnp.minimum(i * nj + j + 1, n_steps - 1), 0, 0),
                                  memory_space=pltpu.SMEM)]
        args += [route, idx, idx]
        scratch = [pltpu.VMEM((2, TOP_K * ts, d), F32), pltpu.SemaphoreType.DMA((2,))]
    out_specs = [pl.BlockSpec((None, ts, d), lambda i, j: (i, j, 0))]
    out_shape = [jax.ShapeDtypeStruct((b, s, d), F32)]
    if has_next:
        in_specs.append(pl.BlockSpec((None, None, 3, d), lambda i, j: (l2 + 1, i, 0, 0)))
        args.append(mods)
        out_specs.append(pl.BlockSpec((None, ts, d), lambda i, j: (i, j, 0)))
        out_shape.append(jax.ShapeDtypeStruct((b, s, d), next_dtype))
    sem = ("arbitrary", "arbitrary") if routed else ("parallel", "parallel")
    out = pl.pallas_call(
        functools.partial(_postnorm_kernel, alpha=alpha, routed=routed, has_next=has_next),
        grid=(b, nj),
        in_specs=in_specs,
        out_specs=out_specs,
        out_shape=out_shape,
        scratch_shapes=scratch,
        compiler_params=_cparams(sem),
        name="postnorm",
    )(*args)
    return (out[0], out[1]) if has_next else (out[0], None)


def _rms_rows(a_ref, g_ref):
    a = a_ref[...].astype(F32)
    return (a * lax.rsqrt(jnp.mean(a * a, axis=-1, keepdims=True) + RMS_EPS) * g_ref[...]).astype(BF16)


def _matmul_kernel(a_ref, w_ref, o_ref):
    o_ref[...] = jnp.dot(a_ref[...], w_ref[...], preferred_element_type=F32).astype(o_ref.dtype)


def matmul(a, w, out_dtype, tm=512, tn=None):
    m, k = a.shape
    n = w.shape[1]
    tn = n if tn is None else tn
    return pl.pallas_call(
        _matmul_kernel,
        grid=(m // tm, n // tn),
        in_specs=[pl.BlockSpec((tm, k), lambda i, j: (i, 0)),
                  pl.BlockSpec((k, tn), lambda i, j: (0, j))],
        out_specs=pl.BlockSpec((tm, tn), lambda i, j: (i, j)),
        out_shape=jax.ShapeDtypeStruct((m, n), out_dtype),
        compiler_params=_cparams(("parallel", "parallel")),
        name="matmul",
    )(a, w)


def _matmul2_kernel(a1_ref, a2_ref, w1_ref, w2_ref, o_ref):
    acc = jnp.dot(a1_ref[...], w1_ref[...], preferred_element_type=F32)
    acc += jnp.dot(a2_ref[...], w2_ref[...], preferred_element_type=F32)
    o_ref[...] = acc.astype(o_ref.dtype)


def matmul2(a1, a2, w, out_dtype, tm=512):
    m, k1 = a1.shape
    k2 = a2.shape[1]
    assert k1 == k2
    n = w.shape[1]
    return pl.pallas_call(
        _matmul2_kernel,
        grid=(m // tm,),
        in_specs=[pl.BlockSpec((tm, k1), lambda i: (i, 0)),
                  pl.BlockSpec((tm, k2), lambda i: (i, 0)),
                  pl.BlockSpec((k1, n), lambda i: (0, 0)),
                  pl.BlockSpec((k2, n), lambda i: (1, 0))],
        out_specs=pl.BlockSpec((tm, n), lambda i: (i, 0)),
        out_shape=jax.ShapeDtypeStruct((m, n), out_dtype),
        compiler_params=_cparams(("parallel",)),
        name="matmul2",
    )(a1, a2, w, w)


def _inproj_ab_kernel(a_ref, w_ref, ra_ref, rb_ref, o_ref):
    acc = jnp.dot(a_ref[...], w_ref[...], preferred_element_type=F32)
    n = acc.shape[1]
    o_ref[:, :n - LANES] = acc[:, :n - LANES].astype(o_ref.dtype)
    last = acc[:, n - LANES:]
    roped = last * ra_ref[...] + pltpu.roll(last, LANES // 2, 1) * rb_ref[...]
    o_ref[:, n - LANES:] = roped.astype(o_ref.dtype)


def inproj_ab(h, w, rope_a, rope_b, seq):
    m, k = h.shape
    n = w.shape[1]
    tm = 512
    nsb = seq // tm
    return pl.pallas_call(
        _inproj_ab_kernel,
        grid=(m // tm,),
        in_specs=[pl.BlockSpec((tm, k), lambda i: (i, 0)),
                  pl.BlockSpec((k, n), lambda i: (0, 0)),
                  pl.BlockSpec((tm, LANES), lambda i: (i % nsb, 0)),
                  pl.BlockSpec((tm, LANES), lambda i: (i % nsb, 0))],
        out_specs=pl.BlockSpec((tm, n), lambda i: (i, 0)),
        out_shape=jax.ShapeDtypeStruct((m, n), BF16),
        compiler_params=_cparams(("parallel",)),
        name="inproj_ab",
    )(h, w, rope_a, rope_b)


def _uq_kernel(a_ref, g_ref, w_ref, ra_ref, rb_ref, o_ref):
    an = _rms_rows(a_ref, g_ref)
    acc = jnp.dot(an, w_ref[...], preferred_element_type=F32)
    n = acc.shape[1]
    o_ref[...] = (acc * ra_ref[...] + pltpu.roll(acc, n - B_ROPE, 1) * rb_ref[...]).astype(o_ref.dtype)


def uq_proj(p_ab, col_block, g, w, rope_a, rope_b, seq):
    m = p_ab.shape[0]
    k, n = w.shape
    tm = 512
    nsb = seq // tm
    return pl.pallas_call(
        _uq_kernel,
        grid=(m // tm,),
        in_specs=[pl.BlockSpec((tm, k), lambda i: (i, col_block)),
                  pl.BlockSpec((1, k), lambda i: (0, 0)),
                  pl.BlockSpec((k, n), lambda i: (0, 0)),
                  pl.BlockSpec((tm, n), lambda i: (i % nsb, 0)),
                  pl.BlockSpec((tm, n), lambda i: (i % nsb, 0))],
        out_specs=pl.BlockSpec((tm, n), lambda i: (i, 0)),
        out_shape=jax.ShapeDtypeStruct((m, n), BF16),
        compiler_params=_cparams(("parallel",)),
        name="uq_proj",
    )(p_ab, g, w, rope_a, rope_b)


def _ukv_kernel(a_ref, g_ref, w_ref, kr_ref, k_ref, v_ref):
    an = _rms_rows(a_ref, g_ref)
    acc = jnp.dot(an, w_ref[...], preferred_element_type=F32)
    kr = kr_ref[...]
    for h in range(B_HEADS):
        base = h * (B_NOPE + B_V_DIM)
        k_ref[:, 2 * h * LANES:(2 * h + 1) * LANES] = acc[:, base:base + B_NOPE].astype(k_ref.dtype)
        k_ref[:, (2 * h + 1) * LANES:(2 * h + 2) * LANES] = kr
        v_ref[:, h * B_V_DIM:(h + 1) * B_V_DIM] = acc[:, base + B_NOPE:base + B_NOPE + B_V_DIM].astype(v_ref.dtype)


def ukv_proj(p_ab, ckv_block, kr_block, g, w):
    m = p_ab.shape[0]
    k, n = w.shape
    tm = 512
    return pl.pallas_call(
        _ukv_kernel,
        grid=(m // tm,),
        in_specs=[pl.BlockSpec((tm, k), lambda i: (i, ckv_block)),
                  pl.BlockSpec((1, k), lambda i: (0, 0)),
                  pl.BlockSpec((k, n), lambda i: (0, 0)),
                  pl.BlockSpec((tm, LANES), lambda i: (i, kr_block))],
        out_specs=[pl.BlockSpec((tm, B_HEADS * 2 * LANES), lambda i: (i, 0)),
                   pl.BlockSpec((tm, B_HEADS * B_V_DIM), lambda i: (i, 0))],
        out_shape=[jax.ShapeDtypeStruct((m, B_HEADS * 2 * LANES), BF16),
                   jax.ShapeDtypeStruct((m, B_HEADS * B_V_DIM), BF16)],
        compiler_params=_cparams(("parallel",)),
        name="ukv_proj",
    )(p_ab, g, w, p_ab)


_NT = (((1,), (1,)), ((), ()))


def _ones_block(rows):
    return jnp.ones((rows, LANES), BF16)


def _online_softmax_step(s, v_aug, m_ref, acc_ref):
    tk = s.shape[1]
    m_prev = m_ref[...]
    m_new = jnp.maximum(m_prev, jnp.max(s, axis=1, keepdims=True))
    alpha = jnp.exp2(m_prev - m_new)
    p = jnp.exp2(s - jnp.concatenate([m_new] * (tk // LANES), axis=1))
    pv = jnp.dot(p.astype(v_aug.dtype), v_aug, preferred_element_type=F32)
    acc_ref[...] = jnp.concatenate([alpha, alpha], axis=1) * acc_ref[...] + pv
    m_ref[...] = m_new


def _softmax_result(acc_ref):
    acc = acc_ref[...]
    return acc[:, :LANES] / acc[:, LANES:]


def _flash_diff_kernel(slopes_ref, q_ref, eq_ref, k_ref, ek_ref, v_ref, lam_ref, g_ref, o_ref,
                       qaug_sc, m_sc, acc_sc, *, tk, lam_init):
    tq = q_ref.shape[0]
    seq = k_ref.shape[0]
    qi = pl.program_id(2)
    q0 = qi * tq
    q = q_ref[...]
    eq = eq_ref[...]
    lane = lax.broadcasted_iota(jnp.int32, q.shape, 1)
    zero = jnp.zeros_like(q)
    for mp in range(2):
        qm = jnp.where((lane < A_QK_DIM) == (mp == 0), q, zero)
        qaug_sc[0, mp] = jnp.concatenate([qm, -eq], axis=1)
        qaug_sc[1, mp] = jnp.concatenate([qm, eq], axis=1)
    m_sc[...] = jnp.full(m_sc.shape, NEG_INF, F32)
    acc_sc[...] = jnp.zeros(acc_sc.shape, F32)
    ones_col = _ones_block(tk)

    def chunk(c, side, diagonal):
        k0 = pl.multiple_of(c * tk, tk)
        kc = jnp.concatenate([k_ref[pl.ds(k0, tk), :], ek_ref[pl.ds(k0, tk), :]], axis=1)
        v_aug = jnp.concatenate([v_ref[pl.ds(k0, tk), :], ones_col], axis=1)
        if diagonal:
            rel = (lax.broadcasted_iota(jnp.int32, (tq, tk), 0)
                   - lax.broadcasted_iota(jnp.int32, (tq, tk), 1)) + (q0 - k0)
            corr = (2.0 * slopes_ref[pl.program_id(1)]) * jnp.maximum(rel, 0).astype(F32)
        for mp in range(2):
            s = lax.dot_general(qaug_sc[side, mp], kc, _NT, preferred_element_type=F32)
            if diagonal:
                s = s - corr
            _online_softmax_step(s, v_aug, m_sc.at[mp], acc_sc.at[mp])

    def left(c, carry):
        chunk(c, 0, False)
        return carry

    def right(c, carry):
        chunk(c, 1, False)
        return carry

    cd = q0 // tk
    lax.fori_loop(0, cd, left, 0)
    chunk(cd, 1, True)
    lax.fori_loop(cd + 1, seq // tk, right, 0)

    o0 = _softmax_result(acc_sc.at[0])
    o1 = _softmax_result(acc_sc.at[1])
    lam = lam_ref[...]
    lam_full = (jnp.exp(jnp.sum(lam[0:1] * lam[1:2], axis=1, keepdims=True))
                - jnp.exp(jnp.sum(lam[2:3] * lam[3:4], axis=1, keepdims=True)) + lam_init)
    o = o0 - lam_full * o1
    on = o * lax.rsqrt(jnp.mean(o * o, axis=-1, keepdims=True) + RMS_EPS) * g_ref[...]
    o_ref[...] = (on * (1.0 - lam_init)).astype(o_ref.dtype)


def flash_diff(p_ab, eq, ek, slopes2, lam, diff_g, lam_init, bsz, seq):
    tq, tk = 1024, 1024
    assert tk % tq == 0
    nh = A_HEADS
    return pl.pallas_call(
        functools.partial(_flash_diff_kernel, tk=tk, lam_init=lam_init),
        grid=(bsz, nh, seq // tq),
        in_specs=[pl.BlockSpec(memory_space=pltpu.SMEM),
                  pl.BlockSpec((None, tq, LANES), lambda b, h, i: (b, i, h)),
                  pl.BlockSpec((None, tq, LANES), lambda b, h, i: (h, i, 0)),
                  pl.BlockSpec((None, seq, LANES), lambda b, h, i: (b, 0, nh + h)),
                  pl.BlockSpec((None, seq, LANES), lambda b, h, i: (h, 0, 0)),
                  pl.BlockSpec((None, seq, LANES), lambda b, h, i: (b, 0, 2 * nh + h)),
                  pl.BlockSpec((4, A_QK_DIM), lambda b, h, i: (0, 0)),
                  pl.BlockSpec((1, A_V_DIM), lambda b, h, i: (0, 0))],
        out_specs=pl.BlockSpec((None, tq, A_V_DIM), lambda b, h, i: (b, i, h)),
        out_shape=jax.ShapeDtypeStruct((bsz, seq, nh * A_V_DIM), BF16),
        scratch_shapes=[pltpu.VMEM((2, 2, tq, 2 * LANES), BF16),
                        pltpu.VMEM((2, tq, LANES), F32),
                        pltpu.VMEM((2, tq, 2 * LANES), F32)],
        compiler_params=_cparams(("parallel", "parallel", "parallel")),
        name="flash_diff",
    )(slopes2, p_ab, eq, p_ab, ek, p_ab, lam, diff_g)


def _flash_mla_kernel(q_ref, k_ref, v_ref, o_ref, m_sc, acc_sc, *, tk):
    seq = k_ref.shape[0]
    q = q_ref[...]
    m_sc[...] = jnp.full(m_sc.shape, NEG_INF, F32)
    acc_sc[...] = jnp.zeros(acc_sc.shape, F32)
    ones_col = _ones_block(tk)

    def body(c, carry):
        k0 = pl.multiple_of(c * tk, tk)
        s = lax.dot_general(q, k_ref[pl.ds(k0, tk), :], _NT, preferred_element_type=F32)
        v_aug = jnp.concatenate([v_ref[pl.ds(k0, tk), :], ones_col], axis=1)
        _online_softmax_step(s, v_aug, m_sc, acc_sc)
        return carry

    lax.fori_loop(0, seq // tk, body, 0)
    o_ref[...] = _softmax_result(acc_sc).astype(o_ref.dtype)


def flash_mla(q_cat, k_cat, v_b, bsz, seq):
    tq, tk = 1024, 1024
    kd = 2 * LANES
    return pl.pallas_call(
        functools.partial(_flash_mla_kernel, tk=tk),
        grid=(bsz, B_HEADS, seq // tq),
        in_specs=[pl.BlockSpec((None, tq, kd), lambda b, h, i: (b, i, h)),
                  pl.BlockSpec((None, seq, kd), lambda b, h, i: (b, 0, h)),
                  pl.BlockSpec((None, seq, B_V_DIM), lambda b, h, i: (b, 0, h))],
        out_specs=pl.BlockSpec((None, tq, B_V_DIM), lambda b, h, i: (b, i, h)),
        out_shape=jax.ShapeDtypeStruct((bsz, seq, B_HEADS * B_V_DIM), BF16),
        scratch_shapes=[pltpu.VMEM((tq, LANES), F32), pltpu.VMEM((tq, 2 * LANES), F32)],
        compiler_params=_cparams(("parallel", "parallel", "parallel")),
        name="flash_mla",
    )(q_cat, k_cat, v_b)


def _window_kernel(slopes_ref, sink_ref, q_ref, k_ref, v_ref, o_ref):
    tq = q_ref.shape[0]
    seq = k_ref.shape[0]
    blk = WINDOW
    kw = 3 * blk
    n = pl.program_id(1)
    q0 = pl.program_id(2) * tq
    rel = (lax.broadcasted_iota(jnp.int32, (blk, kw), 1) - lax.broadcasted_iota(jnp.int32, (blk, kw), 0))
    ones_blk = _ones_block(kw)
    heads = [n * C_GROUP + g for g in range(C_GROUP)]
    slope_rows = jnp.concatenate([jnp.full((blk, kw), slopes_ref[hd], F32) for hd in heads], axis=0)
    sink_rows = jnp.concatenate([jnp.full((blk, LANES), sink_ref[hd], F32) for hd in heads], axis=0)
    for sb in range(tq // blk):
        t0 = q0 + sb * blk
        start = pl.multiple_of(jnp.clip(t0 - blk, 0, seq - kw), blk)
        kc = k_ref[pl.ds(start, kw), :]
        v_aug = jnp.concatenate([v_ref[pl.ds(start, kw), :], ones_blk], axis=1)
        dist = jnp.abs(rel + (start - t0))
        dist_rows = jnp.concatenate([dist] * C_GROUP, axis=0)
        qs = jnp.concatenate([q_ref[sb * blk:(sb + 1) * blk, g * C_HEAD_DIM:(g + 1) * C_HEAD_DIM]
                              for g in range(C_GROUP)], axis=0)
        s = lax.dot_general(qs, kc, _NT, preferred_element_type=F32) - slope_rows * dist_rows.astype(F32)
        s = jnp.where(dist_rows <= WINDOW, s, NEG_INF)
        m = jnp.maximum(jnp.max(s, axis=1, keepdims=True), sink_rows)
        e = jnp.exp(s - jnp.concatenate([m] * (kw // LANES), axis=1))
        pv = jnp.dot(e.astype(v_aug.dtype), v_aug, preferred_element_type=F32)
        o = pv[:, :C_HEAD_DIM] / (pv[:, C_HEAD_DIM:] + jnp.exp(sink_rows - m))
        for g in range(C_GROUP):
            o_ref[sb * blk:(sb + 1) * blk, g * C_HEAD_DIM:(g + 1) * C_HEAD_DIM] = (
                o[g * blk:(g + 1) * blk].astype(o_ref.dtype))


def window_attn(p_c, slopes, sink, bsz, seq):
    tq = 512
    gw = C_GROUP * C_HEAD_DIM
    nq = C_HEADS * C_HEAD_DIM // LANES
    return pl.pallas_call(
        _window_kernel,
        grid=(bsz, C_KV_HEADS, seq // tq),
        in_specs=[pl.BlockSpec(memory_space=pltpu.SMEM),
                  pl.BlockSpec(memory_space=pltpu.SMEM),
                  pl.BlockSpec((None, tq, gw), lambda b, n, i: (b, i, n)),
                  pl.BlockSpec((None, seq, C_HEAD_DIM), lambda b, n, i: (b, 0, nq + n)),
                  pl.BlockSpec((None, seq, C_HEAD_DIM), lambda b, n, i: (b, 0, nq + C_KV_HEADS + n))],
        out_specs=pl.BlockSpec((None, tq, gw), lambda b, n, i: (b, i, n)),
        out_shape=jax.ShapeDtypeStruct((bsz, seq, C_HEADS * C_HEAD_DIM), BF16),
        compiler_params=_cparams(("parallel", "parallel", "parallel")),
        name="window_attn",
    )(slopes, sink, p_c, p_c, p_c)


def _swiglu_step(x, wg_ref, wu_ref, wd_ref, acc_ref):
    g = jnp.dot(x, wg_ref[...], preferred_element_type=F32)
    u = jnp.dot(x, wu_ref[...], preferred_element_type=F32)
    a = (g * jax.nn.sigmoid(g) * u).astype(BF16)
    acc_ref[...] += jnp.dot(a, wd_ref[...], preferred_element_type=F32)


def _ffn_kernel(x_ref, wg_ref, wu_ref, wd_ref, o_ref, acc_ref):
    j = pl.program_id(1)

    @pl.when(j == 0)
    def _():
        acc_ref[...] = jnp.zeros(acc_ref.shape, F32)

    _swiglu_step(x_ref[...], wg_ref, wu_ref, wd_ref, acc_ref)

    @pl.when(j == pl.num_programs(1) - 1)
    def _():
        o_ref[...] = acc_ref[...].astype(o_ref.dtype)


def ffn(h, wg, wu, wd):
    m, d = h.shape
    f = wg.shape[1]
    tm, tf = 512, 512
    return pl.pallas_call(
        _ffn_kernel,
        grid=(m // tm, f // tf),
        in_specs=[pl.BlockSpec((tm, d), lambda i, j: (i, 0)),
                  pl.BlockSpec((d, tf), lambda i, j: (0, j)),
                  pl.BlockSpec((d, tf), lambda i, j: (0, j)),
                  pl.BlockSpec((tf, d), lambda i, j: (j, 0))],
        out_specs=pl.BlockSpec((tm, d), lambda i, j: (i, 0)),
        out_shape=jax.ShapeDtypeStruct((m, d), F32),
        scratch_shapes=[pltpu.VMEM((tm, d), F32)],
        compiler_params=_cparams(("parallel", "arbitrary")),
        name="ffn",
    )(h, wg, wu, wd)


def _moe_ffn_kernel(be_ref, tok_ref, tokn_ref, h_ref, wg_ref, wu_ref, wd_ref, o_ref,
                    xg_ref, xb_ref, acc_ref, sem):
    del be_ref
    i = pl.program_id(0)
    j = pl.program_id(1)
    rows = xg_ref.shape[0]

    def issue_gather(idx_ref):
        def body(r, carry):
            pltpu.make_async_copy(h_ref.at[pl.ds(idx_ref[0, r], 1)], xg_ref.at[pl.ds(r, 1)], sem).start()
            return carry
        lax.fori_loop(0, rows, body, 0, unroll=8)

    @pl.when(j == 0)
    def _():
        @pl.when(i == 0)
        def _():
            issue_gather(tok_ref)
        pltpu.make_async_copy(h_ref.at[pl.ds(0, rows)], xg_ref, sem).wait()
        xb_ref[...] = xg_ref[...].astype(BF16)
        acc_ref[...] = jnp.zeros(acc_ref.shape, F32)

    @pl.when((j == 1) & (i + 1 < pl.num_programs(0)))
    def _():
        issue_gather(tokn_ref)

    _swiglu_step(xb_ref[...], wg_ref, wu_ref, wd_ref, acc_ref)

    @pl.when(j == pl.num_programs(1) - 1)
    def _():
        o_ref[...] = acc_ref[...]


def moe_ffn(block_e, row_tok, h, wg, wu, wd):
    d = h.shape[1]
    r = row_tok.shape[0]
    f = wg.shape[2]
    tm, tf = MOE_BLOCK, 256
    assert f // tf >= 2
    nblk = r // tm
    tok3 = row_tok.reshape(nblk, 1, tm)
    grid_spec = pltpu.PrefetchScalarGridSpec(
        num_scalar_prefetch=1,
        grid=(nblk, f // tf),
        in_specs=[pl.BlockSpec((None, 1, tm), lambda i, j, be: (i, 0, 0), memory_space=pltpu.SMEM),
                  pl.BlockSpec((None, 1, tm), lambda i, j, be: (jnp.minimum(i + 1, nblk - 1), 0, 0),
                               memory_space=pltpu.SMEM),
                  pl.BlockSpec(memory_space=pl.ANY),
                  pl.BlockSpec((None, d, tf), lambda i, j, be: (be[i], 0, j)),
                  pl.BlockSpec((None, d, tf), lambda i, j, be: (be[i], 0, j)),
                  pl.BlockSpec((None, tf, d), lambda i, j, be: (be[i], j, 0))],
        out_specs=pl.BlockSpec((tm, d), lambda i, j, be: (i, 0)),
        scratch_shapes=[pltpu.VMEM((tm, d), F32), pltpu.VMEM((tm, d), BF16),
                        pltpu.VMEM((tm, d), F32), pltpu.SemaphoreType.DMA(())],
    )
    return pl.pallas_call(
        _moe_ffn_kernel,
        grid_spec=grid_spec,
        out_shape=jax.ShapeDtypeStruct((r, d), F32),
        compiler_params=_cparams(("arbitrary", "arbitrary")),
        name="moe_ffn",
    )(block_e, tok3, tok3, h, wg, wu, wd)


def _router_kernel(h_ref, w_ref, b_ref, o_ref):
    logits = jnp.dot(h_ref[...], w_ref[...], preferred_element_type=F32,
                     precision=lax.Precision.HIGHEST) + b_ref[...]
    lane = lax.broadcasted_iota(jnp.int32, logits.shape, 1)
    logits = jnp.where(lane < N_EXPERTS, logits, NEG_INF)
    m1 = jnp.max(logits, axis=1, keepdims=True)
    i1 = jnp.min(jnp.where(logits == m1, lane, LANES), axis=1, keepdims=True)
    rest = jnp.where(lane == i1, NEG_INF, logits)
    m2 = jnp.max(rest, axis=1, keepdims=True)
    i2 = jnp.min(jnp.where(rest == m2, lane, LANES), axis=1, keepdims=True)
    e2 = jnp.exp(m2 - m1)
    g1 = 1.0 / (1.0 + e2)
    g2 = e2 / (1.0 + e2)
    out = jnp.where(lane == 0, i1.astype(F32),
                    jnp.where(lane == 1, i2.astype(F32),
                              jnp.where(lane == 2, g1, jnp.where(lane == 3, g2, 0.0))))
    o_ref[...] = out


def router(h, w_pad, b_pad):
    m, d = h.shape
    tm = 512
    return pl.pallas_call(
        _router_kernel,
        grid=(m // tm,),
        in_specs=[pl.BlockSpec((tm, d), lambda i: (i, 0)),
                  pl.BlockSpec((d, LANES), lambda i: (0, 0)),
                  pl.BlockSpec((1, LANES), lambda i: (0, 0))],
        out_specs=pl.BlockSpec((tm, LANES), lambda i: (i, 0)),
        out_shape=jax.ShapeDtypeStruct((m, LANES), F32),
        compiler_params=_cparams(("parallel",)),
        name="router",
    )(h, w_pad, b_pad)


def _alibi_slopes(n_heads):
    return jnp.asarray(2.0 ** (-8.0 * np.arange(1, n_heads + 1) / n_heads), dtype=F32)


def _bf16_parts(v):
    a = v.astype(BF16).astype(F32)
    b = (v - a).astype(BF16).astype(F32)
    c = (v - a - b).astype(BF16).astype(F32)
    return a, b, c


def _alibi_tables(seq, slopes2):
    nh = slopes2.shape[0]
    pos = jnp.arange(seq, dtype=jnp.int32)
    pos_parts = (((pos // LANES) * LANES).astype(F32), (pos % LANES).astype(F32))
    cols_q, cols_k = [], []
    for sp in _bf16_parts(slopes2):
        sp_col = jnp.broadcast_to(sp[:, None], (nh, seq))
        for pp in pos_parts:
            pp_row = jnp.broadcast_to(pp[None, :], (nh, seq))
            cols_q += [pp_row, sp_col]
            cols_k += [sp_col, -pp_row]
    pad = jnp.zeros((nh, seq, LANES - len(cols_q)), F32)
    eq = jnp.concatenate([jnp.stack(cols_q, axis=-1), pad], axis=-1).astype(BF16)
    ek = jnp.concatenate([jnp.stack(cols_k, axis=-1), pad], axis=-1).astype(BF16)
    return eq, ek


def _rope_tables(seq):
    inv_freq = ROPE_THETA ** (-jnp.arange(0, B_ROPE, 2, dtype=F32) / B_ROPE)
    ang = jnp.arange(seq, dtype=F32)[:, None] * inv_freq[None, :]
    cos, sin = jnp.cos(ang), jnp.sin(ang)
    cos2 = jnp.concatenate([cos, cos], axis=-1)
    sin2 = jnp.concatenate([sin, sin], axis=-1)
    z64 = jnp.zeros((seq, B_ROPE), F32)
    ka = jnp.concatenate([cos2, z64], axis=-1)
    kb = jnp.concatenate([sin2, z64], axis=-1)
    ones = jnp.ones((seq, B_NOPE), F32)
    z128 = jnp.zeros((seq, B_NOPE), F32)
    qa = jnp.tile(jnp.concatenate([ones, cos2, z64], axis=-1), (1, B_HEADS))
    qb = jnp.tile(jnp.concatenate([z128, sin2, z64], axis=-1), (1, B_HEADS))
    return ka, kb, qa, qb


def _rot_cols(w):
    half = w.shape[-1] // 2
    return jnp.concatenate([-w[..., half:], w[..., :half]], axis=-1)


def _routing_tables(flat_e):
    n_assign = flat_e.shape[0]
    onehot = (flat_e[:, None] == jnp.arange(N_EXPERTS, dtype=jnp.int32)[None, :]).astype(jnp.int32)
    csum = jnp.cumsum(onehot, axis=0)
    rank = jnp.sum((csum - onehot) * onehot, axis=1)
    sizes = csum[-1]
    padded = ((sizes + MOE_BLOCK - 1) // MOE_BLOCK) * MOE_BLOCK
    pad_end = jnp.cumsum(padded)
    pad_start = pad_end - padded
    dest = (pad_start[flat_e] + rank).astype(jnp.int32)
    n_blocks = -(-n_assign // MOE_BLOCK) + N_EXPERTS
    n_rows = n_blocks * MOE_BLOCK
    tok = jnp.arange(n_assign, dtype=jnp.int32) // TOP_K
    row_tok = jnp.zeros((n_rows,), jnp.int32).at[dest].set(tok)
    block_start = jnp.arange(n_blocks, dtype=jnp.int32) * MOE_BLOCK
    block_e = jnp.minimum(jnp.searchsorted(pad_end, block_start, side='right'), N_EXPERTS - 1).astype(jnp.int32)
    return dest, row_tok, block_e


def kernel(x, c, ada_w, ada_b, ln_g, ln_b, ab_w_in, ab_lam, ab_diff_g, ab_q_norm_g, ab_kv_norm_g, ab_w_uq,
           ab_w_ukv, ab_w_out, c_w_in, c_sink, c_w_out, ffn_w_gate, ffn_w_up, ffn_w_down, moe_w_router,
           moe_b_router, moe_w_gate, moe_w_up, moe_w_down):
    bsz, seq, d = x.shape
    depth = ada_w.shape[0]
    n_tok = bsz * seq
    alpha = (2 * depth) ** 0.25

    c_pad = jnp.zeros((8, d), F32).at[:bsz].set(c)
    mods = adaln_all(c_pad, ada_w.reshape(2 * depth, d, 3 * d), ada_b.reshape(2 * depth, 1, 3 * d))
    mods = mods.reshape(2 * depth, 8, 3, d)

    ka, kb, qa, qb = _rope_tables(seq)
    slopes_a2 = _alibi_slopes(A_HEADS) * LOG2E
    eq_a, ek_a = _alibi_tables(seq, slopes_a2)
    slopes_c = _alibi_slopes(C_HEADS)
    scale_a = A_QK_DIM ** -0.5 * LOG2E
    scale_b = (B_NOPE + B_ROPE) ** -0.5 * LOG2E
    scale_c = C_HEAD_DIM ** -0.5

    h = modulate(x, mods, 0, BF16)
    for layer in range(depth):
        i = layer // 2
        l2 = 2 * layer
        last = layer == depth - 1
        if layer % 2 == 0:
            lam_init = 0.8 - 0.6 * math.exp(-0.3 * layer)
            w_in = ab_w_in[i]
            w_kr = w_in[:, 3 * A_COLS + B_Q_LORA + B_KV_LORA:]
            w_in2 = jnp.concatenate([w_in[:, :A_COLS] * scale_a, w_in[:, A_COLS:], _rot_cols(w_kr)],
                                    axis=1).astype(BF16)
            p_ab = inproj_ab(h.reshape(n_tok, d), w_in2, ka, kb, seq)
            o_a = flash_diff(p_ab.reshape(bsz, seq, -1), eq_a, ek_a, slopes_a2, ab_lam[i],
                             ab_diff_g[i].reshape(1, -1), lam_init, bsz, seq)
            w_uq = ab_w_uq[i].reshape(B_Q_LORA, B_HEADS, B_NOPE + B_ROPE) * scale_b
            w_uq2 = jnp.concatenate([w_uq, _rot_cols(w_uq[..., B_NOPE:])], axis=-1)
            w_uq2 = w_uq2.reshape(B_Q_LORA, B_HEADS * 2 * LANES).astype(BF16)
            q_cat = uq_proj(p_ab, 3 * A_COLS // B_Q_LORA, ab_q_norm_g[i].reshape(1, -1), w_uq2, qa, qb, seq)
            k_cat, v_b = ukv_proj(p_ab, (3 * A_COLS + B_Q_LORA) // B_KV_LORA,
                                  (3 * A_COLS + B_Q_LORA + B_KV_LORA) // LANES,
                                  ab_kv_norm_g[i].reshape(1, -1), ab_w_ukv[i].astype(BF16))
            o_b = flash_mla(q_cat.reshape(bsz, seq, -1), k_cat.reshape(bsz, seq, -1),
                            v_b.reshape(bsz, seq, -1), bsz, seq)
            y = matmul2(o_a.reshape(n_tok, -1), o_b.reshape(n_tok, -1), ab_w_out[i].astype(BF16), F32)
        else:
            w_in = c_w_in[i]
            nqc = C_HEADS * C_HEAD_DIM
            w_in2 = jnp.concatenate([w_in[:, :nqc] * scale_c, w_in[:, nqc:]], axis=1).astype(BF16)
            p_c = matmul(h.reshape(n_tok, d), w_in2, BF16, tn=1024)
            o_c = window_attn(p_c.reshape(bsz, seq, -1), slopes_c, c_sink[i].astype(F32), bsz, seq)
            y = matmul(o_c.reshape(n_tok, -1), c_w_out[i].astype(BF16), F32)
        moe_next = layer % 2 == 1
        x, h = postnorm(x, y.reshape(bsz, seq, d), mods, l2, ln_g[layer, 0], ln_b[layer, 0], alpha,
                        next_dtype=F32 if moe_next else BF16)

        if not moe_next:
            y = ffn(h.reshape(n_tok, d), ffn_w_gate[i].astype(BF16), ffn_w_up[i].astype(BF16),
                    ffn_w_down[i].astype(BF16)).reshape(bsz, seq, d)
            route = dest = None
        else:
            hf = h.reshape(n_tok, d)
            w_r = jnp.zeros((d, LANES), F32).at[:, :N_EXPERTS].set(moe_w_router[i])
            b_r = jnp.zeros((1, LANES), F32).at[0, :N_EXPERTS].set(moe_b_router[i])
            route = router(hf, w_r, b_r)
            flat_e = route[:, :TOP_K].astype(jnp.int32).reshape(-1)
            dest, row_tok, block_e = _routing_tables(flat_e)
            y = moe_ffn(block_e, row_tok, hf, moe_w_gate[i].astype(BF16), moe_w_up[i].astype(BF16),
                        moe_w_down[i].astype(BF16))
            route = route.reshape(bsz, seq, LANES)
        x, h = postnorm(x, y, mods, l2 + 1, ln_g[layer, 1], ln_b[layer, 1], alpha,
                        route=route, dest=dest, next_dtype=None if last else BF16)
    return x
```

```python
import functools
import math

import numpy as np
import jax
import jax.numpy as jnp
from jax import lax
from jax.experimental import pallas as pl
from jax.experimental.pallas import tpu as pltpu

F32 = jnp.float32
BF16 = jnp.bfloat16

A_HEADS = 4
A_QK_DIM = 64
A_V_DIM = 128
B_HEADS = 4
B_Q_LORA = 512
B_KV_LORA = 256
B_NOPE = 128
B_ROPE = 64
B_V_DIM = 128
ROPE_THETA = 10000.0
C_HEADS = 16
C_KV_HEADS = 4
C_GROUP = C_HEADS // C_KV_HEADS
C_HEAD_DIM = 128
WINDOW = 128
N_EXPERTS = 8
TOP_K = 2
MOE_ROWS = 1024
LN_EPS = 1e-5
RMS_EPS = 1e-6

A_COLS = A_HEADS * 2 * A_QK_DIM
LANES = 128
VMEM_LIMIT = 56 * 1024 * 1024
NEG_INF = float("-inf")
LOG2E = math.log2(math.e)


def _cparams(sem, vmem=VMEM_LIMIT):
    return pltpu.CompilerParams(dimension_semantics=sem, vmem_limit_bytes=vmem)


def _adaln_kernel(c_ref, w_ref, b_ref, o_ref):
    c = c_ref[...]
    ca = c * jax.nn.sigmoid(c)
    o_ref[...] = jnp.dot(ca, w_ref[...], preferred_element_type=F32,
                         precision=lax.Precision.HIGHEST) + b_ref[...]


def adaln_all(c_pad, ada_w, ada_b):
    l2, d, n = ada_w.shape
    tn = 768
    return pl.pallas_call(
        _adaln_kernel,
        grid=(l2, n // tn),
        in_specs=[pl.BlockSpec((8, d), lambda l, j: (0, 0)),
                  pl.BlockSpec((None, d, tn), lambda l, j: (l, 0, j)),
                  pl.BlockSpec((None, 1, tn), lambda l, j: (l, 0, j))],
        out_specs=pl.BlockSpec((None, 8, tn), lambda l, j: (l, 0, j)),
        out_shape=jax.ShapeDtypeStruct((l2, 8, n), F32),
        compiler_params=_cparams(("parallel", "parallel")),
        name="adaln",
    )(c_pad, ada_w, ada_b)


def _modulate_kernel(x_ref, mod_ref, h_ref):
    x = x_ref[...]
    h_ref[...] = (x * (1.0 + mod_ref[1:2, :]) + mod_ref[0:1, :]).astype(h_ref.dtype)


def modulate(x, mods, l2, out_dtype):
    b, s, d = x.shape
    ts = 512
    return pl.pallas_call(
        _modulate_kernel,
        grid=(b, s // ts),
        in_specs=[pl.BlockSpec((None, ts, d), lambda i, j: (i, j, 0)),
                  pl.BlockSpec((None, None, 3, d), lambda i, j: (l2, i, 0, 0))],
        out_specs=pl.BlockSpec((None, ts, d), lambda i, j: (i, j, 0)),
        out_shape=jax.ShapeDtypeStruct((b, s, d), out_dtype),
        compiler_params=_cparams(("parallel", "parallel")),
        name="modulate",
    )(x, mods)


def _postnorm_kernel(*refs, alpha, routed, has_next):
    refs = list(refs)
    x_ref, y_ref, modc_ref, g_ref, b_ref = refs[:5]
    rest = refs[5:]
    if routed:
        route_ref, idx_ref, idxn_ref = rest[:3]
        rest = rest[3:]
    modn_ref = rest.pop(0) if has_next else None
    xo_ref = rest.pop(0)
    h_ref = rest.pop(0) if has_next else None
    x = x_ref[...]
    ts = x.shape[0]
    if routed:
        yg_ref, sem = rest
        step = pl.program_id(0) * pl.num_programs(1) + pl.program_id(1)
        n_steps = pl.num_programs(0) * pl.num_programs(1)
        slot = step % 2

        def issue_gather(idx, dst_slot):
            def body(r, carry):
                pltpu.make_async_copy(y_ref.at[pl.ds(idx[0, r], 1)], yg_ref.at[dst_slot, pl.ds(r, 1)],
                                      sem.at[dst_slot]).start()
                return carry
            lax.fori_loop(0, TOP_K * ts, body, 0, unroll=8)

        @pl.when(step == 0)
        def _():
            issue_gather(idx_ref, 0)

        @pl.when(step + 1 < n_steps)
        def _():
            issue_gather(idxn_ref, 1 - slot)

        pltpu.make_async_copy(y_ref.at[pl.ds(0, TOP_K * ts)], yg_ref.at[slot], sem.at[slot]).wait()
        g1 = route_ref[:, TOP_K:TOP_K + 1]
        g2 = route_ref[:, TOP_K + 1:TOP_K + 2]
        y = g1 * yg_ref[slot, 0:ts, :] + g2 * yg_ref[slot, ts:2 * ts, :]
    else:
        y = y_ref[...].astype(F32)
    z = alpha * x + (1.0 + modc_ref[2:3, :]) * y
    mu = jnp.mean(z, axis=-1, keepdims=True)
    zc = z - mu
    var = jnp.mean(zc * zc, axis=-1, keepdims=True)
    xn = zc * lax.rsqrt(var + LN_EPS) * g_ref[...] + b_ref[...]
    xo_ref[...] = xn
    if has_next:
        h_ref[...] = (xn * (1.0 + modn_ref[1:2, :]) + modn_ref[0:1, :]).astype(h_ref.dtype)


def postnorm(x, y, mods, l2, ln_g, ln_b, alpha, *, route=None, dest=None, next_dtype=None):
    b, s, d = x.shape
    ts = 256
    has_next = next_dtype is not None
    routed = route is not None
    nj = s // ts
    if routed:
        y_spec = pl.BlockSpec(memory_space=pl.ANY)
    else:
        y_spec = pl.BlockSpec((None, ts, d), lambda i, j: (i, j, 0))
    in_specs = [pl.BlockSpec((None, ts, d), lambda i, j: (i, j, 0)),
                y_spec,
                pl.BlockSpec((None, None, 3, d), lambda i, j: (l2, i, 0, 0)),
                pl.BlockSpec((1, d), lambda i, j: (0, 0)),
                pl.BlockSpec((1, d), lambda i, j: (0, 0))]
    args = [x, y, mods, ln_g.reshape(1, d), ln_b.reshape(1, d)]
    scratch = []
    if routed:
        n_steps = b * nj
        idx = dest.reshape(n_steps, ts, TOP_K).transpose(0, 2, 1).reshape(n_steps, 1, TOP_K * ts)
        in_specs += [pl.BlockSpec((None, ts, LANES), lambda i, j: (i, j, 0)),
                     pl.BlockSpec((None, 1, TOP_K * ts), lambda i, j: (i * nj + j, 0, 0),
                                  memory_space=pltpu.SMEM),
                     pl.BlockSpec((None, 1, TOP_K * ts),
                                  lambda i, j: (jnp.minimum(i * nj + j + 1, n_steps - 1), 0, 0),
                                  memory_space=pltpu.SMEM)]
        args += [route, idx, idx]
        scratch = [pltpu.VMEM((2, TOP_K * ts, d), F32), pltpu.SemaphoreType.DMA((2,))]
    out_specs = [pl.BlockSpec((None, ts, d), lambda i, j: (i, j, 0))]
    out_shape = [jax.ShapeDtypeStruct((b, s, d), F32)]
    if has_next:
        in_specs.append(pl.BlockSpec((None, None, 3, d), lambda i, j: (l2 + 1, i, 0, 0)))
        args.append(mods)
        out_specs.append(pl.BlockSpec((None, ts, d), lambda i, j: (i, j, 0)))
        out_shape.append(jax.ShapeDtypeStruct((b, s, d), next_dtype))
    sem = ("arbitrary", "arbitrary") if routed else ("parallel", "parallel")
    out = pl.pallas_call(
        functools.partial(_postnorm_kernel, alpha=alpha, routed=routed, has_next=has_next),
        grid=(b, nj),
        in_specs=in_specs,
        out_specs=out_specs,
        out_shape=out_shape,
        scratch_shapes=scratch,
        compiler_params=_cparams(sem),
        name="postnorm",
    )(*args)
    return (out[0], out[1]) if has_next else (out[0], None)


def _rms_rows(a_ref, g_ref):
    a = a_ref[...].astype(F32)
    return (a * lax.rsqrt(jnp.mean(a * a, axis=-1, keepdims=True) + RMS_EPS) * g_ref[...]).astype(BF16)


def _matmul_kernel(a_ref, w_ref, o_ref):
    o_ref[...] = jnp.dot(a_ref[...], w_ref[...], preferred_element_type=F32).astype(o_ref.dtype)


def matmul(a, w, out_dtype, tm=512, tn=None):
    m, k = a.shape
    n = w.shape[1]
    tn = n if tn is None else tn
    return pl.pallas_call(
        _matmul_kernel,
        grid=(m // tm, n // tn),
        in_specs=[pl.BlockSpec((tm, k), lambda i, j: (i, 0)),
                  pl.BlockSpec((k, tn), lambda i, j: (0, j))],
        out_specs=pl.BlockSpec((tm, tn), lambda i, j: (i, j)),
        out_shape=jax.ShapeDtypeStruct((m, n), out_dtype),
        compiler_params=_cparams(("parallel", "parallel")),
        name="matmul",
    )(a, w)


def _matmul2_kernel(a1_ref, a2_ref, w1_ref, w2_ref, o_ref):
    acc = jnp.dot(a1_ref[...], w1_ref[...], preferred_element_type=F32)
    acc += jnp.dot(a2_ref[...], w2_ref[...], preferred_element_type=F32)
    o_ref[...] = acc.astype(o_ref.dtype)


def matmul2(a1, a2, w, out_dtype, tm=512):
    m, k1 = a1.shape
    k2 = a2.shape[1]
    assert k1 == k2
    n = w.shape[1]
    return pl.pallas_call(
        _matmul2_kernel,
        grid=(m // tm,),
        in_specs=[pl.BlockSpec((tm, k1), lambda i: (i, 0)),
                  pl.BlockSpec((tm, k2), lambda i: (i, 0)),
                  pl.BlockSpec((k1, n), lambda i: (0, 0)),
                  pl.BlockSpec((k2, n), lambda i: (1, 0))],
        out_specs=pl.BlockSpec((tm, n), lambda i: (i, 0)),
        out_shape=jax.ShapeDtypeStruct((m, n), out_dtype),
        compiler_params=_cparams(("parallel",)),
        name="matmul2",
    )(a1, a2, w, w)


def _inproj_ab_kernel(a_ref, w_ref, ra_ref, rb_ref, o_ref):
    acc = jnp.dot(a_ref[...], w_ref[...], preferred_element_type=F32)
    n = acc.shape[1]
    o_ref[:, :n - LANES] = acc[:, :n - LANES].astype(o_ref.dtype)
    last = acc[:, n - LANES:]
    roped = last * ra_ref[...] + pltpu.roll(last, LANES // 2, 1) * rb_ref[...]
    o_ref[:, n - LANES:] = roped.astype(o_ref.dtype)


def inproj_ab(h, w, rope_a, rope_b, seq):
    m, k = h.shape
    n = w.shape[1]
    tm = 512
    nsb = seq // tm
    return pl.pallas_call(
        _inproj_ab_kernel,
        grid=(m // tm,),
        in_specs=[pl.BlockSpec((tm, k), lambda i: (i, 0)),
                  pl.BlockSpec((k, n), lambda i: (0, 0)),
                  pl.BlockSpec((tm, LANES), lambda i: (i % nsb, 0)),
                  pl.BlockSpec((tm, LANES), lambda i: (i % nsb, 0))],
        out_specs=pl.BlockSpec((tm, n), lambda i: (i, 0)),
        out_shape=jax.ShapeDtypeStruct((m, n), BF16),
        compiler_params=_cparams(("parallel",)),
        name="inproj_ab",
    )(h, w, rope_a, rope_b)


def _uq_kernel(a_ref, g_ref, w_ref, ra_ref, rb_ref, o_ref):
    an = _rms_rows(a_ref, g_ref)
    acc = jnp.dot(an, w_ref[...], preferred_element_type=F32)
    n = acc.shape[1]
    o_ref[...] = (acc * ra_ref[...] + pltpu.roll(acc, n - B_ROPE, 1) * rb_ref[...]).astype(o_ref.dtype)


def uq_proj(p_ab, col_block, g, w, rope_a, rope_b, seq):
    m = p_ab.shape[0]
    k, n = w.shape
    tm = 512
    nsb = seq // tm
    return pl.pallas_call(
        _uq_kernel,
        grid=(m // tm,),
        in_specs=[pl.BlockSpec((tm, k), lambda i: (i, col_block)),
                  pl.BlockSpec((1, k), lambda i: (0, 0)),
                  pl.BlockSpec((k, n), lambda i: (0, 0)),
                  pl.BlockSpec((tm, n), lambda i: (i % nsb, 0)),
                  pl.BlockSpec((tm, n), lambda i: (i % nsb, 0))],
        out_specs=pl.BlockSpec((tm, n), lambda i: (i, 0)),
        out_shape=jax.ShapeDtypeStruct((m, n), BF16),
        compiler_params=_cparams(("parallel",)),
        name="uq_proj",
    )(p_ab, g, w, rope_a, rope_b)


def _ukv_kernel(a_ref, g_ref, w_ref, kr_ref, k_ref, v_ref):
    an = _rms_rows(a_ref, g_ref)
    acc = jnp.dot(an, w_ref[...], preferred_element_type=F32)
    kr = kr_ref[...]
    for h in range(B_HEADS):
        base = h * (B_NOPE + B_V_DIM)
        k_ref[:, 2 * h * LANES:(2 * h + 1) * LANES] = acc[:, base:base + B_NOPE].astype(k_ref.dtype)
        k_ref[:, (2 * h + 1) * LANES:(2 * h + 2) * LANES] = kr
        v_ref[:, h * B_V_DIM:(h + 1) * B_V_DIM] = acc[:, base + B_NOPE:base + B_NOPE + B_V_DIM].astype(v_ref.dtype)


def ukv_proj(p_ab, ckv_block, kr_block, g, w):
    m = p_ab.shape[0]
    k, n = w.shape
    tm = 512
    return pl.pallas_call(
        _ukv_kernel,
        grid=(m // tm,),
        in_specs=[pl.BlockSpec((tm, k), lambda i: (i, ckv_block)),
                  pl.BlockSpec((1, k), lambda i: (0, 0)),
                  pl.BlockSpec((k, n), lambda i: (0, 0)),
                  pl.BlockSpec((tm, LANES), lambda i: (i, kr_block))],
        out_specs=[pl.BlockSpec((tm, B_HEADS * 2 * LANES), lambda i: (i, 0)),
                   pl.BlockSpec((tm, B_HEADS * B_V_DIM), lambda i: (i, 0))],
        out_shape=[jax.ShapeDtypeStruct((m, B_HEADS * 2 * LANES), BF16),
                   jax.ShapeDtypeStruct((m, B_HEADS * B_V_DIM), BF16)],
        compiler_params=_cparams(("parallel",)),
        name="ukv_proj",
    )(p_ab, g, w, p_ab)


_NT = (((1,), (1,)), ((), ()))


def _ones_block(rows):
    return jnp.ones((rows, LANES), BF16)


def _online_softmax_step(s, v_aug, m_ref, acc_ref):
    tk = s.shape[1]
    m_prev = m_ref[...]
    m_new = jnp.maximum(m_prev, jnp.max(s, axis=1, keepdims=True))
    alpha = jnp.exp2(m_prev - m_new)
    p = jnp.exp2(s - jnp.concatenate([m_new] * (tk // LANES), axis=1))
    pv = jnp.dot(p.astype(v_aug.dtype), v_aug, preferred_element_type=F32)
    acc_ref[...] = jnp.concatenate([alpha, alpha], axis=1) * acc_ref[...] + pv
    m_ref[...] = m_new


def _softmax_result(acc_ref):
    acc = acc_ref[...]
    return acc[:, :LANES] / acc[:, LANES:]


def _flash_diff_kernel(slopes_ref, q_ref, eq_ref, k_ref, ek_ref, v_ref, lam_ref, g_ref, o_ref,
                       qaug_sc, m_sc, acc_sc, *, tk, lam_init):
    tq = q_ref.shape[0]
    seq = k_ref.shape[0]
    qi = pl.program_id(2)
    q0 = qi * tq
    q = q_ref[...]
    eq = eq_ref[...]
    lane = lax.broadcasted_iota(jnp.int32, q.shape, 1)
    zero = jnp.zeros_like(q)
    for mp in range(2):
        qm = jnp.where((lane < A_QK_DIM) == (mp == 0), q, zero)
        qaug_sc[0, mp] = jnp.concatenate([qm, -eq], axis=1)
        qaug_sc[1, mp] = jnp.concatenate([qm, eq], axis=1)
    m_sc[...] = jnp.full(m_sc.shape, NEG_INF, F32)
    acc_sc[...] = jnp.zeros(acc_sc.shape, F32)
    ones_col = _ones_block(tk)

    def chunk(c, side, diagonal):
        k0 = pl.multiple_of(c * tk, tk)
        kc = jnp.concatenate([k_ref[pl.ds(k0, tk), :], ek_ref[pl.ds(k0, tk), :]], axis=1)
        v_aug = jnp.concatenate([v_ref[pl.ds(k0, tk), :], ones_col], axis=1)
        if diagonal:
            rel = (lax.broadcasted_iota(jnp.int32, (tq, tk), 0)
                   - lax.broadcasted_iota(jnp.int32, (tq, tk), 1)) + (q0 - k0)
            corr = (2.0 * slopes_ref[pl.program_id(1)]) * jnp.maximum(rel, 0).astype(F32)
        for mp in range(2):
            s = lax.dot_general(qaug_sc[side, mp], kc, _NT, preferred_element_type=F32)
            if diagonal:
                s = s - corr
            _online_softmax_step(s, v_aug, m_sc.at[mp], acc_sc.at[mp])

    def left(c, carry):
        chunk(c, 0, False)
        return carry

    def right(c, carry):
        chunk(c, 1, False)
        return carry

    cd = q0 // tk
    lax.fori_loop(0, cd, left, 0)
    chunk(cd, 1, True)
    lax.fori_loop(cd + 1, seq // tk, right, 0)

    o0 = _softmax_result(acc_sc.at[0])
    o1 = _softmax_result(acc_sc.at[1])
    lam = lam_ref[...]
    lam_full = (jnp.exp(jnp.sum(lam[0:1] * lam[1:2], axis=1, keepdims=True))
                - jnp.exp(jnp.sum(lam[2:3] * lam[3:4], axis=1, keepdims=True)) + lam_init)
    o = o0 - lam_full * o1
    on = o * lax.rsqrt(jnp.mean(o * o, axis=-1, keepdims=True) + RMS_EPS) * g_ref[...]
    o_ref[...] = (on * (1.0 - lam_init)).astype(o_ref.dtype)


def flash_diff(p_ab, eq, ek, slopes2, lam, diff_g, lam_init, bsz, seq):
    tq, tk = 1024, 1024
    assert tk % tq == 0
    nh = A_HEADS
    return pl.pallas_call(
        functools.partial(_flash_diff_kernel, tk=tk, lam_init=lam_init),
        grid=(bsz, nh, seq // tq),
        in_specs=[pl.BlockSpec(memory_space=pltpu.SMEM),
                  pl.BlockSpec((None, tq, LANES), lambda b, h, i: (b, i, h)),
                  pl.BlockSpec((None, tq, LANES), lambda b, h, i: (h, i, 0)),
                  pl.BlockSpec((None, seq, LANES), lambda b, h, i: (b, 0, nh + h)),
                  pl.BlockSpec((None, seq, LANES), lambda b, h, i: (h, 0, 0)),
                  pl.BlockSpec((None, seq, LANES), lambda b, h, i: (b, 0, 2 * nh + h)),
                  pl.BlockSpec((4, A_QK_DIM), lambda b, h, i: (0, 0)),
                  pl.BlockSpec((1, A_V_DIM), lambda b, h, i: (0, 0))],
        out_specs=pl.BlockSpec((None, tq, A_V_DIM), lambda b, h, i: (b, i, h)),
        out_shape=jax.ShapeDtypeStruct((bsz, seq, nh * A_V_DIM), BF16),
        scratch_shapes=[pltpu.VMEM((2, 2, tq, 2 * LANES), BF16),
                        pltpu.VMEM((2, tq, LANES), F32),
                        pltpu.VMEM((2, tq, 2 * LANES), F32)],
        compiler_params=_cparams(("parallel", "parallel", "parallel")),
        name="flash_diff",
    )(slopes2, p_ab, eq, p_ab, ek, p_ab, lam, diff_g)


def _flash_mla_kernel(q_ref, k_ref, v_ref, o_ref, m_sc, acc_sc, *, tk):
    seq = k_ref.shape[0]
    q = q_ref[...]
    m_sc[...] = jnp.full(m_sc.shape, NEG_INF, F32)
    acc_sc[...] = jnp.zeros(acc_sc.shape, F32)
    ones_col = _ones_block(tk)

    def body(c, carry):
        k0 = pl.multiple_of(c * tk, tk)
        s = lax.dot_general(q, k_ref[pl.ds(k0, tk), :], _NT, preferred_element_type=F32)
        v_aug = jnp.concatenate([v_ref[pl.ds(k0, tk), :], ones_col], axis=1)
        _online_softmax_step(s, v_aug, m_sc, acc_sc)
        return carry

    lax.fori_loop(0, seq // tk, body, 0)
    o_ref[...] = _softmax_result(acc_sc).astype(o_ref.dtype)


def flash_mla(q_cat, k_cat, v_b, bsz, seq):
    tq, tk = 1024, 1024
    kd = 2 * LANES
    return pl.pallas_call(
        functools.partial(_flash_mla_kernel, tk=tk),
        grid=(bsz, B_HEADS, seq // tq),
        in_specs=[pl.BlockSpec((None, tq, kd), lambda b, h, i: (b, i, h)),
                  pl.BlockSpec((None, seq, kd), lambda b, h, i: (b, 0, h)),
                  pl.BlockSpec((None, seq, B_V_DIM), lambda b, h, i: (b, 0, h))],
        out_specs=pl.BlockSpec((None, tq, B_V_DIM), lambda b, h, i: (b, i, h)),
        out_shape=jax.ShapeDtypeStruct((bsz, seq, B_HEADS * B_V_DIM), BF16),
        scratch_shapes=[pltpu.VMEM((tq, LANES), F32), pltpu.VMEM((tq, 2 * LANES), F32)],
        compiler_params=_cparams(("parallel", "parallel", "parallel")),
        name="flash_mla",
    )(q_cat, k_cat, v_b)


def _window_kernel(slopes_ref, sink_ref, q_ref, k_ref, v_ref, o_ref):
    tq = q_ref.shape[0]
    seq = k_ref.shape[0]
    blk = WINDOW
    kw = 3 * blk
    n = pl.program_id(1)
    q0 = pl.program_id(2) * tq
    rel = (lax.broadcasted_iota(jnp.int32, (blk, kw), 1) - lax.broadcasted_iota(jnp.int32, (blk, kw), 0))
    ones_blk = _ones_block(kw)
    heads = [n * C_GROUP + g for g in range(C_GROUP)]
    slope_rows = jnp.concatenate([jnp.full((blk, kw), slopes_ref[hd], F32) for hd in heads], axis=0)
    sink_rows = jnp.concatenate([jnp.full((blk, LANES), sink_ref[hd], F32) for hd in heads], axis=0)
    for sb in range(tq // blk):
        t0 = q0 + sb * blk
        start = pl.multiple_of(jnp.clip(t0 - blk, 0, seq - kw), blk)
        kc = k_ref[pl.ds(start, kw), :]
        v_aug = jnp.concatenate([v_ref[pl.ds(start, kw), :], ones_blk], axis=1)
        dist = jnp.abs(rel + (start - t0))
        dist_rows = jnp.concatenate([dist] * C_GROUP, axis=0)
        qs = jnp.concatenate([q_ref[sb * blk:(sb + 1) * blk, g * C_HEAD_DIM:(g + 1) * C_HEAD_DIM]
                              for g in range(C_GROUP)], axis=0)
        s = lax.dot_general(qs, kc, _NT, preferred_element_type=F32) - slope_rows * dist_rows.astype(F32)
        s = jnp.where(dist_rows <= WINDOW, s, NEG_INF)
        m = jnp.maximum(jnp.max(s, axis=1, keepdims=True), sink_rows)
        e = jnp.exp(s - jnp.concatenate([m] * (kw // LANES), axis=1))
        pv = jnp.dot(e.astype(v_aug.dtype), v_aug, preferred_element_type=F32)
        o = pv[:, :C_HEAD_DIM] / (pv[:, C_HEAD_DIM:] + jnp.exp(sink_rows - m))
        for g in range(C_GROUP):
            o_ref[sb * blk:(sb + 1) * blk, g * C_HEAD_DIM:(g + 1) * C_HEAD_DIM] = (
                o[g * blk:(g + 1) * blk].astype(o_ref.dtype))


def window_attn(p_c, slopes, sink, bsz, seq):
    tq = 512
    gw = C_GROUP * C_HEAD_DIM
    nq = C_HEADS * C_HEAD_DIM // LANES
    return pl.pallas_call(
        _window_kernel,
        grid=(bsz, C_KV_HEADS, seq // tq),
        in_specs=[pl.BlockSpec(memory_space=pltpu.SMEM),
                  pl.BlockSpec(memory_space=pltpu.SMEM),
                  pl.BlockSpec((None, tq, gw), lambda b, n, i: (b, i, n)),
                  pl.BlockSpec((None, seq, C_HEAD_DIM), lambda b, n, i: (b, 0, nq + n)),
                  pl.BlockSpec((None, seq, C_HEAD_DIM), lambda b, n, i: (b, 0, nq + C_KV_HEADS + n))],
        out_specs=pl.BlockSpec((None, tq, gw), lambda b, n, i: (b, i, n)),
        out_shape=jax.ShapeDtypeStruct((bsz, seq, C_HEADS * C_HEAD_DIM), BF16),
        compiler_params=_cparams(("parallel", "parallel", "parallel")),
        name="window_attn",
    )(slopes, sink, p_c, p_c, p_c)


def _swiglu_step(x, wg_ref, wu_ref, wd_ref, acc_ref):
    g = jnp.dot(x, wg_ref[...], preferred_element_type=F32)
    u = jnp.dot(x, wu_ref[...], preferred_element_type=F32)
    a = (g * jax.nn.sigmoid(g) * u).astype(BF16)
    acc_ref[...] += jnp.dot(a, wd_ref[...], preferred_element_type=F32)


def _ffn_kernel(x_ref, wg_ref, wu_ref, wd_ref, o_ref, acc_ref):
    j = pl.program_id(1)

    @pl.when(j == 0)
    def _():
        acc_ref[...] = jnp.zeros(acc_ref.shape, F32)

    _swiglu_step(x_ref[...], wg_ref, wu_ref, wd_ref, acc_ref)

    @pl.when(j == pl.num_programs(1) - 1)
    def _():
        o_ref[...] = acc_ref[...].astype(o_ref.dtype)


def ffn(h, wg, wu, wd):
    m, d = h.shape
    f = wg.shape[1]
    tm, tf = 1024, 512
    return pl.pallas_call(
        _ffn_kernel,
        grid=(m // tm, f // tf),
        in_specs=[pl.BlockSpec((tm, d), lambda i, j: (i, 0)),
                  pl.BlockSpec((d, tf), lambda i, j: (0, j)),
                  pl.BlockSpec((d, tf), lambda i, j: (0, j)),
                  pl.BlockSpec((tf, d), lambda i, j: (j, 0))],
        out_specs=pl.BlockSpec((tm, d), lambda i, j: (i, 0)),
        out_shape=jax.ShapeDtypeStruct((m, d), F32),
        scratch_shapes=[pltpu.VMEM((tm, d), F32)],
        compiler_params=_cparams(("parallel", "arbitrary")),
        name="ffn",
    )(h, wg, wu, wd)


def _moe_ffn_kernel(be_ref, nu_ref, tok_ref, tokn_ref, h_ref, wg_ref, wu_ref, wd_ref, o_ref,
                    xg_ref, xb_ref, acc_ref, sem):
    del be_ref
    i = pl.program_id(0)
    j = pl.program_id(1)
    rows = xg_ref.shape[0]
    n_used = nu_ref[0]

    def issue_gather(idx_ref):
        def body(r, carry):
            pltpu.make_async_copy(h_ref.at[pl.ds(idx_ref[0, r], 1)], xg_ref.at[pl.ds(r, 1)], sem).start()
            return carry
        lax.fori_loop(0, rows, body, 0, unroll=8)

    @pl.when((j == 0) & (i < n_used))
    def _():
        @pl.when(i == 0)
        def _():
            issue_gather(tok_ref)
        pltpu.make_async_copy(h_ref.at[pl.ds(0, rows)], xg_ref, sem).wait()
        xb_ref[...] = xg_ref[...].astype(BF16)
        acc_ref[...] = jnp.zeros(acc_ref.shape, F32)

    @pl.when((j == 1) & (i + 1 < n_used))
    def _():
        issue_gather(tokn_ref)

    @pl.when(i < n_used)
    def _():
        _swiglu_step(xb_ref[...], wg_ref, wu_ref, wd_ref, acc_ref)

    @pl.when((j == pl.num_programs(1) - 1) & (i < n_used))
    def _():
        o_ref[...] = acc_ref[...]

    @pl.when((j == pl.num_programs(1) - 1) & (i >= n_used))
    def _():
        o_ref[...] = jnp.zeros(o_ref.shape, F32)


def moe_ffn(block_e, n_used, row_tok, h, wg, wu, wd):
    d = h.shape[1]
    r = row_tok.shape[0]
    f = wg.shape[2]
    tm, tf = MOE_ROWS, 256
    nj = f // tf
    assert nj >= 2
    nblk = r // tm
    tok3 = row_tok.reshape(nblk, 1, tm)

    def w_col(i, j, be, nu):
        return jnp.where(i < nu[0], j, 0)

    grid_spec = pltpu.PrefetchScalarGridSpec(
        num_scalar_prefetch=2,
        grid=(nblk, nj),
        in_specs=[pl.BlockSpec((None, 1, tm), lambda i, j, be, nu: (i, 0, 0), memory_space=pltpu.SMEM),
                  pl.BlockSpec((None, 1, tm), lambda i, j, be, nu: (jnp.minimum(i + 1, nblk - 1), 0, 0),
                               memory_space=pltpu.SMEM),
                  pl.BlockSpec(memory_space=pl.ANY),
                  pl.BlockSpec((None, d, tf), lambda i, j, be, nu: (be[i], 0, w_col(i, j, be, nu))),
                  pl.BlockSpec((None, d, tf), lambda i, j, be, nu: (be[i], 0, w_col(i, j, be, nu))),
                  pl.BlockSpec((None, tf, d), lambda i, j, be, nu: (be[i], w_col(i, j, be, nu), 0))],
        out_specs=pl.BlockSpec((tm, d), lambda i, j, be, nu: (i, 0)),
        scratch_shapes=[pltpu.VMEM((tm, d), F32), pltpu.VMEM((tm, d), BF16),
                        pltpu.VMEM((tm, d), F32), pltpu.SemaphoreType.DMA(())],
    )
    return pl.pallas_call(
        _moe_ffn_kernel,
        grid_spec=grid_spec,
        out_shape=jax.ShapeDtypeStruct((r, d), F32),
        compiler_params=_cparams(("arbitrary", "arbitrary")),
        name="moe_ffn",
    )(block_e, n_used, tok3, tok3, h, wg, wu, wd)


def _router_kernel(h_ref, w_ref, b_ref, o_ref):
    logits = jnp.dot(h_ref[...], w_ref[...], preferred_element_type=F32,
                     precision=lax.Precision.HIGHEST) + b_ref[...]
    lane = lax.broadcasted_iota(jnp.int32, logits.shape, 1)
    logits = jnp.where(lane < N_EXPERTS, logits, NEG_INF)
    m1 = jnp.max(logits, axis=1, keepdims=True)
    i1 = jnp.min(jnp.where(logits == m1, lane, LANES), axis=1, keepdims=True)
    rest = jnp.where(lane == i1, NEG_INF, logits)
    m2 = jnp.max(rest, axis=1, keepdims=True)
    i2 = jnp.min(jnp.where(rest == m2, lane, LANES), axis=1, keepdims=True)
    e2 = jnp.exp(m2 - m1)
    g1 = 1.0 / (1.0 + e2)
    g2 = e2 / (1.0 + e2)
    out = jnp.where(lane == 0, i1.astype(F32),
                    jnp.where(lane == 1, i2.astype(F32),
                              jnp.where(lane == 2, g1, jnp.where(lane == 3, g2, 0.0))))
    o_ref[...] = out


def router(h, w_pad, b_pad):
    m, d = h.shape
    tm = 512
    return pl.pallas_call(
        _router_kernel,
        grid=(m // tm,),
        in_specs=[pl.BlockSpec((tm, d), lambda i: (i, 0)),
                  pl.BlockSpec((d, LANES), lambda i: (0, 0)),
                  pl.BlockSpec((1, LANES), lambda i: (0, 0))],
        out_specs=pl.BlockSpec((tm, LANES), lambda i: (i, 0)),
        out_shape=jax.ShapeDtypeStruct((m, LANES), F32),
        compiler_params=_cparams(("parallel",)),
        name="router",
    )(h, w_pad, b_pad)


def _alibi_slopes(n_heads):
    return jnp.asarray(2.0 ** (-8.0 * np.arange(1, n_heads + 1) / n_heads), dtype=F32)


def _bf16_parts(v):
    a = v.astype(BF16).astype(F32)
    b = (v - a).astype(BF16).astype(F32)
    c = (v - a - b).astype(BF16).astype(F32)
    return a, b, c


def _alibi_tables(seq, slopes2):
    nh = slopes2.shape[0]
    pos = jnp.arange(seq, dtype=jnp.int32)
    pos_parts = (((pos // LANES) * LANES).astype(F32), (pos % LANES).astype(F32))
    cols_q, cols_k = [], []
    for sp in _bf16_parts(slopes2):
        sp_col = jnp.broadcast_to(sp[:, None], (nh, seq))
        for pp in pos_parts:
            pp_row = jnp.broadcast_to(pp[None, :], (nh, seq))
            cols_q += [pp_row, sp_col]
            cols_k += [sp_col, -pp_row]
    pad = jnp.zeros((nh, seq, LANES - len(cols_q)), F32)
    eq = jnp.concatenate([jnp.stack(cols_q, axis=-1), pad], axis=-1).astype(BF16)
    ek = jnp.concatenate([jnp.stack(cols_k, axis=-1), pad], axis=-1).astype(BF16)
    return eq, ek


def _rope_tables(seq):
    inv_freq = ROPE_THETA ** (-jnp.arange(0, B_ROPE, 2, dtype=F32) / B_ROPE)
    ang = jnp.arange(seq, dtype=F32)[:, None] * inv_freq[None, :]
    cos, sin = jnp.cos(ang), jnp.sin(ang)
    cos2 = jnp.concatenate([cos, cos], axis=-1)
    sin2 = jnp.concatenate([sin, sin], axis=-1)
    z64 = jnp.zeros((seq, B_ROPE), F32)
    ka = jnp.concatenate([cos2, z64], axis=-1)
    kb = jnp.concatenate([sin2, z64], axis=-1)
    ones = jnp.ones((seq, B_NOPE), F32)
    z128 = jnp.zeros((seq, B_NOPE), F32)
    qa = jnp.tile(jnp.concatenate([ones, cos2, z64], axis=-1), (1, B_HEADS))
    qb = jnp.tile(jnp.concatenate([z128, sin2, z64], axis=-1), (1, B_HEADS))
    return ka, kb, qa, qb


def _rot_cols(w):
    half = w.shape[-1] // 2
    return jnp.concatenate([-w[..., half:], w[..., :half]], axis=-1)


def _routing_tables(flat_e):
    n_assign = flat_e.shape[0]
    onehot = (flat_e[:, None] == jnp.arange(N_EXPERTS, dtype=jnp.int32)[None, :]).astype(jnp.int32)
    csum = jnp.cumsum(onehot, axis=0)
    rank = jnp.sum((csum - onehot) * onehot, axis=1)
    sizes = csum[-1]
    padded = ((sizes + MOE_ROWS - 1) // MOE_ROWS) * MOE_ROWS
    pad_end = jnp.cumsum(padded)
    pad_start = pad_end - padded
    dest = (pad_start[flat_e] + rank).astype(jnp.int32)
    n_blocks = -(-n_assign // MOE_ROWS) + N_EXPERTS
    n_rows = n_blocks * MOE_ROWS
    tok = jnp.arange(n_assign, dtype=jnp.int32) // TOP_K
    row_tok = jnp.zeros((n_rows,), jnp.int32).at[dest].set(tok)
    block_start = jnp.arange(n_blocks, dtype=jnp.int32) * MOE_ROWS
    block_e = jnp.minimum(jnp.searchsorted(pad_end, block_start, side='right'), N_EXPERTS - 1).astype(jnp.int32)
    n_used = (pad_end[-1:] // MOE_ROWS).astype(jnp.int32)
    return dest, row_tok, block_e, n_used


def kernel(x, c, ada_w, ada_b, ln_g, ln_b, ab_w_in, ab_lam, ab_diff_g, ab_q_norm_g, ab_kv_norm_g, ab_w_uq,
           ab_w_ukv, ab_w_out, c_w_in, c_sink, c_w_out, ffn_w_gate, ffn_w_up, ffn_w_down, moe_w_router,
           moe_b_router, moe_w_gate, moe_w_up, moe_w_down):
    bsz, seq, d = x.shape
    depth = ada_w.shape[0]
    n_tok = bsz * seq
    alpha = (2 * depth) ** 0.25

    c_pad = jnp.zeros((8, d), F32).at[:bsz].set(c)
    mods = adaln_all(c_pad, ada_w.reshape(2 * depth, d, 3 * d), ada_b.reshape(2 * depth, 1, 3 * d))
    mods = mods.reshape(2 * depth, 8, 3, d)

    ka, kb, qa, qb = _rope_tables(seq)
    slopes_a2 = _alibi_slopes(A_HEADS) * LOG2E
    eq_a, ek_a = _alibi_tables(seq, slopes_a2)
    slopes_c = _alibi_slopes(C_HEADS)
    scale_a = A_QK_DIM ** -0.5 * LOG2E
    scale_b = (B_NOPE + B_ROPE) ** -0.5 * LOG2E
    scale_c = C_HEAD_DIM ** -0.5

    h = modulate(x, mods, 0, BF16)
    for layer in range(depth):
        i = layer // 2
        l2 = 2 * layer
        last = layer == depth - 1
        if layer % 2 == 0:
            lam_init = 0.8 - 0.6 * math.exp(-0.3 * layer)
            w_in = ab_w_in[i]
            w_kr = w_in[:, 3 * A_COLS + B_Q_LORA + B_KV_LORA:]
            w_in2 = jnp.concatenate([w_in[:, :A_COLS] * scale_a, w_in[:, A_COLS:], _rot_cols(w_kr)],
                                    axis=1).astype(BF16)
            p_ab = inproj_ab(h.reshape(n_tok, d), w_in2, ka, kb, seq)
            o_a = flash_diff(p_ab.reshape(bsz, seq, -1), eq_a, ek_a, slopes_a2, ab_lam[i],
                             ab_diff_g[i].reshape(1, -1), lam_init, bsz, seq)
            w_uq = ab_w_uq[i].reshape(B_Q_LORA, B_HEADS, B_NOPE + B_ROPE) * scale_b
            w_uq2 = jnp.concatenate([w_uq, _rot_cols(w_uq[..., B_NOPE:])], axis=-1)
            w_uq2 = w_uq2.reshape(B_Q_LORA, B_HEADS * 2 * LANES).astype(BF16)
            q_cat = uq_proj(p_ab, 3 * A_COLS // B_Q_LORA, ab_q_norm_g[i].reshape(1, -1), w_uq2, qa, qb, seq)
            k_cat, v_b = ukv_proj(p_ab, (3 * A_COLS + B_Q_LORA) // B_KV_LORA,
                                  (3 * A_COLS + B_Q_LORA + B_KV_LORA) // LANES,
                                  ab_kv_norm_g[i].reshape(1, -1), ab_w_ukv[i].astype(BF16))
            o_b = flash_mla(q_cat.reshape(bsz, seq, -1), k_cat.reshape(bsz, seq, -1),
                            v_b.reshape(bsz, seq, -1), bsz, seq)
            y = matmul2(o_a.reshape(n_tok, -1), o_b.reshape(n_tok, -1), ab_w_out[i].astype(BF16), F32)
        else:
            w_in = c_w_in[i]
            nqc = C_HEADS * C_HEAD_DIM
            w_in2 = jnp.concatenate([w_in[:, :nqc] * scale_c, w_in[:, nqc:]], axis=1).astype(BF16)
            p_c = matmul(h.reshape(n_tok, d), w_in2, BF16, tn=1024)
            o_c = window_attn(p_c.reshape(bsz, seq, -1), slopes_c, c_sink[i].astype(F32), bsz, seq)
            y = matmul(o_c.reshape(n_tok, -1), c_w_out[i].astype(BF16), F32)
        moe_next = layer % 2 == 1
        x, h = postnorm(x, y.reshape(bsz, seq, d), mods, l2, ln_g[layer, 0], ln_b[layer, 0], alpha,
                        next_dtype=F32 if moe_next else BF16)

        if not moe_next:
            y = ffn(h.reshape(n_tok, d), ffn_w_gate[i].astype(BF16), ffn_w_up[i].astype(BF16),
                    ffn_w_down[i].astype(BF16)).reshape(bsz, seq, d)
            route = dest = None
        else:
            hf = h.reshape(n_tok, d)
            w_r = jnp.zeros((d, LANES), F32).at[:, :N_EXPERTS].set(moe_w_router[i])
            b_r = jnp.zeros((1, LANES), F32).at[0, :N_EXPERTS].set(moe_b_router[i])
            route = router(hf, w_r, b_r)
            flat_e = route[:, :TOP_K].astype(jnp.int32).reshape(-1)
            dest, row_tok, block_e, n_used = _routing_tables(flat_e)
            y = moe_ffn(block_e, n_used, row_tok, hf, moe_w_gate[i].astype(BF16), moe_w_up[i].astype(BF16),
                        moe_w_down[i].astype(BF16))
            route = route.reshape(bsz, seq, LANES)
        x, h = postnorm(x, y, mods, l2 + 1, ln_g[layer, 1], ln_b[layer, 1], alpha,
                        route=route, dest=dest, next_dtype=None if last else BF16)
    return x
```

```python
import functools
import math

import numpy as np
import jax
import jax.numpy as jnp
from jax import lax
from jax.experimental import pallas as pl
from jax.experimental.pallas import tpu as pltpu

F32 = jnp.float32
BF16 = jnp.bfloat16

A_HEADS = 4
A_QK_DIM = 64
A_V_DIM = 128
B_HEADS = 4
B_Q_LORA = 512
B_KV_LORA = 256
B_NOPE = 128
B_ROPE = 64
B_V_DIM = 128
ROPE_THETA = 10000.0
C_HEADS = 16
C_KV_HEADS = 4
C_GROUP = C_HEADS // C_KV_HEADS
C_HEAD_DIM = 128
WINDOW = 128
N_EXPERTS = 8
TOP_K = 2
MOE_ROWS = 1024
LN_EPS = 1e-5
RMS_EPS = 1e-6

A_COLS = A_HEADS * 2 * A_QK_DIM
LANES = 128
VMEM_LIMIT = 56 * 1024 * 1024
NEG_INF = float("-inf")
LOG2E = math.log2(math.e)


def _cparams(sem, vmem=VMEM_LIMIT):
    return pltpu.CompilerParams(dimension_semantics=sem, vmem_limit_bytes=vmem)


def _adaln_kernel(c_ref, w_ref, b_ref, o_ref):
    c = c_ref[...]
    ca = c * jax.nn.sigmoid(c)
    o_ref[...] = jnp.dot(ca, w_ref[...], preferred_element_type=F32,
                         precision=lax.Precision.HIGHEST) + b_ref[...]


def adaln_all(c_pad, ada_w, ada_b):
    l2, d, n = ada_w.shape
    tn = 768
    return pl.pallas_call(
        _adaln_kernel,
        grid=(l2, n // tn),
        in_specs=[pl.BlockSpec((8, d), lambda l, j: (0, 0)),
                  pl.BlockSpec((None, d, tn), lambda l, j: (l, 0, j)),
                  pl.BlockSpec((None, 1, tn), lambda l, j: (l, 0, j))],
        out_specs=pl.BlockSpec((None, 8, tn), lambda l, j: (l, 0, j)),
        out_shape=jax.ShapeDtypeStruct((l2, 8, n), F32),
        compiler_params=_cparams(("parallel", "parallel")),
        name="adaln",
    )(c_pad, ada_w, ada_b)


def _modulate_kernel(x_ref, mod_ref, h_ref):
    x = x_ref[...]
    h_ref[...] = (x * (1.0 + mod_ref[1:2, :]) + mod_ref[0:1, :]).astype(h_ref.dtype)


def modulate(x, mods, l2, out_dtype):
    b, s, d = x.shape
    ts = 512
    return pl.pallas_call(
        _modulate_kernel,
        grid=(b, s // ts),
        in_specs=[pl.BlockSpec((None, ts, d), lambda i, j: (i, j, 0)),
                  pl.BlockSpec((None, None, 3, d), lambda i, j: (l2, i, 0, 0))],
        out_specs=pl.BlockSpec((None, ts, d), lambda i, j: (i, j, 0)),
        out_shape=jax.ShapeDtypeStruct((b, s, d), out_dtype),
        compiler_params=_cparams(("parallel", "parallel")),
        name="modulate",
    )(x, mods)


def _postnorm_kernel(*refs, alpha, routed, has_next):
    refs = list(refs)
    x_ref, y_ref, modc_ref, g_ref, b_ref = refs[:5]
    rest = refs[5:]
    if routed:
        route_ref, idx_ref, idxn_ref = rest[:3]
        rest = rest[3:]
    modn_ref = rest.pop(0) if has_next else None
    xo_ref = rest.pop(0)
    h_ref = rest.pop(0) if has_next else None
    x = x_ref[...]
    ts = x.shape[0]
    if routed:
        yg_ref, sem = rest
        step = pl.program_id(0) * pl.num_programs(1) + pl.program_id(1)
        n_steps = pl.num_programs(0) * pl.num_programs(1)
        slot = step % 2

        def issue_gather(idx, dst_slot):
            def body(r, carry):
                pltpu.make_async_copy(y_ref.at[pl.ds(idx[0, r], 1)], yg_ref.at[dst_slot, pl.ds(r, 1)],
                                      sem.at[dst_slot]).start()
                return carry
            lax.fori_loop(0, TOP_K * ts, body, 0, unroll=8)

        @pl.when(step == 0)
        def _():
            issue_gather(idx_ref, 0)

        @pl.when(step + 1 < n_steps)
        def _():
            issue_gather(idxn_ref, 1 - slot)

        pltpu.make_async_copy(y_ref.at[pl.ds(0, TOP_K * ts)], yg_ref.at[slot], sem.at[slot]).wait()
        g1 = route_ref[:, TOP_K:TOP_K + 1]
        g2 = route_ref[:, TOP_K + 1:TOP_K + 2]
        y = g1 * yg_ref[slot, 0:ts, :] + g2 * yg_ref[slot, ts:2 * ts, :]
    else:
        y = y_ref[...].astype(F32)
    z = alpha * x + (1.0 + modc_ref[2:3, :]) * y
    mu = jnp.mean(z, axis=-1, keepdims=True)
    zc = z - mu
    var = jnp.mean(zc * zc, axis=-1, keepdims=True)
    xn = zc * lax.rsqrt(var + LN_EPS) * g_ref[...] + b_ref[...]
    xo_ref[...] = xn
    if has_next:
        h_ref[...] = (xn * (1.0 + modn_ref[1:2, :]) + modn_ref[0:1, :]).astype(h_ref.dtype)


def postnorm(x, y, mods, l2, ln_g, ln_b, alpha, *, route=None, dest=None, next_dtype=None):
    b, s, d = x.shape
    ts = 256
    has_next = next_dtype is not None
    routed = route is not None
    nj = s // ts
    if routed:
        y_spec = pl.BlockSpec(memory_space=pl.ANY)
    else:
        y_spec = pl.BlockSpec((None, ts, d), lambda i, j: (i, j, 0))
    in_specs = [pl.BlockSpec((None, ts, d), lambda i, j: (i, j, 0)),
                y_spec,
                pl.BlockSpec((None, None, 3, d), lambda i, j: (l2, i, 0, 0)),
                pl.BlockSpec((1, d), lambda i, j: (0, 0)),
                pl.BlockSpec((1, d), lambda i, j: (0, 0))]
    args = [x, y, mods, ln_g.reshape(1, d), ln_b.reshape(1, d)]
    scratch = []
    if routed:
        n_steps = b * nj
        idx = dest.reshape(n_steps, ts, TOP_K).transpose(0, 2, 1).reshape(n_steps, 1, TOP_K * ts)
        in_specs += [pl.BlockSpec((None, ts, LANES), lambda i, j: (i, j, 0)),
                     pl.BlockSpec((None, 1, TOP_K * ts), lambda i, j: (i * nj + j, 0, 0),
                                  memory_space=pltpu.SMEM),
                     pl.BlockSpec((None, 1, TOP_K * ts),
                                  lambda i, j: (jnp.minimum(i * nj + j + 1, n_steps - 1), 0, 0),
                                  memory_space=pltpu.SMEM)]
        args += [route, idx, idx]
        scratch = [pltpu.VMEM((2, TOP_K * ts, d), F32), pltpu.SemaphoreType.DMA((2,))]
    out_specs = [pl.BlockSpec((None, ts, d), lambda i, j: (i, j, 0))]
    out_shape = [jax.ShapeDtypeStruct((b, s, d), F32)]
    if has_next:
        in_specs.append(pl.BlockSpec((None, None, 3, d), lambda i, j: (l2 + 1, i, 0, 0)))
        args.append(mods)
        out_specs.append(pl.BlockSpec((None, ts, d), lambda i, j: (i, j, 0)))
        out_shape.append(jax.ShapeDtypeStruct((b, s, d), next_dtype))
    sem = ("arbitrary", "arbitrary") if routed else ("parallel", "parallel")
    out = pl.pallas_call(
        functools.partial(_postnorm_kernel, alpha=alpha, routed=routed, has_next=has_next),
        grid=(b, nj),
        in_specs=in_specs,
        out_specs=out_specs,
        out_shape=out_shape,
        scratch_shapes=scratch,
        compiler_params=_cparams(sem),
        name="postnorm",
    )(*args)
    return (out[0], out[1]) if has_next else (out[0], None)


def _rms_rows(a_ref, g_ref):
    a = a_ref[...].astype(F32)
    return (a * lax.rsqrt(jnp.mean(a * a, axis=-1, keepdims=True) + RMS_EPS) * g_ref[...]).astype(BF16)


def _matmul_kernel(a_ref, w_ref, o_ref):
    o_ref[...] = jnp.dot(a_ref[...], w_ref[...], preferred_element_type=F32).astype(o_ref.dtype)


def matmul(a, w, out_dtype, tm=512, tn=None):
    m, k = a.shape
    n = w.shape[1]
    tn = n if tn is None else tn
    return pl.pallas_call(
        _matmul_kernel,
        grid=(m // tm, n // tn),
        in_specs=[pl.BlockSpec((tm, k), lambda i, j: (i, 0)),
                  pl.BlockSpec((k, tn), lambda i, j: (0, j))],
        out_specs=pl.BlockSpec((tm, tn), lambda i, j: (i, j)),
        out_shape=jax.ShapeDtypeStruct((m, n), out_dtype),
        compiler_params=_cparams(("parallel", "parallel")),
        name="matmul",
    )(a, w)


def _matmul2_kernel(a1_ref, a2_ref, w1_ref, w2_ref, o_ref):
    acc = jnp.dot(a1_ref[...], w1_ref[...], preferred_element_type=F32)
    acc += jnp.dot(a2_ref[...], w2_ref[...], preferred_element_type=F32)
    o_ref[...] = acc.astype(o_ref.dtype)


def matmul2(a1, a2, w, out_dtype, tm=512):
    m, k1 = a1.shape
    k2 = a2.shape[1]
    assert k1 == k2
    n = w.shape[1]
    return pl.pallas_call(
        _matmul2_kernel,
        grid=(m // tm,),
        in_specs=[pl.BlockSpec((tm, k1), lambda i: (i, 0)),
                  pl.BlockSpec((tm, k2), lambda i: (i, 0)),
                  pl.BlockSpec((k1, n), lambda i: (0, 0)),
                  pl.BlockSpec((k2, n), lambda i: (1, 0))],
        out_specs=pl.BlockSpec((tm, n), lambda i: (i, 0)),
        out_shape=jax.ShapeDtypeStruct((m, n), out_dtype),
        compiler_params=_cparams(("parallel",)),
        name="matmul2",
    )(a1, a2, w, w)


def _inproj_ab_kernel(a_ref, w_ref, ra_ref, rb_ref, o_ref):
    acc = jnp.dot(a_ref[...], w_ref[...], preferred_element_type=F32)
    n = acc.shape[1]
    o_ref[:, :n - LANES] = acc[:, :n - LANES].astype(o_ref.dtype)
    last = acc[:, n - LANES:]
    roped = last * ra_ref[...] + pltpu.roll(last, LANES // 2, 1) * rb_ref[...]
    o_ref[:, n - LANES:] = roped.astype(o_ref.dtype)


def inproj_ab(h, w, rope_a, rope_b, seq):
    m, k = h.shape
    n = w.shape[1]
    tm = 512
    nsb = seq // tm
    return pl.pallas_call(
        _inproj_ab_kernel,
        grid=(m // tm,),
        in_specs=[pl.BlockSpec((tm, k), lambda i: (i, 0)),
                  pl.BlockSpec((k, n), lambda i: (0, 0)),
                  pl.BlockSpec((tm, LANES), lambda i: (i % nsb, 0)),
                  pl.BlockSpec((tm, LANES), lambda i: (i % nsb, 0))],
        out_specs=pl.BlockSpec((tm, n), lambda i: (i, 0)),
        out_shape=jax.ShapeDtypeStruct((m, n), BF16),
        compiler_params=_cparams(("parallel",)),
        name="inproj_ab",
    )(h, w, rope_a, rope_b)


def _uq_kernel(a_ref, g_ref, w_ref, ra_ref, rb_ref, o_ref):
    an = _rms_rows(a_ref, g_ref)
    acc = jnp.dot(an, w_ref[...], preferred_element_type=F32)
    n = acc.shape[1]
    o_ref[...] = (acc * ra_ref[...] + pltpu.roll(acc, n - B_ROPE, 1) * rb_ref[...]).astype(o_ref.dtype)


def uq_proj(p_ab, col_block, g, w, rope_a, rope_b, seq):
    m = p_ab.shape[0]
    k, n = w.shape
    tm = 512
    nsb = seq // tm
    return pl.pallas_call(
        _uq_kernel,
        grid=(m // tm,),
        in_specs=[pl.BlockSpec((tm, k), lambda i: (i, col_block)),
                  pl.BlockSpec((1, k), lambda i: (0, 0)),
                  pl.BlockSpec((k, n), lambda i: (0, 0)),
                  pl.BlockSpec((tm, n), lambda i: (i % nsb, 0)),
                  pl.BlockSpec((tm, n), lambda i: (i % nsb, 0))],
        out_specs=pl.BlockSpec((tm, n), lambda i: (i, 0)),
        out_shape=jax.ShapeDtypeStruct((m, n), BF16),
        compiler_params=_cparams(("parallel",)),
        name="uq_proj",
    )(p_ab, g, w, rope_a, rope_b)


def _ukv_kernel(a_ref, g_ref, w_ref, kr_ref, k_ref, v_ref):
    an = _rms_rows(a_ref, g_ref)
    acc = jnp.dot(an, w_ref[...], preferred_element_type=F32)
    kr = kr_ref[...]
    for h in range(B_HEADS):
        base = h * (B_NOPE + B_V_DIM)
        k_ref[:, 2 * h * LANES:(2 * h + 1) * LANES] = acc[:, base:base + B_NOPE].astype(k_ref.dtype)
        k_ref[:, (2 * h + 1) * LANES:(2 * h + 2) * LANES] = kr
        v_ref[:, h * B_V_DIM:(h + 1) * B_V_DIM] = acc[:, base + B_NOPE:base + B_NOPE + B_V_DIM].astype(v_ref.dtype)


def ukv_proj(p_ab, ckv_block, kr_block, g, w):
    m = p_ab.shape[0]
    k, n = w.shape
    tm = 512
    return pl.pallas_call(
        _ukv_kernel,
        grid=(m // tm,),
        in_specs=[pl.BlockSpec((tm, k), lambda i: (i, ckv_block)),
                  pl.BlockSpec((1, k), lambda i: (0, 0)),
                  pl.BlockSpec((k, n), lambda i: (0, 0)),
                  pl.BlockSpec((tm, LANES), lambda i: (i, kr_block))],
        out_specs=[pl.BlockSpec((tm, B_HEADS * 2 * LANES), lambda i: (i, 0)),
                   pl.BlockSpec((tm, B_HEADS * B_V_DIM), lambda i: (i, 0))],
        out_shape=[jax.ShapeDtypeStruct((m, B_HEADS * 2 * LANES), BF16),
                   jax.ShapeDtypeStruct((m, B_HEADS * B_V_DIM), BF16)],
        compiler_params=_cparams(("parallel",)),
        name="ukv_proj",
    )(p_ab, g, w, p_ab)


_NT = (((1,), (1,)), ((), ()))


def _ones_block(rows):
    return jnp.ones((rows, LANES), BF16)


def _online_softmax_step(s, v_aug, m_ref, acc_ref):
    tk = s.shape[1]
    m_prev = m_ref[...]
    m_new = jnp.maximum(m_prev, jnp.max(s, axis=1, keepdims=True))
    alpha = jnp.exp2(m_prev - m_new)
    p = jnp.exp2(s - jnp.concatenate([m_new] * (tk // LANES), axis=1))
    pv = jnp.dot(p.astype(v_aug.dtype), v_aug, preferred_element_type=F32)
    acc_ref[...] = jnp.concatenate([alpha, alpha], axis=1) * acc_ref[...] + pv
    m_ref[...] = m_new


def _softmax_result(acc_ref):
    acc = acc_ref[...]
    return acc[:, :LANES] / acc[:, LANES:]


def _flash_diff_kernel(slopes_ref, q_ref, eq_ref, k_ref, ek_ref, v_ref, lam_ref, g_ref, o_ref,
                       qaug_sc, m_sc, acc_sc, *, tk, lam_init, splits):
    tq = q_ref.shape[0]
    seq = k_ref.shape[0]
    qi = pl.program_id(2)
    q0 = qi * tq
    q = q_ref[...]
    eq = eq_ref[...]
    lane = lax.broadcasted_iota(jnp.int32, q.shape, 1)
    zero = jnp.zeros_like(q)
    for mp in range(2):
        qm = jnp.where((lane < A_QK_DIM) == (mp == 0), q, zero)
        qaug_sc[0, mp] = jnp.concatenate([qm, -eq], axis=1)
        qaug_sc[1, mp] = jnp.concatenate([qm, eq], axis=1)
    m_sc[...] = jnp.full(m_sc.shape, NEG_INF, F32)
    acc_sc[...] = jnp.zeros(acc_sc.shape, F32)
    ones_col = _ones_block(tk)

    def chunk(c, side, diagonal):
        k0 = pl.multiple_of(c * tk, tk)
        kc = jnp.concatenate([k_ref[pl.ds(k0, tk), :], ek_ref[pl.ds(k0, tk), :]], axis=1)
        v_aug = jnp.concatenate([v_ref[pl.ds(k0, tk), :], ones_col], axis=1)
        if diagonal:
            rel = (lax.broadcasted_iota(jnp.int32, (tq, tk), 0)
                   - lax.broadcasted_iota(jnp.int32, (tq, tk), 1)) + (q0 - k0)
            corr = (2.0 * slopes_ref[pl.program_id(1)]) * jnp.maximum(rel, 0).astype(F32)
        part = tq // splits
        for hh in range(splits):
            rows = pl.ds(hh * part, part)
            for mp in range(2):
                s = lax.dot_general(qaug_sc[side, mp, rows, :], kc, _NT, preferred_element_type=F32)
                if diagonal:
                    s = s - corr[hh * part:(hh + 1) * part]
                _online_softmax_step(s, v_aug, m_sc.at[mp, rows], acc_sc.at[mp, rows])

    def left(c, carry):
        chunk(c, 0, False)
        return carry

    def right(c, carry):
        chunk(c, 1, False)
        return carry

    cd = q0 // tk
    lax.fori_loop(0, cd, left, 0)
    chunk(cd, 1, True)
    lax.fori_loop(cd + 1, seq // tk, right, 0)

    o0 = _softmax_result(acc_sc.at[0])
    o1 = _softmax_result(acc_sc.at[1])
    lam = lam_ref[...]
    lam_full = (jnp.exp(jnp.sum(lam[0:1] * lam[1:2], axis=1, keepdims=True))
                - jnp.exp(jnp.sum(lam[2:3] * lam[3:4], axis=1, keepdims=True)) + lam_init)
    o = o0 - lam_full * o1
    on = o * lax.rsqrt(jnp.mean(o * o, axis=-1, keepdims=True) + RMS_EPS) * g_ref[...]
    o_ref[...] = (on * (1.0 - lam_init)).astype(o_ref.dtype)


def flash_diff(p_ab, eq, ek, slopes2, lam, diff_g, lam_init, bsz, seq):
    tq, tk = 1024, 1024
    assert tk % tq == 0
    nh = A_HEADS
    return pl.pallas_call(
        functools.partial(_flash_diff_kernel, tk=tk, lam_init=lam_init, splits=4),
        grid=(bsz, nh, seq // tq),
        in_specs=[pl.BlockSpec(memory_space=pltpu.SMEM),
                  pl.BlockSpec((None, tq, LANES), lambda b, h, i: (b, i, h)),
                  pl.BlockSpec((None, tq, LANES), lambda b, h, i: (h, i, 0)),
                  pl.BlockSpec((None, seq, LANES), lambda b, h, i: (b, 0, nh + h)),
                  pl.BlockSpec((None, seq, LANES), lambda b, h, i: (h, 0, 0)),
                  pl.BlockSpec((None, seq, LANES), lambda b, h, i: (b, 0, 2 * nh + h)),
                  pl.BlockSpec((4, A_QK_DIM), lambda b, h, i: (0, 0)),
                  pl.BlockSpec((1, A_V_DIM), lambda b, h, i: (0, 0))],
        out_specs=pl.BlockSpec((None, tq, A_V_DIM), lambda b, h, i: (b, i, h)),
        out_shape=jax.ShapeDtypeStruct((bsz, seq, nh * A_V_DIM), BF16),
        scratch_shapes=[pltpu.VMEM((2, 2, tq, 2 * LANES), BF16),
                        pltpu.VMEM((2, tq, LANES), F32),
                        pltpu.VMEM((2, tq, 2 * LANES), F32)],
        compiler_params=_cparams(("parallel", "parallel", "parallel")),
        name="flash_diff",
    )(slopes2, p_ab, eq, p_ab, ek, p_ab, lam, diff_g)


def _flash_mla_kernel(q_ref, k_ref, v_ref, o_ref, m_sc, acc_sc, *, tk, splits):
    seq = k_ref.shape[0]
    part = q_ref.shape[0] // splits
    m_sc[...] = jnp.full(m_sc.shape, NEG_INF, F32)
    acc_sc[...] = jnp.zeros(acc_sc.shape, F32)
    ones_col = _ones_block(tk)

    def body(c, carry):
        k0 = pl.multiple_of(c * tk, tk)
        kc = k_ref[pl.ds(k0, tk), :]
        v_aug = jnp.concatenate([v_ref[pl.ds(k0, tk), :], ones_col], axis=1)
        for hh in range(splits):
            rows = pl.ds(hh * part, part)
            s = lax.dot_general(q_ref[rows, :], kc, _NT, preferred_element_type=F32)
            _online_softmax_step(s, v_aug, m_sc.at[rows], acc_sc.at[rows])
        return carry

    lax.fori_loop(0, seq // tk, body, 0)
    o_ref[...] = _softmax_result(acc_sc).astype(o_ref.dtype)


def flash_mla(q_cat, k_cat, v_b, bsz, seq):
    tq, tk = 1024, 1024
    kd = 2 * LANES
    return pl.pallas_call(
        functools.partial(_flash_mla_kernel, tk=tk, splits=4),
        grid=(bsz, B_HEADS, seq // tq),
        in_specs=[pl.BlockSpec((None, tq, kd), lambda b, h, i: (b, i, h)),
                  pl.BlockSpec((None, seq, kd), lambda b, h, i: (b, 0, h)),
                  pl.BlockSpec((None, seq, B_V_DIM), lambda b, h, i: (b, 0, h))],
        out_specs=pl.BlockSpec((None, tq, B_V_DIM), lambda b, h, i: (b, i, h)),
        out_shape=jax.ShapeDtypeStruct((bsz, seq, B_HEADS * B_V_DIM), BF16),
        scratch_shapes=[pltpu.VMEM((tq, LANES), F32), pltpu.VMEM((tq, 2 * LANES), F32)],
        compiler_params=_cparams(("parallel", "parallel", "parallel")),
        name="flash_mla",
    )(q_cat, k_cat, v_b)


def _window_kernel(slopes_ref, sink_ref, q_ref, k_ref, v_ref, o_ref):
    tq = q_ref.shape[0]
    seq = k_ref.shape[0]
    blk = WINDOW
    kw = 3 * blk
    n = pl.program_id(1)
    q0 = pl.program_id(2) * tq
    rel = (lax.broadcasted_iota(jnp.int32, (blk, kw), 1) - lax.broadcasted_iota(jnp.int32, (blk, kw), 0))
    ones_blk = _ones_block(kw)
    heads = [n * C_GROUP + g for g in range(C_GROUP)]
    slope_rows = jnp.concatenate([jnp.full((blk, kw), slopes_ref[hd], F32) for hd in heads], axis=0)
    sink_rows = jnp.concatenate([jnp.full((blk, LANES), sink_ref[hd], F32) for hd in heads], axis=0)
    for sb in range(tq // blk):
        t0 = q0 + sb * blk
        start = pl.multiple_of(jnp.clip(t0 - blk, 0, seq - kw), blk)
        kc = k_ref[pl.ds(start, kw), :]
        v_aug = jnp.concatenate([v_ref[pl.ds(start, kw), :], ones_blk], axis=1)
        dist = jnp.abs(rel + (start - t0))
        dist_rows = jnp.concatenate([dist] * C_GROUP, axis=0)
        qs = jnp.concatenate([q_ref[sb * blk:(sb + 1) * blk, g * C_HEAD_DIM:(g + 1) * C_HEAD_DIM]
                              for g in range(C_GROUP)], axis=0)
        s = lax.dot_general(qs, kc, _NT, preferred_element_type=F32) - slope_rows * dist_rows.astype(F32)
        s = jnp.where(dist_rows <= WINDOW, s, NEG_INF)
        m = jnp.maximum(jnp.max(s, axis=1, keepdims=True), sink_rows)
        e = jnp.exp(s - jnp.concatenate([m] * (kw // LANES), axis=1))
        pv = jnp.dot(e.astype(v_aug.dtype), v_aug, preferred_element_type=F32)
        o = pv[:, :C_HEAD_DIM] / (pv[:, C_HEAD_DIM:] + jnp.exp(sink_rows - m))
        for g in range(C_GROUP):
            o_ref[sb * blk:(sb + 1) * blk, g * C_HEAD_DIM:(g + 1) * C_HEAD_DIM] = (
                o[g * blk:(g + 1) * blk].astype(o_ref.dtype))


def window_attn(p_c, slopes, sink, bsz, seq):
    tq = 512
    gw = C_GROUP * C_HEAD_DIM
    nq = C_HEADS * C_HEAD_DIM // LANES
    return pl.pallas_call(
        _window_kernel,
        grid=(bsz, C_KV_HEADS, seq // tq),
        in_specs=[pl.BlockSpec(memory_space=pltpu.SMEM),
                  pl.BlockSpec(memory_space=pltpu.SMEM),
                  pl.BlockSpec((None, tq, gw), lambda b, n, i: (b, i, n)),
                  pl.BlockSpec((None, seq, C_HEAD_DIM), lambda b, n, i: (b, 0, nq + n)),
                  pl.BlockSpec((None, seq, C_HEAD_DIM), lambda b, n, i: (b, 0, nq + C_KV_HEADS + n))],
        out_specs=pl.BlockSpec((None, tq, gw), lambda b, n, i: (b, i, n)),
        out_shape=jax.ShapeDtypeStruct((bsz, seq, C_HEADS * C_HEAD_DIM), BF16),
        compiler_params=_cparams(("parallel", "parallel", "parallel")),
        name="window_attn",
    )(slopes, sink, p_c, p_c, p_c)


def _swiglu_step(x, wg_ref, wu_ref, wd_ref, acc_ref):
    g = jnp.dot(x, wg_ref[...], preferred_element_type=F32)
    u = jnp.dot(x, wu_ref[...], preferred_element_type=F32)
    a = (g * jax.nn.sigmoid(g) * u).astype(BF16)
    acc_ref[...] += jnp.dot(a, wd_ref[...], preferred_element_type=F32)


def _ffn_kernel(x_ref, wg_ref, wu_ref, wd_ref, o_ref, acc_ref):
    j = pl.program_id(1)

    @pl.when(j == 0)
    def _():
        acc_ref[...] = jnp.zeros(acc_ref.shape, F32)

    _swiglu_step(x_ref[...], wg_ref, wu_ref, wd_ref, acc_ref)

    @pl.when(j == pl.num_programs(1) - 1)
    def _():
        o_ref[...] = acc_ref[...].astype(o_ref.dtype)


def ffn(h, wg, wu, wd):
    m, d = h.shape
    f = wg.shape[1]
    tm, tf = 1024, 512
    return pl.pallas_call(
        _ffn_kernel,
        grid=(m // tm, f // tf),
        in_specs=[pl.BlockSpec((tm, d), lambda i, j: (i, 0)),
                  pl.BlockSpec((d, tf), lambda i, j: (0, j)),
                  pl.BlockSpec((d, tf), lambda i, j: (0, j)),
                  pl.BlockSpec((tf, d), lambda i, j: (j, 0))],
        out_specs=pl.BlockSpec((tm, d), lambda i, j: (i, 0)),
        out_shape=jax.ShapeDtypeStruct((m, d), F32),
        scratch_shapes=[pltpu.VMEM((tm, d), F32)],
        compiler_params=_cparams(("parallel", "arbitrary")),
        name="ffn",
    )(h, wg, wu, wd)


def _moe_ffn_kernel(be_ref, nu_ref, tok_ref, tokn_ref, h_ref, wg_ref, wu_ref, wd_ref, o_ref,
                    xg_ref, xb_ref, acc_ref, sem):
    del be_ref
    i = pl.program_id(0)
    j = pl.program_id(1)
    rows = xg_ref.shape[0]
    n_used = nu_ref[0]

    def issue_gather(idx_ref):
        def body(r, carry):
            pltpu.make_async_copy(h_ref.at[pl.ds(idx_ref[0, r], 1)], xg_ref.at[pl.ds(r, 1)], sem).start()
            return carry
        lax.fori_loop(0, rows, body, 0, unroll=8)

    @pl.when((j == 0) & (i < n_used))
    def _():
        @pl.when(i == 0)
        def _():
            issue_gather(tok_ref)
        pltpu.make_async_copy(h_ref.at[pl.ds(0, rows)], xg_ref, sem).wait()
        xb_ref[...] = xg_ref[...].astype(BF16)
        acc_ref[...] = jnp.zeros(acc_ref.shape, F32)

    @pl.when((j == 1) & (i + 1 < n_used))
    def _():
        issue_gather(tokn_ref)

    @pl.when(i < n_used)
    def _():
        _swiglu_step(xb_ref[...], wg_ref, wu_ref, wd_ref, acc_ref)

    @pl.when((j == pl.num_programs(1) - 1) & (i < n_used))
    def _():
        o_ref[...] = acc_ref[...]

    @pl.when((j == pl.num_programs(1) - 1) & (i >= n_used))
    def _():
        o_ref[...] = jnp.zeros(o_ref.shape, F32)


def moe_ffn(block_e, n_used, row_tok, h, wg, wu, wd):
    d = h.shape[1]
    r = row_tok.shape[0]
    f = wg.shape[2]
    tm, tf = MOE_ROWS, 256
    nj = f // tf
    assert nj >= 2
    nblk = r // tm
    tok3 = row_tok.reshape(nblk, 1, tm)

    def w_col(i, j, be, nu):
        return jnp.where(i < nu[0], j, 0)

    grid_spec = pltpu.PrefetchScalarGridSpec(
        num_scalar_prefetch=2,
        grid=(nblk, nj),
        in_specs=[pl.BlockSpec((None, 1, tm), lambda i, j, be, nu: (i, 0, 0), memory_space=pltpu.SMEM),
                  pl.BlockSpec((None, 1, tm), lambda i, j, be, nu: (jnp.minimum(i + 1, nblk - 1), 0, 0),
                               memory_space=pltpu.SMEM),
                  pl.BlockSpec(memory_space=pl.ANY),
                  pl.BlockSpec((None, d, tf), lambda i, j, be, nu: (be[i], 0, w_col(i, j, be, nu))),
                  pl.BlockSpec((None, d, tf), lambda i, j, be, nu: (be[i], 0, w_col(i, j, be, nu))),
                  pl.BlockSpec((None, tf, d), lambda i, j, be, nu: (be[i], w_col(i, j, be, nu), 0))],
        out_specs=pl.BlockSpec((tm, d), lambda i, j, be, nu: (i, 0)),
        scratch_shapes=[pltpu.VMEM((tm, d), F32), pltpu.VMEM((tm, d), BF16),
                        pltpu.VMEM((tm, d), F32), pltpu.SemaphoreType.DMA(())],
    )
    return pl.pallas_call(
        _moe_ffn_kernel,
        grid_spec=grid_spec,
        out_shape=jax.ShapeDtypeStruct((r, d), F32),
        compiler_params=_cparams(("arbitrary", "arbitrary")),
        name="moe_ffn",
    )(block_e, n_used, tok3, tok3, h, wg, wu, wd)


def _router_kernel(h_ref, w_ref, b_ref, o_ref):
    logits = jnp.dot(h_ref[...], w_ref[...], preferred_element_type=F32,
                     precision=lax.Precision.HIGHEST) + b_ref[...]
    lane = lax.broadcasted_iota(jnp.int32, logits.shape, 1)
    logits = jnp.where(lane < N_EXPERTS, logits, NEG_INF)
    m1 = jnp.max(logits, axis=1, keepdims=True)
    i1 = jnp.min(jnp.where(logits == m1, lane, LANES), axis=1, keepdims=True)
    rest = jnp.where(lane == i1, NEG_INF, logits)
    m2 = jnp.max(rest, axis=1, keepdims=True)
    i2 = jnp.min(jnp.where(rest == m2, lane, LANES), axis=1, keepdims=True)
    e2 = jnp.exp(m2 - m1)
    g1 = 1.0 / (1.0 + e2)
    g2 = e2 / (1.0 + e2)
    out = jnp.where(lane == 0, i1.astype(F32),
                    jnp.where(lane == 1, i2.astype(F32),
                              jnp.where(lane == 2, g1, jnp.where(lane == 3, g2, 0.0))))
    o_ref[...] = out


def router(h, w_pad, b_pad):
    m, d = h.shape
    tm = 512
    return pl.pallas_call(
        _router_kernel,
        grid=(m // tm,),
        in_specs=[pl.BlockSpec((tm, d), lambda i: (i, 0)),
                  pl.BlockSpec((d, LANES), lambda i: (0, 0)),
                  pl.BlockSpec((1, LANES), lambda i: (0, 0))],
        out_specs=pl.BlockSpec((tm, LANES), lambda i: (i, 0)),
        out_shape=jax.ShapeDtypeStruct((m, LANES), F32),
        compiler_params=_cparams(("parallel",)),
        name="router",
    )(h, w_pad, b_pad)


def _alibi_slopes(n_heads):
    return jnp.asarray(2.0 ** (-8.0 * np.arange(1, n_heads + 1) / n_heads), dtype=F32)


def _bf16_parts(v):
    a = v.astype(BF16).astype(F32)
    b = (v - a).astype(BF16).astype(F32)
    c = (v - a - b).astype(BF16).astype(F32)
    return a, b, c


def _alibi_tables(seq, slopes2):
    nh = slopes2.shape[0]
    pos = jnp.arange(seq, dtype=jnp.int32)
    pos_parts = (((pos // LANES) * LANES).astype(F32), (pos % LANES).astype(F32))
    cols_q, cols_k = [], []
    for sp in _bf16_parts(slopes2):
        sp_col = jnp.broadcast_to(sp[:, None], (nh, seq))
        for pp in pos_parts:
            pp_row = jnp.broadcast_to(pp[None, :], (nh, seq))
            cols_q += [pp_row, sp_col]
            cols_k += [sp_col, -pp_row]
    pad = jnp.zeros((nh, seq, LANES - len(cols_q)), F32)
    eq = jnp.concatenate([jnp.stack(cols_q, axis=-1), pad], axis=-1).astype(BF16)
    ek = jnp.concatenate([jnp.stack(cols_k, axis=-1), pad], axis=-1).astype(BF16)
    return eq, ek


def _rope_tables(seq):
    inv_freq = ROPE_THETA ** (-jnp.arange(0, B_ROPE, 2, dtype=F32) / B_ROPE)
    ang = jnp.arange(seq, dtype=F32)[:, None] * inv_freq[None, :]
    cos, sin = jnp.cos(ang), jnp.sin(ang)
    cos2 = jnp.concatenate([cos, cos], axis=-1)
    sin2 = jnp.concatenate([sin, sin], axis=-1)
    z64 = jnp.zeros((seq, B_ROPE), F32)
    ka = jnp.concatenate([cos2, z64], axis=-1)
    kb = jnp.concatenate([sin2, z64], axis=-1)
    ones = jnp.ones((seq, B_NOPE), F32)
    z128 = jnp.zeros((seq, B_NOPE), F32)
    qa = jnp.tile(jnp.concatenate([ones, cos2, z64], axis=-1), (1, B_HEADS))
    qb = jnp.tile(jnp.concatenate([z128, sin2, z64], axis=-1), (1, B_HEADS))
    return ka, kb, qa, qb


def _rot_cols(w):
    half = w.shape[-1] // 2
    return jnp.concatenate([-w[..., half:], w[..., :half]], axis=-1)


def _routing_tables(flat_e):
    n_assign = flat_e.shape[0]
    onehot = (flat_e[:, None] == jnp.arange(N_EXPERTS, dtype=jnp.int32)[None, :]).astype(jnp.int32)
    csum = jnp.cumsum(onehot, axis=0)
    rank = jnp.sum((csum - onehot) * onehot, axis=1)
    sizes = csum[-1]
    padded = ((sizes + MOE_ROWS - 1) // MOE_ROWS) * MOE_ROWS
    pad_end = jnp.cumsum(padded)
    pad_start = pad_end - padded
    dest = (pad_start[flat_e] + rank).astype(jnp.int32)
    n_blocks = -(-n_assign // MOE_ROWS) + N_EXPERTS
    n_rows = n_blocks * MOE_ROWS
    tok = jnp.arange(n_assign, dtype=jnp.int32) // TOP_K
    row_tok = jnp.zeros((n_rows,), jnp.int32).at[dest].set(tok)
    block_start = jnp.arange(n_blocks, dtype=jnp.int32) * MOE_ROWS
    block_e = jnp.minimum(jnp.searchsorted(pad_end, block_start, side='right'), N_EXPERTS - 1).astype(jnp.int32)
    n_used = (pad_end[-1:] // MOE_ROWS).astype(jnp.int32)
    return dest, row_tok, block_e, n_used


def kernel(x, c, ada_w, ada_b, ln_g, ln_b, ab_w_in, ab_lam, ab_diff_g, ab_q_norm_g, ab_kv_norm_g, ab_w_uq,
           ab_w_ukv, ab_w_out, c_w_in, c_sink, c_w_out, ffn_w_gate, ffn_w_up, ffn_w_down, moe_w_router,
           moe_b_router, moe_w_gate, moe_w_up, moe_w_down):
    bsz, seq, d = x.shape
    depth = ada_w.shape[0]
    n_tok = bsz * seq
    alpha = (2 * depth) ** 0.25

    c_pad = jnp.zeros((8, d), F32).at[:bsz].set(c)
    mods = adaln_all(c_pad, ada_w.reshape(2 * depth, d, 3 * d), ada_b.reshape(2 * depth, 1, 3 * d))
    mods = mods.reshape(2 * depth, 8, 3, d)

    ka, kb, qa, qb = _rope_tables(seq)
    slopes_a2 = _alibi_slopes(A_HEADS) * LOG2E
    eq_a, ek_a = _alibi_tables(seq, slopes_a2)
    slopes_c = _alibi_slopes(C_HEADS)
    scale_a = A_QK_DIM ** -0.5 * LOG2E
    scale_b = (B_NOPE + B_ROPE) ** -0.5 * LOG2E
    scale_c = C_HEAD_DIM ** -0.5

    h = modulate(x, mods, 0, BF16)
    for layer in range(depth):
        i = layer // 2
        l2 = 2 * layer
        last = layer == depth - 1
        if layer % 2 == 0:
            lam_init = 0.8 - 0.6 * math.exp(-0.3 * layer)
            w_in = ab_w_in[i]
            w_kr = w_in[:, 3 * A_COLS + B_Q_LORA + B_KV_LORA:]
            w_in2 = jnp.concatenate([w_in[:, :A_COLS] * scale_a, w_in[:, A_COLS:], _rot_cols(w_kr)],
                                    axis=1).astype(BF16)
            p_ab = inproj_ab(h.reshape(n_tok, d), w_in2, ka, kb, seq)
            o_a = flash_diff(p_ab.reshape(bsz, seq, -1), eq_a, ek_a, slopes_a2, ab_lam[i],
                             ab_diff_g[i].reshape(1, -1), lam_init, bsz, seq)
            w_uq = ab_w_uq[i].reshape(B_Q_LORA, B_HEADS, B_NOPE + B_ROPE) * scale_b
            w_uq2 = jnp.concatenate([w_uq, _rot_cols(w_uq[..., B_NOPE:])], axis=-1)
            w_uq2 = w_uq2.reshape(B_Q_LORA, B_HEADS * 2 * LANES).astype(BF16)
            q_cat = uq_proj(p_ab, 3 * A_COLS // B_Q_LORA, ab_q_norm_g[i].reshape(1, -1), w_uq2, qa, qb, seq)
            k_cat, v_b = ukv_proj(p_ab, (3 * A_COLS + B_Q_LORA) // B_KV_LORA,
                                  (3 * A_COLS + B_Q_LORA + B_KV_LORA) // LANES,
                                  ab_kv_norm_g[i].reshape(1, -1), ab_w_ukv[i].astype(BF16))
            o_b = flash_mla(q_cat.reshape(bsz, seq, -1), k_cat.reshape(bsz, seq, -1),
                            v_b.reshape(bsz, seq, -1), bsz, seq)
            y = matmul2(o_a.reshape(n_tok, -1), o_b.reshape(n_tok, -1), ab_w_out[i].astype(BF16), F32)
        else:
            w_in = c_w_in[i]
            nqc = C_HEADS * C_HEAD_DIM
            w_in2 = jnp.concatenate([w_in[:, :nqc] * scale_c, w_in[:, nqc:]], axis=1).astype(BF16)
            p_c = matmul(h.reshape(n_tok, d), w_in2, BF16, tn=1024)
            o_c = window_attn(p_c.reshape(bsz, seq, -1), slopes_c, c_sink[i].astype(F32), bsz, seq)
            y = matmul(o_c.reshape(n_tok, -1), c_w_out[i].astype(BF16), F32)
        moe_next = layer % 2 == 1
        x, h = postnorm(x, y.reshape(bsz, seq, d), mods, l2, ln_g[layer, 0], ln_b[layer, 0], alpha,
                        next_dtype=F32 if moe_next else BF16)

        if not moe_next:
            y = ffn(h.reshape(n_tok, d), ffn_w_gate[i].astype(BF16), ffn_w_up[i].astype(BF16),
                    ffn_w_down[i].astype(BF16)).reshape(bsz, seq, d)
            route = dest = None
        else:
            hf = h.reshape(n_tok, d)
            w_r = jnp.zeros((d, LANES), F32).at[:, :N_EXPERTS].set(moe_w_router[i])
            b_r = jnp.zeros((1, LANES), F32).at[0, :N_EXPERTS].set(moe_b_router[i])
            route = router(hf, w_r, b_r)
            flat_e = route[:, :TOP_K].astype(jnp.int32).reshape(-1)
            dest, row_tok, block_e, n_used = _routing_tables(flat_e)
            y = moe_ffn(block_e, n_used, row_tok, hf, moe_w_gate[i].astype(BF16), moe_w_up[i].astype(BF16),
                        moe_w_down[i].astype(BF16))
            route = route.reshape(bsz, seq, LANES)
        x, h = postnorm(x, y, mods, l2 + 1, ln_g[layer, 1], ln_b[layer, 1], alpha,
                        route=route, dest=dest, next_dtype=None if last else BF16)
    return x
```

```python
import functools
import math

import numpy as np
import jax
import jax.numpy as jnp
from jax import lax
from jax.experimental import pallas as pl
from jax.experimental.pallas import tpu as pltpu

F32 = jnp.float32
BF16 = jnp.bfloat16

A_HEADS = 4
A_QK_DIM = 64
A_V_DIM = 128
B_HEADS = 4
B_Q_LORA = 512
B_KV_LORA = 256
B_NOPE = 128
B_ROPE = 64
B_V_DIM = 128
ROPE_THETA = 10000.0
C_HEADS = 16
C_KV_HEADS = 4
C_GROUP = C_HEADS // C_KV_HEADS
C_HEAD_DIM = 128
WINDOW = 128
N_EXPERTS = 8
TOP_K = 2
MOE_ROWS = 1024
LN_EPS = 1e-5
RMS_EPS = 1e-6

A_COLS = A_HEADS * 2 * A_QK_DIM
LANES = 128
VMEM_LIMIT = 56 * 1024 * 1024
NEG_INF = float("-inf")
LOG2E = math.log2(math.e)


def _cparams(sem, vmem=VMEM_LIMIT):
    return pltpu.CompilerParams(dimension_semantics=sem, vmem_limit_bytes=vmem)


def _adaln_kernel(c_ref, w_ref, b_ref, o_ref):
    c = c_ref[...]
    ca = c * jax.nn.sigmoid(c)
    o_ref[...] = jnp.dot(ca, w_ref[...], preferred_element_type=F32,
                         precision=lax.Precision.HIGHEST) + b_ref[...]


def adaln_all(c_pad, ada_w, ada_b):
    l2, d, n = ada_w.shape
    tn = 768
    return pl.pallas_call(
        _adaln_kernel,
        grid=(l2, n // tn),
        in_specs=[pl.BlockSpec((8, d), lambda l, j: (0, 0)),
                  pl.BlockSpec((None, d, tn), lambda l, j: (l, 0, j)),
                  pl.BlockSpec((None, 1, tn), lambda l, j: (l, 0, j))],
        out_specs=pl.BlockSpec((None, 8, tn), lambda l, j: (l, 0, j)),
        out_shape=jax.ShapeDtypeStruct((l2, 8, n), F32),
        compiler_params=_cparams(("parallel", "parallel")),
        name="adaln",
    )(c_pad, ada_w, ada_b)


def _modulate_kernel(x_ref, mod_ref, h_ref):
    x = x_ref[...]
    h_ref[...] = (x * (1.0 + mod_ref[1:2, :]) + mod_ref[0:1, :]).astype(h_ref.dtype)


def modulate(x, mods, l2, out_dtype):
    b, s, d = x.shape
    ts = 512
    return pl.pallas_call(
        _modulate_kernel,
        grid=(b, s // ts),
        in_specs=[pl.BlockSpec((None, ts, d), lambda i, j: (i, j, 0)),
                  pl.BlockSpec((None, None, 3, d), lambda i, j: (l2, i, 0, 0))],
        out_specs=pl.BlockSpec((None, ts, d), lambda i, j: (i, j, 0)),
        out_shape=jax.ShapeDtypeStruct((b, s, d), out_dtype),
        compiler_params=_cparams(("parallel", "parallel")),
        name="modulate",
    )(x, mods)


def _residual_norm(x, y, modc_ref, g_ref, b_ref, modn_ref, xo_ref, h_ref, alpha):
    z = alpha * x + (1.0 + modc_ref[2:3, :]) * y
    mu = jnp.mean(z, axis=-1, keepdims=True)
    zc = z - mu
    var = jnp.mean(zc * zc, axis=-1, keepdims=True)
    xn = zc * lax.rsqrt(var + LN_EPS) * g_ref[...] + b_ref[...]
    xo_ref[...] = xn
    if h_ref is not None:
        h_ref[...] = (xn * (1.0 + modn_ref[1:2, :]) + modn_ref[0:1, :]).astype(h_ref.dtype)


def _postnorm_kernel(*refs, alpha, routed, has_next):
    refs = list(refs)
    x_ref, y_ref, modc_ref, g_ref, b_ref = refs[:5]
    rest = refs[5:]
    if routed:
        route_ref, idx_ref, idxn_ref = rest[:3]
        rest = rest[3:]
    modn_ref = rest.pop(0) if has_next else None
    xo_ref = rest.pop(0)
    h_ref = rest.pop(0) if has_next else None
    x = x_ref[...]
    ts = x.shape[0]
    if routed:
        yg_ref, sem = rest
        step = pl.program_id(0) * pl.num_programs(1) + pl.program_id(1)
        n_steps = pl.num_programs(0) * pl.num_programs(1)
        slot = step % 2

        def issue_gather(idx, dst_slot):
            def body(r, carry):
                pltpu.make_async_copy(y_ref.at[pl.ds(idx[0, r], 1)], yg_ref.at[dst_slot, pl.ds(r, 1)],
                                      sem.at[dst_slot]).start()
                return carry
            lax.fori_loop(0, TOP_K * ts, body, 0, unroll=8)

        @pl.when(step == 0)
        def _():
            issue_gather(idx_ref, 0)

        @pl.when(step + 1 < n_steps)
        def _():
            issue_gather(idxn_ref, 1 - slot)

        pltpu.make_async_copy(y_ref.at[pl.ds(0, TOP_K * ts)], yg_ref.at[slot], sem.at[slot]).wait()
        g1 = route_ref[:, TOP_K:TOP_K + 1]
        g2 = route_ref[:, TOP_K + 1:TOP_K + 2]
        y = g1 * yg_ref[slot, 0:ts, :] + g2 * yg_ref[slot, ts:2 * ts, :]
    else:
        y = y_ref[...].astype(F32)
    _residual_norm(x, y, modc_ref, g_ref, b_ref, modn_ref, xo_ref, h_ref, alpha)


def postnorm(x, y, mods, l2, ln_g, ln_b, alpha, *, route=None, dest=None, next_dtype=None):
    b, s, d = x.shape
    ts = 256
    has_next = next_dtype is not None
    routed = route is not None
    nj = s // ts
    if routed:
        y_spec = pl.BlockSpec(memory_space=pl.ANY)
    else:
        y_spec = pl.BlockSpec((None, ts, d), lambda i, j: (i, j, 0))
    in_specs = [pl.BlockSpec((None, ts, d), lambda i, j: (i, j, 0)),
                y_spec,
                pl.BlockSpec((None, None, 3, d), lambda i, j: (l2, i, 0, 0)),
                pl.BlockSpec((1, d), lambda i, j: (0, 0)),
                pl.BlockSpec((1, d), lambda i, j: (0, 0))]
    args = [x, y, mods, ln_g.reshape(1, d), ln_b.reshape(1, d)]
    scratch = []
    if routed:
        n_steps = b * nj
        idx = dest.reshape(n_steps, ts, TOP_K).transpose(0, 2, 1).reshape(n_steps, 1, TOP_K * ts)
        in_specs += [pl.BlockSpec((None, ts, LANES), lambda i, j: (i, j, 0)),
                     pl.BlockSpec((None, 1, TOP_K * ts), lambda i, j: (i * nj + j, 0, 0),
                                  memory_space=pltpu.SMEM),
                     pl.BlockSpec((None, 1, TOP_K * ts),
                                  lambda i, j: (jnp.minimum(i * nj + j + 1, n_steps - 1), 0, 0),
                                  memory_space=pltpu.SMEM)]
        args += [route, idx, idx]
        scratch = [pltpu.VMEM((2, TOP_K * ts, d), F32), pltpu.SemaphoreType.DMA((2,))]
    out_specs = [pl.BlockSpec((None, ts, d), lambda i, j: (i, j, 0))]
    out_shape = [jax.ShapeDtypeStruct((b, s, d), F32)]
    if has_next:
        in_specs.append(pl.BlockSpec((None, None, 3, d), lambda i, j: (l2 + 1, i, 0, 0)))
        args.append(mods)
        out_specs.append(pl.BlockSpec((None, ts, d), lambda i, j: (i, j, 0)))
        out_shape.append(jax.ShapeDtypeStruct((b, s, d), next_dtype))
    sem = ("arbitrary", "arbitrary") if routed else ("parallel", "parallel")
    out = pl.pallas_call(
        functools.partial(_postnorm_kernel, alpha=alpha, routed=routed, has_next=has_next),
        grid=(b, nj),
        in_specs=in_specs,
        out_specs=out_specs,
        out_shape=out_shape,
        scratch_shapes=scratch,
        compiler_params=_cparams(sem),
        name="postnorm",
    )(*args)
    return (out[0], out[1]) if has_next else (out[0], None)


def _rms_rows(a_ref, g_ref):
    a = a_ref[...].astype(F32)
    return (a * lax.rsqrt(jnp.mean(a * a, axis=-1, keepdims=True) + RMS_EPS) * g_ref[...]).astype(BF16)


def _matmul_kernel(a_ref, w_ref, o_ref):
    o_ref[...] = jnp.dot(a_ref[...], w_ref[...], preferred_element_type=F32).astype(o_ref.dtype)


def matmul(a, w, out_dtype, tm=512, tn=None):
    m, k = a.shape
    n = w.shape[1]
    tn = n if tn is None else tn
    return pl.pallas_call(
        _matmul_kernel,
        grid=(m // tm, n // tn),
        in_specs=[pl.BlockSpec((tm, k), lambda i, j: (i, 0)),
                  pl.BlockSpec((k, tn), lambda i, j: (0, j))],
        out_specs=pl.BlockSpec((tm, tn), lambda i, j: (i, j)),
        out_shape=jax.ShapeDtypeStruct((m, n), out_dtype),
        compiler_params=_cparams(("parallel", "parallel")),
        name="matmul",
    )(a, w)


def _outproj_norm_kernel(*refs, n_in, alpha):
    a_refs, w_refs = refs[:n_in], refs[n_in:2 * n_in]
    x_ref, modc_ref, g_ref, b_ref, modn_ref, xo_ref, h_ref = refs[2 * n_in:]
    y = jnp.dot(a_refs[0][...], w_refs[0][...], preferred_element_type=F32)
    for a_ref, w_ref in zip(a_refs[1:], w_refs[1:]):
        y += jnp.dot(a_ref[...], w_ref[...], preferred_element_type=F32)
    _residual_norm(x_ref[...], y, modc_ref, g_ref, b_ref, modn_ref, xo_ref, h_ref, alpha)


def outproj_norm(acts, w, x, mods, l2, ln_g, ln_b, alpha, next_dtype):
    b, s, d = x.shape
    k = acts[0].shape[-1]
    assert all(a.shape[-1] == k for a in acts) and w.shape[0] == k * len(acts)
    ts = 512
    n_in = len(acts)
    row = lambda i, j: (i, j, 0)
    in_specs = ([pl.BlockSpec((None, ts, k), row) for _ in acts]
                + [pl.BlockSpec((k, d), functools.partial(lambda p, i, j: (p, 0), p)) for p in range(n_in)]
                + [pl.BlockSpec((None, ts, d), row),
                   pl.BlockSpec((None, None, 3, d), lambda i, j: (l2, i, 0, 0)),
                   pl.BlockSpec((1, d), lambda i, j: (0, 0)),
                   pl.BlockSpec((1, d), lambda i, j: (0, 0)),
                   pl.BlockSpec((None, None, 3, d), lambda i, j: (l2 + 1, i, 0, 0))])
    out = pl.pallas_call(
        functools.partial(_outproj_norm_kernel, n_in=n_in, alpha=alpha),
        grid=(b, s // ts),
        in_specs=in_specs,
        out_specs=[pl.BlockSpec((None, ts, d), row), pl.BlockSpec((None, ts, d), row)],
        out_shape=[jax.ShapeDtypeStruct((b, s, d), F32), jax.ShapeDtypeStruct((b, s, d), next_dtype)],
        compiler_params=_cparams(("parallel", "parallel")),
        name="outproj_norm",
    )(*acts, *([w] * n_in), x, mods, ln_g.reshape(1, d), ln_b.reshape(1, d), mods)
    return out[0], out[1]


def _inproj_ab_kernel(a_ref, w_ref, ra_ref, rb_ref, o_ref):
    acc = jnp.dot(a_ref[...], w_ref[...], preferred_element_type=F32)
    n = acc.shape[1]
    o_ref[:, :n - LANES] = acc[:, :n - LANES].astype(o_ref.dtype)
    last = acc[:, n - LANES:]
    roped = last * ra_ref[...] + pltpu.roll(last, LANES // 2, 1) * rb_ref[...]
    o_ref[:, n - LANES:] = roped.astype(o_ref.dtype)


def inproj_ab(h, w, rope_a, rope_b, seq):
    m, k = h.shape
    n = w.shape[1]
    tm = 512
    nsb = seq // tm
    return pl.pallas_call(
        _inproj_ab_kernel,
        grid=(m // tm,),
        in_specs=[pl.BlockSpec((tm, k), lambda i: (i, 0)),
                  pl.BlockSpec((k, n), lambda i: (0, 0)),
                  pl.BlockSpec((tm, LANES), lambda i: (i % nsb, 0)),
                  pl.BlockSpec((tm, LANES), lambda i: (i % nsb, 0))],
        out_specs=pl.BlockSpec((tm, n), lambda i: (i, 0)),
        out_shape=jax.ShapeDtypeStruct((m, n), BF16),
        compiler_params=_cparams(("parallel",)),
        name="inproj_ab",
    )(h, w, rope_a, rope_b)


def _uq_kernel(a_ref, g_ref, w_ref, ra_ref, rb_ref, o_ref):
    an = _rms_rows(a_ref, g_ref)
    acc = jnp.dot(an, w_ref[...], preferred_element_type=F32)
    n = acc.shape[1]
    o_ref[...] = (acc * ra_ref[...] + pltpu.roll(acc, n - B_ROPE, 1) * rb_ref[...]).astype(o_ref.dtype)


def uq_proj(p_ab, col_block, g, w, rope_a, rope_b, seq):
    m = p_ab.shape[0]
    k, n = w.shape
    tm = 512
    nsb = seq // tm
    return pl.pallas_call(
        _uq_kernel,
        grid=(m // tm,),
        in_specs=[pl.BlockSpec((tm, k), lambda i: (i, col_block)),
                  pl.BlockSpec((1, k), lambda i: (0, 0)),
                  pl.BlockSpec((k, n), lambda i: (0, 0)),
                  pl.BlockSpec((tm, n), lambda i: (i % nsb, 0)),
                  pl.BlockSpec((tm, n), lambda i: (i % nsb, 0))],
        out_specs=pl.BlockSpec((tm, n), lambda i: (i, 0)),
        out_shape=jax.ShapeDtypeStruct((m, n), BF16),
        compiler_params=_cparams(("parallel",)),
        name="uq_proj",
    )(p_ab, g, w, rope_a, rope_b)


def _ukv_kernel(a_ref, g_ref, w_ref, kr_ref, k_ref, v_ref):
    an = _rms_rows(a_ref, g_ref)
    acc = jnp.dot(an, w_ref[...], preferred_element_type=F32)
    kr = kr_ref[...]
    for h in range(B_HEADS):
        base = h * (B_NOPE + B_V_DIM)
        k_ref[:, 2 * h * LANES:(2 * h + 1) * LANES] = acc[:, base:base + B_NOPE].astype(k_ref.dtype)
        k_ref[:, (2 * h + 1) * LANES:(2 * h + 2) * LANES] = kr
        v_ref[:, h * B_V_DIM:(h + 1) * B_V_DIM] = acc[:, base + B_NOPE:base + B_NOPE + B_V_DIM].astype(v_ref.dtype)


def ukv_proj(p_ab, ckv_block, kr_block, g, w):
    m = p_ab.shape[0]
    k, n = w.shape
    tm = 512
    return pl.pallas_call(
        _ukv_kernel,
        grid=(m // tm,),
        in_specs=[pl.BlockSpec((tm, k), lambda i: (i, ckv_block)),
                  pl.BlockSpec((1, k), lambda i: (0, 0)),
                  pl.BlockSpec((k, n), lambda i: (0, 0)),
                  pl.BlockSpec((tm, LANES), lambda i: (i, kr_block))],
        out_specs=[pl.BlockSpec((tm, B_HEADS * 2 * LANES), lambda i: (i, 0)),
                   pl.BlockSpec((tm, B_HEADS * B_V_DIM), lambda i: (i, 0))],
        out_shape=[jax.ShapeDtypeStruct((m, B_HEADS * 2 * LANES), BF16),
                   jax.ShapeDtypeStruct((m, B_HEADS * B_V_DIM), BF16)],
        compiler_params=_cparams(("parallel",)),
        name="ukv_proj",
    )(p_ab, g, w, p_ab)


_NT = (((1,), (1,)), ((), ()))


def _ones_block(rows):
    return jnp.ones((rows, LANES), BF16)


def _online_softmax_step(s, v_aug, m_ref, acc_ref):
    tk = s.shape[1]
    m_prev = m_ref[...]
    m_new = jnp.maximum(m_prev, jnp.max(s, axis=1, keepdims=True))
    alpha = jnp.exp2(m_prev - m_new)
    p = jnp.exp2(s - jnp.concatenate([m_new] * (tk // LANES), axis=1))
    pv = jnp.dot(p.astype(v_aug.dtype), v_aug, preferred_element_type=F32)
    acc_ref[...] = jnp.concatenate([alpha, alpha], axis=1) * acc_ref[...] + pv
    m_ref[...] = m_new


def _softmax_result(acc_ref):
    acc = acc_ref[...]
    return acc[:, :LANES] / acc[:, LANES:]


def _flash_diff_kernel(slopes_ref, q_ref, eq_ref, k_ref, ek_ref, v_ref, lam_ref, g_ref, o_ref,
                       qaug_sc, m_sc, acc_sc, *, tk, lam_init, splits):
    tq = q_ref.shape[0]
    seq = k_ref.shape[0]
    qi = pl.program_id(2)
    q0 = qi * tq
    q = q_ref[...]
    eq = eq_ref[...]
    lane = lax.broadcasted_iota(jnp.int32, q.shape, 1)
    zero = jnp.zeros_like(q)
    for mp in range(2):
        qm = jnp.where((lane < A_QK_DIM) == (mp == 0), q, zero)
        qaug_sc[0, mp] = jnp.concatenate([qm, -eq], axis=1)
        qaug_sc[1, mp] = jnp.concatenate([qm, eq], axis=1)
    m_sc[...] = jnp.full(m_sc.shape, NEG_INF, F32)
    acc_sc[...] = jnp.zeros(acc_sc.shape, F32)
    ones_col = _ones_block(tk)

    def chunk(c, side, diagonal):
        k0 = pl.multiple_of(c * tk, tk)
        kc = jnp.concatenate([k_ref[pl.ds(k0, tk), :], ek_ref[pl.ds(k0, tk), :]], axis=1)
        v_aug = jnp.concatenate([v_ref[pl.ds(k0, tk), :], ones_col], axis=1)
        if diagonal:
            rel = (lax.broadcasted_iota(jnp.int32, (tq, tk), 0)
                   - lax.broadcasted_iota(jnp.int32, (tq, tk), 1)) + (q0 - k0)
            corr = (2.0 * slopes_ref[pl.program_id(1)]) * jnp.maximum(rel, 0).astype(F32)
        part = tq // splits
        for hh in range(splits):
            rows = pl.ds(hh * part, part)
            for mp in range(2):
                s = lax.dot_general(qaug_sc[side, mp, rows, :], kc, _NT, preferred_element_type=F32)
                if diagonal:
                    s = s - corr[hh * part:(hh + 1) * part]
                _online_softmax_step(s, v_aug, m_sc.at[mp, rows], acc_sc.at[mp, rows])

    def left(c, carry):
        chunk(c, 0, False)
        return carry

    def right(c, carry):
        chunk(c, 1, False)
        return carry

    cd = q0 // tk
    lax.fori_loop(0, cd, left, 0)
    chunk(cd, 1, True)
    lax.fori_loop(cd + 1, seq // tk, right, 0)

    o0 = _softmax_result(acc_sc.at[0])
    o1 = _softmax_result(acc_sc.at[1])
    lam = lam_ref[...]
    lam_full = (jnp.exp(jnp.sum(lam[0:1] * lam[1:2], axis=1, keepdims=True))
                - jnp.exp(jnp.sum(lam[2:3] * lam[3:4], axis=1, keepdims=True)) + lam_init)
    o = o0 - lam_full * o1
    on = o * lax.rsqrt(jnp.mean(o * o, axis=-1, keepdims=True) + RMS_EPS) * g_ref[...]
    o_ref[...] = (on * (1.0 - lam_init)).astype(o_ref.dtype)


def flash_diff(p_ab, eq, ek, slopes2, lam, diff_g, lam_init, bsz, seq):
    tq, tk = 1024, 1024
    assert tk % tq == 0
    nh = A_HEADS
    return pl.pallas_call(
        functools.partial(_flash_diff_kernel, tk=tk, lam_init=lam_init, splits=4),
        grid=(bsz, nh, seq // tq),
        in_specs=[pl.BlockSpec(memory_space=pltpu.SMEM),
                  pl.BlockSpec((None, tq, LANES), lambda b, h, i: (b, i, h)),
                  pl.BlockSpec((None, tq, LANES), lambda b, h, i: (h, i, 0)),
                  pl.BlockSpec((None, seq, LANES), lambda b, h, i: (b, 0, nh + h)),
                  pl.BlockSpec((None, seq, LANES), lambda b, h, i: (h, 0, 0)),
                  pl.BlockSpec((None, seq, LANES), lambda b, h, i: (b, 0, 2 * nh + h)),
                  pl.BlockSpec((4, A_QK_DIM), lambda b, h, i: (0, 0)),
                  pl.BlockSpec((1, A_V_DIM), lambda b, h, i: (0, 0))],
        out_specs=pl.BlockSpec((None, tq, A_V_DIM), lambda b, h, i: (b, i, h)),
        out_shape=jax.ShapeDtypeStruct((bsz, seq, nh * A_V_DIM), BF16),
        scratch_shapes=[pltpu.VMEM((2, 2, tq, 2 * LANES), BF16),
                        pltpu.VMEM((2, tq, LANES), F32),
                        pltpu.VMEM((2, tq, 2 * LANES), F32)],
        compiler_params=_cparams(("parallel", "parallel", "parallel")),
        name="flash_diff",
    )(slopes2, p_ab, eq, p_ab, ek, p_ab, lam, diff_g)


def _flash_mla_kernel(q_ref, k_ref, v_ref, o_ref, m_sc, acc_sc, *, tk, splits):
    seq = k_ref.shape[0]
    part = q_ref.shape[0] // splits
    m_sc[...] = jnp.full(m_sc.shape, NEG_INF, F32)
    acc_sc[...] = jnp.zeros(acc_sc.shape, F32)
    ones_col = _ones_block(tk)

    def body(c, carry):
        k0 = pl.multiple_of(c * tk, tk)
        kc = k_ref[pl.ds(k0, tk), :]
        v_aug = jnp.concatenate([v_ref[pl.ds(k0, tk), :], ones_col], axis=1)
        for hh in range(splits):
            rows = pl.ds(hh * part, part)
            s = lax.dot_general(q_ref[rows, :], kc, _NT, preferred_element_type=F32)
            _online_softmax_step(s, v_aug, m_sc.at[rows], acc_sc.at[rows])
        return carry

    lax.fori_loop(0, seq // tk, body, 0, unroll=2)
    o_ref[...] = _softmax_result(acc_sc).astype(o_ref.dtype)


def flash_mla(q_cat, k_cat, v_b, bsz, seq):
    tq, tk = 1024, 1024
    kd = 2 * LANES
    return pl.pallas_call(
        functools.partial(_flash_mla_kernel, tk=tk, splits=4),
        grid=(bsz, B_HEADS, seq // tq),
        in_specs=[pl.BlockSpec((None, tq, kd), lambda b, h, i: (b, i, h)),
                  pl.BlockSpec((None, seq, kd), lambda b, h, i: (b, 0, h)),
                  pl.BlockSpec((None, seq, B_V_DIM), lambda b, h, i: (b, 0, h))],
        out_specs=pl.BlockSpec((None, tq, B_V_DIM), lambda b, h, i: (b, i, h)),
        out_shape=jax.ShapeDtypeStruct((bsz, seq, B_HEADS * B_V_DIM), BF16),
        scratch_shapes=[pltpu.VMEM((tq, LANES), F32), pltpu.VMEM((tq, 2 * LANES), F32)],
        compiler_params=_cparams(("parallel", "parallel", "parallel")),
        name="flash_mla",
    )(q_cat, k_cat, v_b)


def _window_kernel(slopes_ref, sink_ref, q_ref, k_ref, v_ref, o_ref):
    tq = q_ref.shape[0]
    seq = k_ref.shape[0]
    blk = WINDOW
    kw = 3 * blk
    n = pl.program_id(1)
    q0 = pl.program_id(2) * tq
    rel = (lax.broadcasted_iota(jnp.int32, (blk, kw), 1) - lax.broadcasted_iota(jnp.int32, (blk, kw), 0))
    ones_blk = _ones_block(kw)
    heads = [n * C_GROUP + g for g in range(C_GROUP)]
    slope_rows = jnp.concatenate([jnp.full((blk, kw), slopes_ref[hd], F32) for hd in heads], axis=0)
    sink_rows = jnp.concatenate([jnp.full((blk, LANES), sink_ref[hd], F32) for hd in heads], axis=0)
    for sb in range(tq // blk):
        t0 = q0 + sb * blk
        start = pl.multiple_of(jnp.clip(t0 - blk, 0, seq - kw), blk)
        kc = k_ref[pl.ds(start, kw), :]
        v_aug = jnp.concatenate([v_ref[pl.ds(start, kw), :], ones_blk], axis=1)
        dist = jnp.abs(rel + (start - t0))
        dist_rows = jnp.concatenate([dist] * C_GROUP, axis=0)
        qs = jnp.concatenate([q_ref[sb * blk:(sb + 1) * blk, g * C_HEAD_DIM:(g + 1) * C_HEAD_DIM]
                              for g in range(C_GROUP)], axis=0)
        s = lax.dot_general(qs, kc, _NT, preferred_element_type=F32) - slope_rows * dist_rows.astype(F32)
        s = jnp.where(dist_rows <= WINDOW, s, NEG_INF)
        m = jnp.maximum(jnp.max(s, axis=1, keepdims=True), sink_rows)
        e = jnp.exp(s - jnp.concatenate([m] * (kw // LANES), axis=1))
        pv = jnp.dot(e.astype(v_aug.dtype), v_aug, preferred_element_type=F32)
        o = pv[:, :C_HEAD_DIM] / (pv[:, C_HEAD_DIM:] + jnp.exp(sink_rows - m))
        for g in range(C_GROUP):
            o_ref[sb * blk:(sb + 1) * blk, g * C_HEAD_DIM:(g + 1) * C_HEAD_DIM] = (
                o[g * blk:(g + 1) * blk].astype(o_ref.dtype))


def window_attn(p_c, slopes, sink, bsz, seq):
    tq = 512
    gw = C_GROUP * C_HEAD_DIM
    nq = C_HEADS * C_HEAD_DIM // LANES
    return pl.pallas_call(
        _window_kernel,
        grid=(bsz, C_KV_HEADS, seq // tq),
        in_specs=[pl.BlockSpec(memory_space=pltpu.SMEM),
                  pl.BlockSpec(memory_space=pltpu.SMEM),
                  pl.BlockSpec((None, tq, gw), lambda b, n, i: (b, i, n)),
                  pl.BlockSpec((None, seq, C_HEAD_DIM), lambda b, n, i: (b, 0, nq + n)),
                  pl.BlockSpec((None, seq, C_HEAD_DIM), lambda b, n, i: (b, 0, nq + C_KV_HEADS + n))],
        out_specs=pl.BlockSpec((None, tq, gw), lambda b, n, i: (b, i, n)),
        out_shape=jax.ShapeDtypeStruct((bsz, seq, C_HEADS * C_HEAD_DIM), BF16),
        compiler_params=_cparams(("parallel", "parallel", "parallel")),
        name="window_attn",
    )(slopes, sink, p_c, p_c, p_c)


def _swiglu_step(x, wg_ref, wu_ref, wd_ref, acc_ref):
    g = jnp.dot(x, wg_ref[...], preferred_element_type=F32)
    u = jnp.dot(x, wu_ref[...], preferred_element_type=F32)
    a = (g * jax.nn.sigmoid(g) * u).astype(BF16)
    acc_ref[...] += jnp.dot(a, wd_ref[...], preferred_element_type=F32)


def _ffn_kernel(x_ref, wg_ref, wu_ref, wd_ref, o_ref, acc_ref):
    j = pl.program_id(1)

    @pl.when(j == 0)
    def _():
        acc_ref[...] = jnp.zeros(acc_ref.shape, F32)

    _swiglu_step(x_ref[...], wg_ref, wu_ref, wd_ref, acc_ref)

    @pl.when(j == pl.num_programs(1) - 1)
    def _():
        o_ref[...] = acc_ref[...].astype(o_ref.dtype)


def ffn(h, wg, wu, wd):
    m, d = h.shape
    f = wg.shape[1]
    tm, tf = 1024, 512
    return pl.pallas_call(
        _ffn_kernel,
        grid=(m // tm, f // tf),
        in_specs=[pl.BlockSpec((tm, d), lambda i, j: (i, 0)),
                  pl.BlockSpec((d, tf), lambda i, j: (0, j)),
                  pl.BlockSpec((d, tf), lambda i, j: (0, j)),
                  pl.BlockSpec((tf, d), lambda i, j: (j, 0))],
        out_specs=pl.BlockSpec((tm, d), lambda i, j: (i, 0)),
        out_shape=jax.ShapeDtypeStruct((m, d), F32),
        scratch_shapes=[pltpu.VMEM((tm, d), F32)],
        compiler_params=_cparams(("parallel", "arbitrary")),
        name="ffn",
    )(h, wg, wu, wd)


def _moe_ffn_kernel(be_ref, nu_ref, tok_ref, tokn_ref, h_ref, wg_ref, wu_ref, wd_ref, o_ref,
                    xg_ref, xb_ref, acc_ref, sem):
    del be_ref
    i = pl.program_id(0)
    j = pl.program_id(1)
    rows = xg_ref.shape[0]
    n_used = nu_ref[0]

    def issue_gather(idx_ref):
        def body(r, carry):
            pltpu.make_async_copy(h_ref.at[pl.ds(idx_ref[0, r], 1)], xg_ref.at[pl.ds(r, 1)], sem).start()
            return carry
        lax.fori_loop(0, rows, body, 0, unroll=8)

    @pl.when((j == 0) & (i < n_used))
    def _():
        @pl.when(i == 0)
        def _():
            issue_gather(tok_ref)
        pltpu.make_async_copy(h_ref.at[pl.ds(0, rows)], xg_ref, sem).wait()
        xb_ref[...] = xg_ref[...].astype(BF16)
        acc_ref[...] = jnp.zeros(acc_ref.shape, F32)

    @pl.when((j == 1) & (i + 1 < n_used))
    def _():
        issue_gather(tokn_ref)

    @pl.when(i < n_used)
    def _():
        _swiglu_step(xb_ref[...], wg_ref, wu_ref, wd_ref, acc_ref)

    @pl.when((j == pl.num_programs(1) - 1) & (i < n_used))
    def _():
        o_ref[...] = acc_ref[...]

    @pl.when((j == pl.num_programs(1) - 1) & (i >= n_used))
    def _():
        o_ref[...] = jnp.zeros(o_ref.shape, F32)


def moe_ffn(block_e, n_used, row_tok, h, wg, wu, wd):
    d = h.shape[1]
    r = row_tok.shape[0]
    f = wg.shape[2]
    tm, tf = MOE_ROWS, 256
    nj = f // tf
    assert nj >= 2
    nblk = r // tm
    tok3 = row_tok.reshape(nblk, 1, tm)

    def w_col(i, j, be, nu):
        return jnp.where(i < nu[0], j, 0)

    grid_spec = pltpu.PrefetchScalarGridSpec(
        num_scalar_prefetch=2,
        grid=(nblk, nj),
        in_specs=[pl.BlockSpec((None, 1, tm), lambda i, j, be, nu: (i, 0, 0), memory_space=pltpu.SMEM),
                  pl.BlockSpec((None, 1, tm), lambda i, j, be, nu: (jnp.minimum(i + 1, nblk - 1), 0, 0),
                               memory_space=pltpu.SMEM),
                  pl.BlockSpec(memory_space=pl.ANY),
                  pl.BlockSpec((None, d, tf), lambda i, j, be, nu: (be[i], 0, w_col(i, j, be, nu))),
                  pl.BlockSpec((None, d, tf), lambda i, j, be, nu: (be[i], 0, w_col(i, j, be, nu))),
                  pl.BlockSpec((None, tf, d), lambda i, j, be, nu: (be[i], w_col(i, j, be, nu), 0))],
        out_specs=pl.BlockSpec((tm, d), lambda i, j, be, nu: (i, 0)),
        scratch_shapes=[pltpu.VMEM((tm, d), F32), pltpu.VMEM((tm, d), BF16),
                        pltpu.VMEM((tm, d), F32), pltpu.SemaphoreType.DMA(())],
    )
    return pl.pallas_call(
        _moe_ffn_kernel,
        grid_spec=grid_spec,
        out_shape=jax.ShapeDtypeStruct((r, d), F32),
        compiler_params=_cparams(("arbitrary", "arbitrary")),
        name="moe_ffn",
    )(block_e, n_used, tok3, tok3, h, wg, wu, wd)


def _router_kernel(h_ref, w_ref, b_ref, o_ref):
    logits = jnp.dot(h_ref[...], w_ref[...], preferred_element_type=F32,
                     precision=lax.Precision.HIGHEST) + b_ref[...]
    lane = lax.broadcasted_iota(jnp.int32, logits.shape, 1)
    logits = jnp.where(lane < N_EXPERTS, logits, NEG_INF)
    m1 = jnp.max(logits, axis=1, keepdims=True)
    i1 = jnp.min(jnp.where(logits == m1, lane, LANES), axis=1, keepdims=True)
    rest = jnp.where(lane == i1, NEG_INF, logits)
    m2 = jnp.max(rest, axis=1, keepdims=True)
    i2 = jnp.min(jnp.where(rest == m2, lane, LANES), axis=1, keepdims=True)
    e2 = jnp.exp(m2 - m1)
    g1 = 1.0 / (1.0 + e2)
    g2 = e2 / (1.0 + e2)
    out = jnp.where(lane == 0, i1.astype(F32),
                    jnp.where(lane == 1, i2.astype(F32),
                              jnp.where(lane == 2, g1, jnp.where(lane == 3, g2, 0.0))))
    o_ref[...] = out


def router(h, w_pad, b_pad):
    m, d = h.shape
    tm = 512
    return pl.pallas_call(
        _router_kernel,
        grid=(m // tm,),
        in_specs=[pl.BlockSpec((tm, d), lambda i: (i, 0)),
                  pl.BlockSpec((d, LANES), lambda i: (0, 0)),
                  pl.BlockSpec((1, LANES), lambda i: (0, 0))],
        out_specs=pl.BlockSpec((tm, LANES), lambda i: (i, 0)),
        out_shape=jax.ShapeDtypeStruct((m, LANES), F32),
        compiler_params=_cparams(("parallel",)),
        name="router",
    )(h, w_pad, b_pad)


def _alibi_slopes(n_heads):
    return jnp.asarray(2.0 ** (-8.0 * np.arange(1, n_heads + 1) / n_heads), dtype=F32)


def _bf16_parts(v):
    a = v.astype(BF16).astype(F32)
    b = (v - a).astype(BF16).astype(F32)
    c = (v - a - b).astype(BF16).astype(F32)
    return a, b, c


def _alibi_tables(seq, slopes2):
    nh = slopes2.shape[0]
    pos = jnp.arange(seq, dtype=jnp.int32)
    pos_parts = (((pos // LANES) * LANES).astype(F32), (pos % LANES).astype(F32))
    cols_q, cols_k = [], []
    for sp in _bf16_parts(slopes2):
        sp_col = jnp.broadcast_to(sp[:, None], (nh, seq))
        for pp in pos_parts:
            pp_row = jnp.broadcast_to(pp[None, :], (nh, seq))
            cols_q += [pp_row, sp_col]
            cols_k += [sp_col, -pp_row]
    pad = jnp.zeros((nh, seq, LANES - len(cols_q)), F32)
    eq = jnp.concatenate([jnp.stack(cols_q, axis=-1), pad], axis=-1).astype(BF16)
    ek = jnp.concatenate([jnp.stack(cols_k, axis=-1), pad], axis=-1).astype(BF16)
    return eq, ek


def _rope_tables(seq):
    inv_freq = ROPE_THETA ** (-jnp.arange(0, B_ROPE, 2, dtype=F32) / B_ROPE)
    ang = jnp.arange(seq, dtype=F32)[:, None] * inv_freq[None, :]
    cos, sin = jnp.cos(ang), jnp.sin(ang)
    cos2 = jnp.concatenate([cos, cos], axis=-1)
    sin2 = jnp.concatenate([sin, sin], axis=-1)
    z64 = jnp.zeros((seq, B_ROPE), F32)
    ka = jnp.concatenate([cos2, z64], axis=-1)
    kb = jnp.concatenate([sin2, z64], axis=-1)
    ones = jnp.ones((seq, B_NOPE), F32)
    z128 = jnp.zeros((seq, B_NOPE), F32)
    qa = jnp.tile(jnp.concatenate([ones, cos2, z64], axis=-1), (1, B_HEADS))
    qb = jnp.tile(jnp.concatenate([z128, sin2, z64], axis=-1), (1, B_HEADS))
    return ka, kb, qa, qb


def _rot_cols(w):
    half = w.shape[-1] // 2
    return jnp.concatenate([-w[..., half:], w[..., :half]], axis=-1)


def _routing_tables(flat_e):
    n_assign = flat_e.shape[0]
    onehot = (flat_e[:, None] == jnp.arange(N_EXPERTS, dtype=jnp.int32)[None, :]).astype(jnp.int32)
    csum = jnp.cumsum(onehot, axis=0)
    rank = jnp.sum((csum - onehot) * onehot, axis=1)
    sizes = csum[-1]
    padded = ((sizes + MOE_ROWS - 1) // MOE_ROWS) * MOE_ROWS
    pad_end = jnp.cumsum(padded)
    pad_start = pad_end - padded
    dest = (pad_start[flat_e] + rank).astype(jnp.int32)
    n_blocks = -(-n_assign // MOE_ROWS) + N_EXPERTS
    n_rows = n_blocks * MOE_ROWS
    tok = jnp.arange(n_assign, dtype=jnp.int32) // TOP_K
    row_tok = jnp.zeros((n_rows,), jnp.int32).at[dest].set(tok)
    block_start = jnp.arange(n_blocks, dtype=jnp.int32) * MOE_ROWS
    block_e = jnp.minimum(jnp.searchsorted(pad_end, block_start, side='right'), N_EXPERTS - 1).astype(jnp.int32)
    n_used = (pad_end[-1:] // MOE_ROWS).astype(jnp.int32)
    return dest, row_tok, block_e, n_used


def kernel(x, c, ada_w, ada_b, ln_g, ln_b, ab_w_in, ab_lam, ab_diff_g, ab_q_norm_g, ab_kv_norm_g, ab_w_uq,
           ab_w_ukv, ab_w_out, c_w_in, c_sink, c_w_out, ffn_w_gate, ffn_w_up, ffn_w_down, moe_w_router,
           moe_b_router, moe_w_gate, moe_w_up, moe_w_down):
    bsz, seq, d = x.shape
    depth = ada_w.shape[0]
    n_tok = bsz * seq
    alpha = (2 * depth) ** 0.25

    c_pad = jnp.zeros((8, d), F32).at[:bsz].set(c)
    mods = adaln_all(c_pad, ada_w.reshape(2 * depth, d, 3 * d), ada_b.reshape(2 * depth, 1, 3 * d))
    mods = mods.reshape(2 * depth, 8, 3, d)

    ka, kb, qa, qb = _rope_tables(seq)
    slopes_a2 = _alibi_slopes(A_HEADS) * LOG2E
    eq_a, ek_a = _alibi_tables(seq, slopes_a2)
    slopes_c = _alibi_slopes(C_HEADS)
    scale_a = A_QK_DIM ** -0.5 * LOG2E
    scale_b = (B_NOPE + B_ROPE) ** -0.5 * LOG2E
    scale_c = C_HEAD_DIM ** -0.5

    h = modulate(x, mods, 0, BF16)
    for layer in range(depth):
        i = layer // 2
        l2 = 2 * layer
        last = layer == depth - 1
        if layer % 2 == 0:
            lam_init = 0.8 - 0.6 * math.exp(-0.3 * layer)
            w_in = ab_w_in[i]
            w_kr = w_in[:, 3 * A_COLS + B_Q_LORA + B_KV_LORA:]
            w_in2 = jnp.concatenate([w_in[:, :A_COLS] * scale_a, w_in[:, A_COLS:], _rot_cols(w_kr)],
                                    axis=1).astype(BF16)
            p_ab = inproj_ab(h.reshape(n_tok, d), w_in2, ka, kb, seq)
            o_a = flash_diff(p_ab.reshape(bsz, seq, -1), eq_a, ek_a, slopes_a2, ab_lam[i],
                             ab_diff_g[i].reshape(1, -1), lam_init, bsz, seq)
            w_uq = ab_w_uq[i].reshape(B_Q_LORA, B_HEADS, B_NOPE + B_ROPE) * scale_b
            w_uq2 = jnp.concatenate([w_uq, _rot_cols(w_uq[..., B_NOPE:])], axis=-1)
            w_uq2 = w_uq2.reshape(B_Q_LORA, B_HEADS * 2 * LANES).astype(BF16)
            q_cat = uq_proj(p_ab, 3 * A_COLS // B_Q_LORA, ab_q_norm_g[i].reshape(1, -1), w_uq2, qa, qb, seq)
            k_cat, v_b = ukv_proj(p_ab, (3 * A_COLS + B_Q_LORA) // B_KV_LORA,
                                  (3 * A_COLS + B_Q_LORA + B_KV_LORA) // LANES,
                                  ab_kv_norm_g[i].reshape(1, -1), ab_w_ukv[i].astype(BF16))
            o_b = flash_mla(q_cat.reshape(bsz, seq, -1), k_cat.reshape(bsz, seq, -1),
                            v_b.reshape(bsz, seq, -1), bsz, seq)
            acts, w_out = [o_a, o_b], ab_w_out[i]
        else:
            w_in = c_w_in[i]
            nqc = C_HEADS * C_HEAD_DIM
            w_in2 = jnp.concatenate([w_in[:, :nqc] * scale_c, w_in[:, nqc:]], axis=1).astype(BF16)
            p_c = matmul(h.reshape(n_tok, d), w_in2, BF16, tn=1024)
            o_c = window_attn(p_c.reshape(bsz, seq, -1), slopes_c, c_sink[i].astype(F32), bsz, seq)
            acts, w_out = [o_c], c_w_out[i]
        moe_next = layer % 2 == 1
        x, h = outproj_norm(acts, w_out.astype(BF16), x, mods, l2, ln_g[layer, 0], ln_b[layer, 0], alpha,
                            F32 if moe_next else BF16)

        if not moe_next:
            y = ffn(h.reshape(n_tok, d), ffn_w_gate[i].astype(BF16), ffn_w_up[i].astype(BF16),
                    ffn_w_down[i].astype(BF16)).reshape(bsz, seq, d)
            route = dest = None
        else:
            hf = h.reshape(n_tok, d)
            w_r = jnp.zeros((d, LANES), F32).at[:, :N_EXPERTS].set(moe_w_router[i])
            b_r = jnp.zeros((1, LANES), F32).at[0, :N_EXPERTS].set(moe_b_router[i])
            route = router(hf, w_r, b_r)
            flat_e = route[:, :TOP_K].astype(jnp.int32).reshape(-1)
            dest, row_tok, block_e, n_used = _routing_tables(flat_e)
            y = moe_ffn(block_e, n_used, row_tok, hf, moe_w_gate[i].astype(BF16), moe_w_up[i].astype(BF16),
                        moe_w_down[i].astype(BF16))
            route = route.reshape(bsz, seq, LANES)
        x, h = postnorm(x, y, mods, l2 + 1, ln_g[layer, 1], ln_b[layer, 1], alpha,
                        route=route, dest=dest, next_dtype=None if last else BF16)
    return x
```

```python
import functools
import math

import numpy as np
import jax
import jax.numpy as jnp
from jax import lax
from jax.experimental import pallas as pl
from jax.experimental.pallas import tpu as pltpu

F32 = jnp.float32
BF16 = jnp.bfloat16

A_HEADS = 4
A_QK_DIM = 64
A_V_DIM = 128
B_HEADS = 4
B_Q_LORA = 512
B_KV_LORA = 256
B_NOPE = 128
B_ROPE = 64
B_V_DIM = 128
ROPE_THETA = 10000.0
C_HEADS = 16
C_KV_HEADS = 4
C_GROUP = C_HEADS // C_KV_HEADS
C_HEAD_DIM = 128
WINDOW = 128
N_EXPERTS = 8
TOP_K = 2
MOE_ROWS = 1024
LN_EPS = 1e-5
RMS_EPS = 1e-6

A_COLS = A_HEADS * 2 * A_QK_DIM
LANES = 128
VMEM_LIMIT = 56 * 1024 * 1024
NEG_INF = float("-inf")
LOG2E = math.log2(math.e)


def _cparams(sem, vmem=VMEM_LIMIT):
    return pltpu.CompilerParams(dimension_semantics=sem, vmem_limit_bytes=vmem)


def _adaln_kernel(c_ref, w_ref, b_ref, o_ref):
    c = c_ref[...]
    ca = c * jax.nn.sigmoid(c)
    o_ref[...] = jnp.dot(ca, w_ref[...], preferred_element_type=F32,
                         precision=lax.Precision.HIGHEST) + b_ref[...]


def adaln_all(c_pad, ada_w, ada_b):
    l2, d, n = ada_w.shape
    tn = 768
    return pl.pallas_call(
        _adaln_kernel,
        grid=(l2, n // tn),
        in_specs=[pl.BlockSpec((8, d), lambda l, j: (0, 0)),
                  pl.BlockSpec((None, d, tn), lambda l, j: (l, 0, j)),
                  pl.BlockSpec((None, 1, tn), lambda l, j: (l, 0, j))],
        out_specs=pl.BlockSpec((None, 8, tn), lambda l, j: (l, 0, j)),
        out_shape=jax.ShapeDtypeStruct((l2, 8, n), F32),
        compiler_params=_cparams(("parallel", "parallel")),
        name="adaln",
    )(c_pad, ada_w, ada_b)


def _modulate_kernel(x_ref, mod_ref, h_ref):
    x = x_ref[...]
    h_ref[...] = (x * (1.0 + mod_ref[1:2, :]) + mod_ref[0:1, :]).astype(h_ref.dtype)


def modulate(x, mods, l2, out_dtype):
    b, s, d = x.shape
    ts = 512
    return pl.pallas_call(
        _modulate_kernel,
        grid=(b, s // ts),
        in_specs=[pl.BlockSpec((None, ts, d), lambda i, j: (i, j, 0)),
                  pl.BlockSpec((None, None, 3, d), lambda i, j: (l2, i, 0, 0))],
        out_specs=pl.BlockSpec((None, ts, d), lambda i, j: (i, j, 0)),
        out_shape=jax.ShapeDtypeStruct((b, s, d), out_dtype),
        compiler_params=_cparams(("parallel", "parallel")),
        name="modulate",
    )(x, mods)


def _residual_norm(x, y, modc_ref, g_ref, b_ref, modn_ref, xo_ref, h_ref, alpha):
    z = alpha * x + (1.0 + modc_ref[2:3, :]) * y
    mu = jnp.mean(z, axis=-1, keepdims=True)
    zc = z - mu
    var = jnp.mean(zc * zc, axis=-1, keepdims=True)
    xn = zc * lax.rsqrt(var + LN_EPS) * g_ref[...] + b_ref[...]
    xo_ref[...] = xn
    if h_ref is not None:
        h_ref[...] = (xn * (1.0 + modn_ref[1:2, :]) + modn_ref[0:1, :]).astype(h_ref.dtype)


def _postnorm_kernel(*refs, alpha, routed, has_next):
    refs = list(refs)
    x_ref, y_ref, modc_ref, g_ref, b_ref = refs[:5]
    rest = refs[5:]
    if routed:
        route_ref, idx_ref, idxn_ref = rest[:3]
        rest = rest[3:]
    modn_ref = rest.pop(0) if has_next else None
    xo_ref = rest.pop(0)
    h_ref = rest.pop(0) if has_next else None
    x = x_ref[...]
    ts = x.shape[0]
    if routed:
        yg_ref, sem = rest
        step = pl.program_id(0) * pl.num_programs(1) + pl.program_id(1)
        n_steps = pl.num_programs(0) * pl.num_programs(1)
        slot = step % 2

        def issue_gather(idx, dst_slot):
            def body(r, carry):
                pltpu.make_async_copy(y_ref.at[pl.ds(idx[0, r], 1)], yg_ref.at[dst_slot, pl.ds(r, 1)],
                                      sem.at[dst_slot]).start()
                return carry
            lax.fori_loop(0, TOP_K * ts, body, 0, unroll=8)

        @pl.when(step == 0)
        def _():
            issue_gather(idx_ref, 0)

        @pl.when(step + 1 < n_steps)
        def _():
            issue_gather(idxn_ref, 1 - slot)

        pltpu.make_async_copy(y_ref.at[pl.ds(0, TOP_K * ts)], yg_ref.at[slot], sem.at[slot]).wait()
        g1 = route_ref[:, TOP_K:TOP_K + 1]
        g2 = route_ref[:, TOP_K + 1:TOP_K + 2]
        y = g1 * yg_ref[slot, 0:ts, :] + g2 * yg_ref[slot, ts:2 * ts, :]
    else:
        y = y_ref[...].astype(F32)
    _residual_norm(x, y, modc_ref, g_ref, b_ref, modn_ref, xo_ref, h_ref, alpha)


def postnorm(x, y, mods, l2, ln_g, ln_b, alpha, *, route=None, dest=None, next_dtype=None):
    b, s, d = x.shape
    ts = 256
    has_next = next_dtype is not None
    routed = route is not None
    nj = s // ts
    if routed:
        y_spec = pl.BlockSpec(memory_space=pl.ANY)
    else:
        y_spec = pl.BlockSpec((None, ts, d), lambda i, j: (i, j, 0))
    in_specs = [pl.BlockSpec((None, ts, d), lambda i, j: (i, j, 0)),
                y_spec,
                pl.BlockSpec((None, None, 3, d), lambda i, j: (l2, i, 0, 0)),
                pl.BlockSpec((1, d), lambda i, j: (0, 0)),
                pl.BlockSpec((1, d), lambda i, j: (0, 0))]
    args = [x, y, mods, ln_g.reshape(1, d), ln_b.reshape(1, d)]
    scratch = []
    if routed:
        n_steps = b * nj
        idx = dest.reshape(n_steps, ts, TOP_K).transpose(0, 2, 1).reshape(n_steps, 1, TOP_K * ts)
        in_specs += [pl.BlockSpec((None, ts, LANES), lambda i, j: (i, j, 0)),
                     pl.BlockSpec((None, 1, TOP_K * ts), lambda i, j: (i * nj + j, 0, 0),
                                  memory_space=pltpu.SMEM),
                     pl.BlockSpec((None, 1, TOP_K * ts),
                                  lambda i, j: (jnp.minimum(i * nj + j + 1, n_steps - 1), 0, 0),
                                  memory_space=pltpu.SMEM)]
        args += [route, idx, idx]
        scratch = [pltpu.VMEM((2, TOP_K * ts, d), F32), pltpu.SemaphoreType.DMA((2,))]
    out_specs = [pl.BlockSpec((None, ts, d), lambda i, j: (i, j, 0))]
    out_shape = [jax.ShapeDtypeStruct((b, s, d), F32)]
    if has_next:
        in_specs.append(pl.BlockSpec((None, None, 3, d), lambda i, j: (l2 + 1, i, 0, 0)))
        args.append(mods)
        out_specs.append(pl.BlockSpec((None, ts, d), lambda i, j: (i, j, 0)))
        out_shape.append(jax.ShapeDtypeStruct((b, s, d), next_dtype))
    sem = ("arbitrary", "arbitrary") if routed else ("parallel", "parallel")
    out = pl.pallas_call(
        functools.partial(_postnorm_kernel, alpha=alpha, routed=routed, has_next=has_next),
        grid=(b, nj),
        in_specs=in_specs,
        out_specs=out_specs,
        out_shape=out_shape,
        scratch_shapes=scratch,
        compiler_params=_cparams(sem),
        name="postnorm",
    )(*args)
    return (out[0], out[1]) if has_next else (out[0], None)


def _rms_rows(a_ref, g_ref):
    a = a_ref[...].astype(F32)
    return (a * lax.rsqrt(jnp.mean(a * a, axis=-1, keepdims=True) + RMS_EPS) * g_ref[...]).astype(BF16)


def _matmul_kernel(a_ref, w_ref, o_ref):
    o_ref[...] = jnp.dot(a_ref[...], w_ref[...], preferred_element_type=F32).astype(o_ref.dtype)


def matmul(a, w, out_dtype, tm=512, tn=None):
    m, k = a.shape
    n = w.shape[1]
    tn = n if tn is None else tn
    return pl.pallas_call(
        _matmul_kernel,
        grid=(m // tm, n // tn),
        in_specs=[pl.BlockSpec((tm, k), lambda i, j: (i, 0)),
                  pl.BlockSpec((k, tn), lambda i, j: (0, j))],
        out_specs=pl.BlockSpec((tm, tn), lambda i, j: (i, j)),
        out_shape=jax.ShapeDtypeStruct((m, n), out_dtype),
        compiler_params=_cparams(("parallel", "parallel")),
        name="matmul",
    )(a, w)


def _outproj_norm_kernel(*refs, n_in, alpha):
    a_refs, w_refs = refs[:n_in], refs[n_in:2 * n_in]
    x_ref, modc_ref, g_ref, b_ref, modn_ref, xo_ref, h_ref = refs[2 * n_in:]
    y = jnp.dot(a_refs[0][...], w_refs[0][...], preferred_element_type=F32)
    for a_ref, w_ref in zip(a_refs[1:], w_refs[1:]):
        y += jnp.dot(a_ref[...], w_ref[...], preferred_element_type=F32)
    _residual_norm(x_ref[...], y, modc_ref, g_ref, b_ref, modn_ref, xo_ref, h_ref, alpha)


def outproj_norm(acts, w, x, mods, l2, ln_g, ln_b, alpha, next_dtype):
    b, s, d = x.shape
    k = acts[0].shape[-1]
    assert all(a.shape[-1] == k for a in acts) and w.shape[0] == k * len(acts)
    ts = 512
    n_in = len(acts)
    row = lambda i, j: (i, j, 0)
    in_specs = ([pl.BlockSpec((None, ts, k), row) for _ in acts]
                + [pl.BlockSpec((k, d), functools.partial(lambda p, i, j: (p, 0), p)) for p in range(n_in)]
                + [pl.BlockSpec((None, ts, d), row),
                   pl.BlockSpec((None, None, 3, d), lambda i, j: (l2, i, 0, 0)),
                   pl.BlockSpec((1, d), lambda i, j: (0, 0)),
                   pl.BlockSpec((1, d), lambda i, j: (0, 0)),
                   pl.BlockSpec((None, None, 3, d), lambda i, j: (l2 + 1, i, 0, 0))])
    out = pl.pallas_call(
        functools.partial(_outproj_norm_kernel, n_in=n_in, alpha=alpha),
        grid=(b, s // ts),
        in_specs=in_specs,
        out_specs=[pl.BlockSpec((None, ts, d), row), pl.BlockSpec((None, ts, d), row)],
        out_shape=[jax.ShapeDtypeStruct((b, s, d), F32), jax.ShapeDtypeStruct((b, s, d), next_dtype)],
        compiler_params=_cparams(("parallel", "parallel")),
        name="outproj_norm",
    )(*acts, *([w] * n_in), x, mods, ln_g.reshape(1, d), ln_b.reshape(1, d), mods)
    return out[0], out[1]


def _inproj_ab_kernel(a_ref, w_ref, ra_ref, rb_ref, o_ref):
    acc = jnp.dot(a_ref[...], w_ref[...], preferred_element_type=F32)
    n = acc.shape[1]
    o_ref[:, :n - LANES] = acc[:, :n - LANES].astype(o_ref.dtype)
    last = acc[:, n - LANES:]
    roped = last * ra_ref[...] + pltpu.roll(last, LANES // 2, 1) * rb_ref[...]
    o_ref[:, n - LANES:] = roped.astype(o_ref.dtype)


def inproj_ab(h, w, rope_a, rope_b, seq):
    m, k = h.shape
    n = w.shape[1]
    tm = 512
    nsb = seq // tm
    return pl.pallas_call(
        _inproj_ab_kernel,
        grid=(m // tm,),
        in_specs=[pl.BlockSpec((tm, k), lambda i: (i, 0)),
                  pl.BlockSpec((k, n), lambda i: (0, 0)),
                  pl.BlockSpec((tm, LANES), lambda i: (i % nsb, 0)),
                  pl.BlockSpec((tm, LANES), lambda i: (i % nsb, 0))],
        out_specs=pl.BlockSpec((tm, n), lambda i: (i, 0)),
        out_shape=jax.ShapeDtypeStruct((m, n), BF16),
        compiler_params=_cparams(("parallel",)),
        name="inproj_ab",
    )(h, w, rope_a, rope_b)


def _uq_kernel(a_ref, g_ref, w_ref, ra_ref, rb_ref, o_ref):
    an = _rms_rows(a_ref, g_ref)
    acc = jnp.dot(an, w_ref[...], preferred_element_type=F32)
    n = acc.shape[1]
    o_ref[...] = (acc * ra_ref[...] + pltpu.roll(acc, n - B_ROPE, 1) * rb_ref[...]).astype(o_ref.dtype)


def uq_proj(p_ab, col_block, g, w, rope_a, rope_b, seq):
    m = p_ab.shape[0]
    k, n = w.shape
    tm = 512
    nsb = seq // tm
    return pl.pallas_call(
        _uq_kernel,
        grid=(m // tm,),
        in_specs=[pl.BlockSpec((tm, k), lambda i: (i, col_block)),
                  pl.BlockSpec((1, k), lambda i: (0, 0)),
                  pl.BlockSpec((k, n), lambda i: (0, 0)),
                  pl.BlockSpec((tm, n), lambda i: (i % nsb, 0)),
                  pl.BlockSpec((tm, n), lambda i: (i % nsb, 0))],
        out_specs=pl.BlockSpec((tm, n), lambda i: (i, 0)),
        out_shape=jax.ShapeDtypeStruct((m, n), BF16),
        compiler_params=_cparams(("parallel",)),
        name="uq_proj",
    )(p_ab, g, w, rope_a, rope_b)


def _ukv_kernel(a_ref, g_ref, w_ref, kr_ref, k_ref, v_ref):
    an = _rms_rows(a_ref, g_ref)
    acc = jnp.dot(an, w_ref[...], preferred_element_type=F32)
    kr = kr_ref[...]
    for h in range(B_HEADS):
        base = h * (B_NOPE + B_V_DIM)
        k_ref[:, 2 * h * LANES:(2 * h + 1) * LANES] = acc[:, base:base + B_NOPE].astype(k_ref.dtype)
        k_ref[:, (2 * h + 1) * LANES:(2 * h + 2) * LANES] = kr
        v_ref[:, h * B_V_DIM:(h + 1) * B_V_DIM] = acc[:, base + B_NOPE:base + B_NOPE + B_V_DIM].astype(v_ref.dtype)


def ukv_proj(p_ab, ckv_block, kr_block, g, w):
    m = p_ab.shape[0]
    k, n = w.shape
    tm = 512
    return pl.pallas_call(
        _ukv_kernel,
        grid=(m // tm,),
        in_specs=[pl.BlockSpec((tm, k), lambda i: (i, ckv_block)),
                  pl.BlockSpec((1, k), lambda i: (0, 0)),
                  pl.BlockSpec((k, n), lambda i: (0, 0)),
                  pl.BlockSpec((tm, LANES), lambda i: (i, kr_block))],
        out_specs=[pl.BlockSpec((tm, B_HEADS * 2 * LANES), lambda i: (i, 0)),
                   pl.BlockSpec((tm, B_HEADS * B_V_DIM), lambda i: (i, 0))],
        out_shape=[jax.ShapeDtypeStruct((m, B_HEADS * 2 * LANES), BF16),
                   jax.ShapeDtypeStruct((m, B_HEADS * B_V_DIM), BF16)],
        compiler_params=_cparams(("parallel",)),
        name="ukv_proj",
    )(p_ab, g, w, p_ab)


_NT = (((1,), (1,)), ((), ()))


def _ones_block(rows):
    return jnp.ones((rows, LANES), BF16)


def _online_softmax_step(s, v_aug, m_ref, acc_ref):
    tk = s.shape[1]
    m_prev = m_ref[...]
    m_new = jnp.maximum(m_prev, jnp.max(s, axis=1, keepdims=True))
    alpha = jnp.exp2(m_prev - m_new)
    p = jnp.exp2(s - jnp.concatenate([m_new] * (tk // LANES), axis=1))
    pv = jnp.dot(p.astype(v_aug.dtype), v_aug, preferred_element_type=F32)
    acc_ref[...] = jnp.concatenate([alpha, alpha], axis=1) * acc_ref[...] + pv
    m_ref[...] = m_new


def _softmax_result(acc_ref):
    acc = acc_ref[...]
    return acc[:, :LANES] / acc[:, LANES:]


def _flash_diff_kernel(slopes_ref, q_ref, eq_ref, k_ref, ek_ref, v_ref, lam_ref, g_ref, o_ref,
                       qaug_sc, m_sc, acc_sc, *, tk, lam_init, splits):
    tq = q_ref.shape[0]
    seq = k_ref.shape[0]
    qi = pl.program_id(2)
    q0 = qi * tq
    q = q_ref[...]
    eq = eq_ref[...]
    lane = lax.broadcasted_iota(jnp.int32, q.shape, 1)
    zero = jnp.zeros_like(q)
    for mp in range(2):
        qm = jnp.where((lane < A_QK_DIM) == (mp == 0), q, zero)
        qaug_sc[0, mp] = jnp.concatenate([qm, -eq], axis=1)
        qaug_sc[1, mp] = jnp.concatenate([qm, eq], axis=1)
    m_sc[...] = jnp.full(m_sc.shape, NEG_INF, F32)
    acc_sc[...] = jnp.zeros(acc_sc.shape, F32)
    ones_col = _ones_block(tk)

    def chunk(c, side, diagonal):
        k0 = pl.multiple_of(c * tk, tk)
        kc = jnp.concatenate([k_ref[pl.ds(k0, tk), :], ek_ref[pl.ds(k0, tk), :]], axis=1)
        v_aug = jnp.concatenate([v_ref[pl.ds(k0, tk), :], ones_col], axis=1)
        if diagonal:
            rel = (lax.broadcasted_iota(jnp.int32, (tq, tk), 0)
                   - lax.broadcasted_iota(jnp.int32, (tq, tk), 1)) + (q0 - k0)
            corr = (2.0 * slopes_ref[pl.program_id(1)]) * jnp.maximum(rel, 0).astype(F32)
        part = tq // splits
        for hh in range(splits):
            rows = pl.ds(hh * part, part)
            for mp in range(2):
                s = lax.dot_general(qaug_sc[side, mp, rows, :], kc, _NT, preferred_element_type=F32)
                if diagonal:
                    s = s - corr[hh * part:(hh + 1) * part]
                _online_softmax_step(s, v_aug, m_sc.at[mp, rows], acc_sc.at[mp, rows])

    cd = q0 // tk
    chunk(cd, 1, True)

    def off_diagonal(idx, carry):
        c = idx + (idx >= cd).astype(jnp.int32)
        chunk(c, (c > cd).astype(jnp.int32), False)
        return carry

    lax.fori_loop(0, seq // tk - 1, off_diagonal, 0, unroll=2)

    o0 = _softmax_result(acc_sc.at[0])
    o1 = _softmax_result(acc_sc.at[1])
    lam = lam_ref[...]
    lam_full = (jnp.exp(jnp.sum(lam[0:1] * lam[1:2], axis=1, keepdims=True))
                - jnp.exp(jnp.sum(lam[2:3] * lam[3:4], axis=1, keepdims=True)) + lam_init)
    o = o0 - lam_full * o1
    on = o * lax.rsqrt(jnp.mean(o * o, axis=-1, keepdims=True) + RMS_EPS) * g_ref[...]
    o_ref[...] = (on * (1.0 - lam_init)).astype(o_ref.dtype)


def flash_diff(p_ab, eq, ek, slopes2, lam, diff_g, lam_init, bsz, seq):
    tq, tk = 1024, 1024
    assert tk % tq == 0
    nh = A_HEADS
    return pl.pallas_call(
        functools.partial(_flash_diff_kernel, tk=tk, lam_init=lam_init, splits=4),
        grid=(bsz, nh, seq // tq),
        in_specs=[pl.BlockSpec(memory_space=pltpu.SMEM),
                  pl.BlockSpec((None, tq, LANES), lambda b, h, i: (b, i, h)),
                  pl.BlockSpec((None, tq, LANES), lambda b, h, i: (h, i, 0)),
                  pl.BlockSpec((None, seq, LANES), lambda b, h, i: (b, 0, nh + h)),
                  pl.BlockSpec((None, seq, LANES), lambda b, h, i: (h, 0, 0)),
                  pl.BlockSpec((None, seq, LANES), lambda b, h, i: (b, 0, 2 * nh + h)),
                  pl.BlockSpec((4, A_QK_DIM), lambda b, h, i: (0, 0)),
                  pl.BlockSpec((1, A_V_DIM), lambda b, h, i: (0, 0))],
        out_specs=pl.BlockSpec((None, tq, A_V_DIM), lambda b, h, i: (b, i, h)),
        out_shape=jax.ShapeDtypeStruct((bsz, seq, nh * A_V_DIM), BF16),
        scratch_shapes=[pltpu.VMEM((2, 2, tq, 2 * LANES), BF16),
                        pltpu.VMEM((2, tq, LANES), F32),
                        pltpu.VMEM((2, tq, 2 * LANES), F32)],
        compiler_params=_cparams(("parallel", "parallel", "parallel")),
        name="flash_diff",
    )(slopes2, p_ab, eq, p_ab, ek, p_ab, lam, diff_g)


def _flash_mla_kernel(q_ref, k_ref, v_ref, o_ref, m_sc, acc_sc, *, tk, splits):
    seq = k_ref.shape[0]
    part = q_ref.shape[0] // splits
    m_sc[...] = jnp.full(m_sc.shape, NEG_INF, F32)
    acc_sc[...] = jnp.zeros(acc_sc.shape, F32)
    ones_col = _ones_block(tk)

    def body(c, carry):
        k0 = pl.multiple_of(c * tk, tk)
        kc = k_ref[pl.ds(k0, tk), :]
        v_aug = jnp.concatenate([v_ref[pl.ds(k0, tk), :], ones_col], axis=1)
        for hh in range(splits):
            rows = pl.ds(hh * part, part)
            s = lax.dot_general(q_ref[rows, :], kc, _NT, preferred_element_type=F32)
            _online_softmax_step(s, v_aug, m_sc.at[rows], acc_sc.at[rows])
        return carry

    lax.fori_loop(0, seq // tk, body, 0, unroll=2)
    o_ref[...] = _softmax_result(acc_sc).astype(o_ref.dtype)


def flash_mla(q_cat, k_cat, v_b, bsz, seq):
    tq, tk = 1024, 1024
    kd = 2 * LANES
    return pl.pallas_call(
        functools.partial(_flash_mla_kernel, tk=tk, splits=4),
        grid=(bsz, B_HEADS, seq // tq),
        in_specs=[pl.BlockSpec((None, tq, kd), lambda b, h, i: (b, i, h)),
                  pl.BlockSpec((None, seq, kd), lambda b, h, i: (b, 0, h)),
                  pl.BlockSpec((None, seq, B_V_DIM), lambda b, h, i: (b, 0, h))],
        out_specs=pl.BlockSpec((None, tq, B_V_DIM), lambda b, h, i: (b, i, h)),
        out_shape=jax.ShapeDtypeStruct((bsz, seq, B_HEADS * B_V_DIM), BF16),
        scratch_shapes=[pltpu.VMEM((tq, LANES), F32), pltpu.VMEM((tq, 2 * LANES), F32)],
        compiler_params=_cparams(("parallel", "parallel", "parallel")),
        name="flash_mla",
    )(q_cat, k_cat, v_b)


def _window_kernel(slopes_ref, sink_ref, q_ref, k_ref, v_ref, o_ref):
    tq = q_ref.shape[0]
    seq = k_ref.shape[0]
    blk = WINDOW
    kw = 3 * blk
    n = pl.program_id(1)
    q0 = pl.program_id(2) * tq
    rel = (lax.broadcasted_iota(jnp.int32, (blk, kw), 1) - lax.broadcasted_iota(jnp.int32, (blk, kw), 0))
    ones_blk = _ones_block(kw)
    heads = [n * C_GROUP + g for g in range(C_GROUP)]
    slope_rows = jnp.concatenate([jnp.full((blk, kw), slopes_ref[hd], F32) for hd in heads], axis=0)
    sink_rows = jnp.concatenate([jnp.full((blk, LANES), sink_ref[hd], F32) for hd in heads], axis=0)
    for sb in range(tq // blk):
        t0 = q0 + sb * blk
        start = pl.multiple_of(jnp.clip(t0 - blk, 0, seq - kw), blk)
        kc = k_ref[pl.ds(start, kw), :]
        v_aug = jnp.concatenate([v_ref[pl.ds(start, kw), :], ones_blk], axis=1)
        dist = jnp.abs(rel + (start - t0))
        dist_rows = jnp.concatenate([dist] * C_GROUP, axis=0)
        qs = jnp.concatenate([q_ref[sb * blk:(sb + 1) * blk, g * C_HEAD_DIM:(g + 1) * C_HEAD_DIM]
                              for g in range(C_GROUP)], axis=0)
        s = lax.dot_general(qs, kc, _NT, preferred_element_type=F32) - slope_rows * dist_rows.astype(F32)
        s = jnp.where(dist_rows <= WINDOW, s, NEG_INF)
        m = jnp.maximum(jnp.max(s, axis=1, keepdims=True), sink_rows)
        e = jnp.exp(s - jnp.concatenate([m] * (kw // LANES), axis=1))
        pv = jnp.dot(e.astype(v_aug.dtype), v_aug, preferred_element_type=F32)
        o = pv[:, :C_HEAD_DIM] / (pv[:, C_HEAD_DIM:] + jnp.exp(sink_rows - m))
        for g in range(C_GROUP):
            o_ref[sb * blk:(sb + 1) * blk, g * C_HEAD_DIM:(g + 1) * C_HEAD_DIM] = (
                o[g * blk:(g + 1) * blk].astype(o_ref.dtype))


def window_attn(p_c, slopes, sink, bsz, seq):
    tq = 512
    gw = C_GROUP * C_HEAD_DIM
    nq = C_HEADS * C_HEAD_DIM // LANES
    return pl.pallas_call(
        _window_kernel,
        grid=(bsz, C_KV_HEADS, seq // tq),
        in_specs=[pl.BlockSpec(memory_space=pltpu.SMEM),
                  pl.BlockSpec(memory_space=pltpu.SMEM),
                  pl.BlockSpec((None, tq, gw), lambda b, n, i: (b, i, n)),
                  pl.BlockSpec((None, seq, C_HEAD_DIM), lambda b, n, i: (b, 0, nq + n)),
                  pl.BlockSpec((None, seq, C_HEAD_DIM), lambda b, n, i: (b, 0, nq + C_KV_HEADS + n))],
        out_specs=pl.BlockSpec((None, tq, gw), lambda b, n, i: (b, i, n)),
        out_shape=jax.ShapeDtypeStruct((bsz, seq, C_HEADS * C_HEAD_DIM), BF16),
        compiler_params=_cparams(("parallel", "parallel", "parallel")),
        name="window_attn",
    )(slopes, sink, p_c, p_c, p_c)


def _swiglu_step(x, wg_ref, wu_ref, wd_ref, acc_ref):
    g = jnp.dot(x, wg_ref[...], preferred_element_type=F32)
    u = jnp.dot(x, wu_ref[...], preferred_element_type=F32)
    a = (g * jax.nn.sigmoid(g) * u).astype(BF16)
    acc_ref[...] += jnp.dot(a, wd_ref[...], preferred_element_type=F32)


def _ffn_kernel(x_ref, wg_ref, wu_ref, wd_ref, o_ref, acc_ref):
    j = pl.program_id(1)

    @pl.when(j == 0)
    def _():
        acc_ref[...] = jnp.zeros(acc_ref.shape, F32)

    _swiglu_step(x_ref[...], wg_ref, wu_ref, wd_ref, acc_ref)

    @pl.when(j == pl.num_programs(1) - 1)
    def _():
        o_ref[...] = acc_ref[...].astype(o_ref.dtype)


def ffn(h, wg, wu, wd):
    m, d = h.shape
    f = wg.shape[1]
    tm, tf = 1024, 512
    return pl.pallas_call(
        _ffn_kernel,
        grid=(m // tm, f // tf),
        in_specs=[pl.BlockSpec((tm, d), lambda i, j: (i, 0)),
                  pl.BlockSpec((d, tf), lambda i, j: (0, j)),
                  pl.BlockSpec((d, tf), lambda i, j: (0, j)),
                  pl.BlockSpec((tf, d), lambda i, j: (j, 0))],
        out_specs=pl.BlockSpec((tm, d), lambda i, j: (i, 0)),
        out_shape=jax.ShapeDtypeStruct((m, d), F32),
        scratch_shapes=[pltpu.VMEM((tm, d), F32)],
        compiler_params=_cparams(("parallel", "arbitrary")),
        name="ffn",
    )(h, wg, wu, wd)


def _moe_ffn_kernel(be_ref, nu_ref, tok_ref, tokn_ref, h_ref, wg_ref, wu_ref, wd_ref, o_ref,
                    xg_ref, xb_ref, acc_ref, sem):
    del be_ref
    i = pl.program_id(0)
    j = pl.program_id(1)
    rows = xg_ref.shape[0]
    n_used = nu_ref[0]

    def issue_gather(idx_ref):
        def body(r, carry):
            pltpu.make_async_copy(h_ref.at[pl.ds(idx_ref[0, r], 1)], xg_ref.at[pl.ds(r, 1)], sem).start()
            return carry
        lax.fori_loop(0, rows, body, 0, unroll=8)

    @pl.when((j == 0) & (i < n_used))
    def _():
        @pl.when(i == 0)
        def _():
            issue_gather(tok_ref)
        pltpu.make_async_copy(h_ref.at[pl.ds(0, rows)], xg_ref, sem).wait()
        xb_ref[...] = xg_ref[...].astype(BF16)
        acc_ref[...] = jnp.zeros(acc_ref.shape, F32)

    @pl.when((j == 1) & (i + 1 < n_used))
    def _():
        issue_gather(tokn_ref)

    @pl.when(i < n_used)
    def _():
        _swiglu_step(xb_ref[...], wg_ref, wu_ref, wd_ref, acc_ref)

    @pl.when((j == pl.num_programs(1) - 1) & (i < n_used))
    def _():
        o_ref[...] = acc_ref[...]

    @pl.when((j == pl.num_programs(1) - 1) & (i >= n_used))
    def _():
        o_ref[...] = jnp.zeros(o_ref.shape, F32)


def moe_ffn(block_e, n_used, row_tok, h, wg, wu, wd):
    d = h.shape[1]
    r = row_tok.shape[0]
    f = wg.shape[2]
    tm, tf = MOE_ROWS, 256
    nj = f // tf
    assert nj >= 2
    nblk = r // tm
    tok3 = row_tok.reshape(nblk, 1, tm)

    def w_col(i, j, be, nu):
        return jnp.where(i < nu[0], j, 0)

    grid_spec = pltpu.PrefetchScalarGridSpec(
        num_scalar_prefetch=2,
        grid=(nblk, nj),
        in_specs=[pl.BlockSpec((None, 1, tm), lambda i, j, be, nu: (i, 0, 0), memory_space=pltpu.SMEM),
                  pl.BlockSpec((None, 1, tm), lambda i, j, be, nu: (jnp.minimum(i + 1, nblk - 1), 0, 0),
                               memory_space=pltpu.SMEM),
                  pl.BlockSpec(memory_space=pl.ANY),
                  pl.BlockSpec((None, d, tf), lambda i, j, be, nu: (be[i], 0, w_col(i, j, be, nu))),
                  pl.BlockSpec((None, d, tf), lambda i, j, be, nu: (be[i], 0, w_col(i, j, be, nu))),
                  pl.BlockSpec((None, tf, d), lambda i, j, be, nu: (be[i], w_col(i, j, be, nu), 0))],
        out_specs=pl.BlockSpec((tm, d), lambda i, j, be, nu: (i, 0)),
        scratch_shapes=[pltpu.VMEM((tm, d), F32), pltpu.VMEM((tm, d), BF16),
                        pltpu.VMEM((tm, d), F32), pltpu.SemaphoreType.DMA(())],
    )
    return pl.pallas_call(
        _moe_ffn_kernel,
        grid_spec=grid_spec,
        out_shape=jax.ShapeDtypeStruct((r, d), F32),
        compiler_params=_cparams(("arbitrary", "arbitrary")),
        name="moe_ffn",
    )(block_e, n_used, tok3, tok3, h, wg, wu, wd)


def _router_kernel(h_ref, w_ref, b_ref, o_ref):
    logits = jnp.dot(h_ref[...], w_ref[...], preferred_element_type=F32,
                     precision=lax.Precision.HIGHEST) + b_ref[...]
    lane = lax.broadcasted_iota(jnp.int32, logits.shape, 1)
    logits = jnp.where(lane < N_EXPERTS, logits, NEG_INF)
    m1 = jnp.max(logits, axis=1, keepdims=True)
    i1 = jnp.min(jnp.where(logits == m1, lane, LANES), axis=1, keepdims=True)
    rest = jnp.where(lane == i1, NEG_INF, logits)
    m2 = jnp.max(rest, axis=1, keepdims=True)
    i2 = jnp.min(jnp.where(rest == m2, lane, LANES), axis=1, keepdims=True)
    e2 = jnp.exp(m2 - m1)
    g1 = 1.0 / (1.0 + e2)
    g2 = e2 / (1.0 + e2)
    out = jnp.where(lane == 0, i1.astype(F32),
                    jnp.where(lane == 1, i2.astype(F32),
                              jnp.where(lane == 2, g1, jnp.where(lane == 3, g2, 0.0))))
    o_ref[...] = out


def router(h, w_pad, b_pad):
    m, d = h.shape
    tm = 512
    return pl.pallas_call(
        _router_kernel,
        grid=(m // tm,),
        in_specs=[pl.BlockSpec((tm, d), lambda i: (i, 0)),
                  pl.BlockSpec((d, LANES), lambda i: (0, 0)),
                  pl.BlockSpec((1, LANES), lambda i: (0, 0))],
        out_specs=pl.BlockSpec((tm, LANES), lambda i: (i, 0)),
        out_shape=jax.ShapeDtypeStruct((m, LANES), F32),
        compiler_params=_cparams(("parallel",)),
        name="router",
    )(h, w_pad, b_pad)


def _alibi_slopes(n_heads):
    return jnp.asarray(2.0 ** (-8.0 * np.arange(1, n_heads + 1) / n_heads), dtype=F32)


def _bf16_parts(v):
    a = v.astype(BF16).astype(F32)
    b = (v - a).astype(BF16).astype(F32)
    c = (v - a - b).astype(BF16).astype(F32)
    return a, b, c


def _alibi_tables(seq, slopes2):
    nh = slopes2.shape[0]
    pos = jnp.arange(seq, dtype=jnp.int32)
    pos_parts = (((pos // LANES) * LANES).astype(F32), (pos % LANES).astype(F32))
    cols_q, cols_k = [], []
    for sp in _bf16_parts(slopes2):
        sp_col = jnp.broadcast_to(sp[:, None], (nh, seq))
        for pp in pos_parts:
            pp_row = jnp.broadcast_to(pp[None, :], (nh, seq))
            cols_q += [pp_row, sp_col]
            cols_k += [sp_col, -pp_row]
    pad = jnp.zeros((nh, seq, LANES - len(cols_q)), F32)
    eq = jnp.concatenate([jnp.stack(cols_q, axis=-1), pad], axis=-1).astype(BF16)
    ek = jnp.concatenate([jnp.stack(cols_k, axis=-1), pad], axis=-1).astype(BF16)
    return eq, ek


def _rope_tables(seq):
    inv_freq = ROPE_THETA ** (-jnp.arange(0, B_ROPE, 2, dtype=F32) / B_ROPE)
    ang = jnp.arange(seq, dtype=F32)[:, None] * inv_freq[None, :]
    cos, sin = jnp.cos(ang), jnp.sin(ang)
    cos2 = jnp.concatenate([cos, cos], axis=-1)
    sin2 = jnp.concatenate([sin, sin], axis=-1)
    z64 = jnp.zeros((seq, B_ROPE), F32)
    ka = jnp.concatenate([cos2, z64], axis=-1)
    kb = jnp.concatenate([sin2, z64], axis=-1)
    ones = jnp.ones((seq, B_NOPE), F32)
    z128 = jnp.zeros((seq, B_NOPE), F32)
    qa = jnp.tile(jnp.concatenate([ones, cos2, z64], axis=-1), (1, B_HEADS))
    qb = jnp.tile(jnp.concatenate([z128, sin2, z64], axis=-1), (1, B_HEADS))
    return ka, kb, qa, qb


def _rot_cols(w):
    half = w.shape[-1] // 2
    return jnp.concatenate([-w[..., half:], w[..., :half]], axis=-1)


def _routing_tables(flat_e):
    n_assign = flat_e.shape[0]
    onehot = (flat_e[:, None] == jnp.arange(N_EXPERTS, dtype=jnp.int32)[None, :]).astype(jnp.int32)
    csum = jnp.cumsum(onehot, axis=0)
    rank = jnp.sum((csum - onehot) * onehot, axis=1)
    sizes = csum[-1]
    padded = ((sizes + MOE_ROWS - 1) // MOE_ROWS) * MOE_ROWS
    pad_end = jnp.cumsum(padded)
    pad_start = pad_end - padded
    dest = (pad_start[flat_e] + rank).astype(jnp.int32)
    n_blocks = -(-n_assign // MOE_ROWS) + N_EXPERTS
    n_rows = n_blocks * MOE_ROWS
    tok = jnp.arange(n_assign, dtype=jnp.int32) // TOP_K
    row_tok = jnp.zeros((n_rows,), jnp.int32).at[dest].set(tok)
    block_start = jnp.arange(n_blocks, dtype=jnp.int32) * MOE_ROWS
    block_e = jnp.minimum(jnp.searchsorted(pad_end, block_start, side='right'), N_EXPERTS - 1).astype(jnp.int32)
    n_used = (pad_end[-1:] // MOE_ROWS).astype(jnp.int32)
    return dest, row_tok, block_e, n_used


def kernel(x, c, ada_w, ada_b, ln_g, ln_b, ab_w_in, ab_lam, ab_diff_g, ab_q_norm_g, ab_kv_norm_g, ab_w_uq,
           ab_w_ukv, ab_w_out, c_w_in, c_sink, c_w_out, ffn_w_gate, ffn_w_up, ffn_w_down, moe_w_router,
           moe_b_router, moe_w_gate, moe_w_up, moe_w_down):
    bsz, seq, d = x.shape
    depth = ada_w.shape[0]
    n_tok = bsz * seq
    alpha = (2 * depth) ** 0.25

    c_pad = jnp.zeros((8, d), F32).at[:bsz].set(c)
    mods = adaln_all(c_pad, ada_w.reshape(2 * depth, d, 3 * d), ada_b.reshape(2 * depth, 1, 3 * d))
    mods = mods.reshape(2 * depth, 8, 3, d)

    ka, kb, qa, qb = _rope_tables(seq)
    slopes_a2 = _alibi_slopes(A_HEADS) * LOG2E
    eq_a, ek_a = _alibi_tables(seq, slopes_a2)
    slopes_c = _alibi_slopes(C_HEADS)
    scale_a = A_QK_DIM ** -0.5 * LOG2E
    scale_b = (B_NOPE + B_ROPE) ** -0.5 * LOG2E
    scale_c = C_HEAD_DIM ** -0.5

    h = modulate(x, mods, 0, BF16)
    for layer in range(depth):
        i = layer // 2
        l2 = 2 * layer
        last = layer == depth - 1
        if layer % 2 == 0:
            lam_init = 0.8 - 0.6 * math.exp(-0.3 * layer)
            w_in = ab_w_in[i]
            w_kr = w_in[:, 3 * A_COLS + B_Q_LORA + B_KV_LORA:]
            w_in2 = jnp.concatenate([w_in[:, :A_COLS] * scale_a, w_in[:, A_COLS:], _rot_cols(w_kr)],
                                    axis=1).astype(BF16)
            p_ab = inproj_ab(h.reshape(n_tok, d), w_in2, ka, kb, seq)
            o_a = flash_diff(p_ab.reshape(bsz, seq, -1), eq_a, ek_a, slopes_a2, ab_lam[i],
                             ab_diff_g[i].reshape(1, -1), lam_init, bsz, seq)
            w_uq = ab_w_uq[i].reshape(B_Q_LORA, B_HEADS, B_NOPE + B_ROPE) * scale_b
            w_uq2 = jnp.concatenate([w_uq, _rot_cols(w_uq[..., B_NOPE:])], axis=-1)
            w_uq2 = w_uq2.reshape(B_Q_LORA, B_HEADS * 2 * LANES).astype(BF16)
            q_cat = uq_proj(p_ab, 3 * A_COLS // B_Q_LORA, ab_q_norm_g[i].reshape(1, -1), w_uq2, qa, qb, seq)
            k_cat, v_b = ukv_proj(p_ab, (3 * A_COLS + B_Q_LORA) // B_KV_LORA,
                                  (3 * A_COLS + B_Q_LORA + B_KV_LORA) // LANES,
                                  ab_kv_norm_g[i].reshape(1, -1), ab_w_ukv[i].astype(BF16))
            o_b = flash_mla(q_cat.reshape(bsz, seq, -1), k_cat.reshape(bsz, seq, -1),
                            v_b.reshape(bsz, seq, -1), bsz, seq)
            acts, w_out = [o_a, o_b], ab_w_out[i]
        else:
            w_in = c_w_in[i]
            nqc = C_HEADS * C_HEAD_DIM
            w_in2 = jnp.concatenate([w_in[:, :nqc] * scale_c, w_in[:, nqc:]], axis=1).astype(BF16)
            p_c = matmul(h.reshape(n_tok, d), w_in2, BF16, tn=1024)
            o_c = window_attn(p_c.reshape(bsz, seq, -1), slopes_c, c_sink[i].astype(F32), bsz, seq)
            acts, w_out = [o_c], c_w_out[i]
        moe_next = layer % 2 == 1
        x, h = outproj_norm(acts, w_out.astype(BF16), x, mods, l2, ln_g[layer, 0], ln_b[layer, 0], alpha,
                            F32 if moe_next else BF16)

        if not moe_next:
            y = ffn(h.reshape(n_tok, d), ffn_w_gate[i].astype(BF16), ffn_w_up[i].astype(BF16),
                    ffn_w_down[i].astype(BF16)).reshape(bsz, seq, d)
            route = dest = None
        else:
            hf = h.reshape(n_tok, d)
            w_r = jnp.zeros((d, LANES), F32).at[:, :N_EXPERTS].set(moe_w_router[i])
            b_r = jnp.zeros((1, LANES), F32).at[0, :N_EXPERTS].set(moe_b_router[i])
            route = router(hf, w_r, b_r)
            flat_e = route[:, :TOP_K].astype(jnp.int32).reshape(-1)
            dest, row_tok, block_e, n_used = _routing_tables(flat_e)
            y = moe_ffn(block_e, n_used, row_tok, hf, moe_w_gate[i].astype(BF16), moe_w_up[i].astype(BF16),
                        moe_w_down[i].astype(BF16))
            route = route.reshape(bsz, seq, LANES)
        x, h = postnorm(x, y, mods, l2 + 1, ln_g[layer, 1], ln_b[layer, 1], alpha,
                        route=route, dest=dest, next_dtype=None if last else BF16)
    return x
```

```python
import functools
import math

import numpy as np
import jax
import jax.numpy as jnp
from jax import lax
from jax.experimental import pallas as pl
from jax.experimental.pallas import tpu as pltpu

F32 = jnp.float32
BF16 = jnp.bfloat16

A_HEADS = 4
A_QK_DIM = 64
A_V_DIM = 128
B_HEADS = 4
B_Q_LORA = 512
B_KV_LORA = 256
B_NOPE = 128
B_ROPE = 64
B_V_DIM = 128
ROPE_THETA = 10000.0
C_HEADS = 16
C_KV_HEADS = 4
C_GROUP = C_HEADS // C_KV_HEADS
C_HEAD_DIM = 128
WINDOW = 128
N_EXPERTS = 8
TOP_K = 2
MOE_ROWS = 1024
LN_EPS = 1e-5
RMS_EPS = 1e-6

A_COLS = A_HEADS * 2 * A_QK_DIM
LANES = 128
VMEM_LIMIT = 56 * 1024 * 1024
NEG_INF = float("-inf")
LOG2E = math.log2(math.e)
EXP2_UNDERFLOW = -151.0
FLASH_TK = 1024


def _cparams(sem, vmem=VMEM_LIMIT):
    return pltpu.CompilerParams(dimension_semantics=sem, vmem_limit_bytes=vmem)


def _adaln_kernel(c_ref, w_ref, b_ref, o_ref):
    c = c_ref[...]
    ca = c * jax.nn.sigmoid(c)
    o_ref[...] = jnp.dot(ca, w_ref[...], preferred_element_type=F32,
                         precision=lax.Precision.HIGHEST) + b_ref[...]


def adaln_all(c_pad, ada_w, ada_b):
    l2, d, n = ada_w.shape
    tn = 768
    return pl.pallas_call(
        _adaln_kernel,
        grid=(l2, n // tn),
        in_specs=[pl.BlockSpec((8, d), lambda l, j: (0, 0)),
                  pl.BlockSpec((None, d, tn), lambda l, j: (l, 0, j)),
                  pl.BlockSpec((None, 1, tn), lambda l, j: (l, 0, j))],
        out_specs=pl.BlockSpec((None, 8, tn), lambda l, j: (l, 0, j)),
        out_shape=jax.ShapeDtypeStruct((l2, 8, n), F32),
        compiler_params=_cparams(("parallel", "parallel")),
        name="adaln",
    )(c_pad, ada_w, ada_b)


def _modulate_kernel(x_ref, mod_ref, h_ref):
    x = x_ref[...]
    h_ref[...] = (x * (1.0 + mod_ref[1:2, :]) + mod_ref[0:1, :]).astype(h_ref.dtype)


def modulate(x, mods, l2, out_dtype):
    b, s, d = x.shape
    ts = 512
    return pl.pallas_call(
        _modulate_kernel,
        grid=(b, s // ts),
        in_specs=[pl.BlockSpec((None, ts, d), lambda i, j: (i, j, 0)),
                  pl.BlockSpec((None, None, 3, d), lambda i, j: (l2, i, 0, 0))],
        out_specs=pl.BlockSpec((None, ts, d), lambda i, j: (i, j, 0)),
        out_shape=jax.ShapeDtypeStruct((b, s, d), out_dtype),
        compiler_params=_cparams(("parallel", "parallel")),
        name="modulate",
    )(x, mods)


def _residual_norm(x, y, modc_ref, g_ref, b_ref, modn_ref, xo_ref, h_ref, alpha):
    z = alpha * x + (1.0 + modc_ref[2:3, :]) * y
    mu = jnp.mean(z, axis=-1, keepdims=True)
    zc = z - mu
    var = jnp.mean(zc * zc, axis=-1, keepdims=True)
    xn = zc * lax.rsqrt(var + LN_EPS) * g_ref[...] + b_ref[...]
    xo_ref[...] = xn
    if h_ref is not None:
        h_ref[...] = (xn * (1.0 + modn_ref[1:2, :]) + modn_ref[0:1, :]).astype(h_ref.dtype)


def _postnorm_kernel(*refs, alpha, routed, has_next):
    refs = list(refs)
    x_ref, y_ref, modc_ref, g_ref, b_ref = refs[:5]
    rest = refs[5:]
    if routed:
        route_ref, idx_ref, idxn_ref = rest[:3]
        rest = rest[3:]
    modn_ref = rest.pop(0) if has_next else None
    xo_ref = rest.pop(0)
    h_ref = rest.pop(0) if has_next else None
    x = x_ref[...]
    ts = x.shape[0]
    if routed:
        yg_ref, sem = rest
        step = pl.program_id(0) * pl.num_programs(1) + pl.program_id(1)
        n_steps = pl.num_programs(0) * pl.num_programs(1)
        slot = step % 2

        def issue_gather(idx, dst_slot):
            def body(r, carry):
                pltpu.make_async_copy(y_ref.at[pl.ds(idx[0, r], 1)], yg_ref.at[dst_slot, pl.ds(r, 1)],
                                      sem.at[dst_slot]).start()
                return carry
            lax.fori_loop(0, TOP_K * ts, body, 0, unroll=8)

        @pl.when(step == 0)
        def _():
            issue_gather(idx_ref, 0)

        @pl.when(step + 1 < n_steps)
        def _():
            issue_gather(idxn_ref, 1 - slot)

        pltpu.make_async_copy(y_ref.at[pl.ds(0, TOP_K * ts)], yg_ref.at[slot], sem.at[slot]).wait()
        g1 = route_ref[:, TOP_K:TOP_K + 1]
        g2 = route_ref[:, TOP_K + 1:TOP_K + 2]
        y = g1 * yg_ref[slot, 0:ts, :] + g2 * yg_ref[slot, ts:2 * ts, :]
    else:
        y = y_ref[...].astype(F32)
    _residual_norm(x, y, modc_ref, g_ref, b_ref, modn_ref, xo_ref, h_ref, alpha)


def postnorm(x, y, mods, l2, ln_g, ln_b, alpha, *, route=None, dest=None, next_dtype=None):
    b, s, d = x.shape
    ts = 256
    has_next = next_dtype is not None
    routed = route is not None
    nj = s // ts
    if routed:
        y_spec = pl.BlockSpec(memory_space=pl.ANY)
    else:
        y_spec = pl.BlockSpec((None, ts, d), lambda i, j: (i, j, 0))
    in_specs = [pl.BlockSpec((None, ts, d), lambda i, j: (i, j, 0)),
                y_spec,
                pl.BlockSpec((None, None, 3, d), lambda i, j: (l2, i, 0, 0)),
                pl.BlockSpec((1, d), lambda i, j: (0, 0)),
                pl.BlockSpec((1, d), lambda i, j: (0, 0))]
    args = [x, y, mods, ln_g.reshape(1, d), ln_b.reshape(1, d)]
    scratch = []
    if routed:
        n_steps = b * nj
        idx = dest.reshape(n_steps, ts, TOP_K).transpose(0, 2, 1).reshape(n_steps, 1, TOP_K * ts)
        in_specs += [pl.BlockSpec((None, ts, LANES), lambda i, j: (i, j, 0)),
                     pl.BlockSpec((None, 1, TOP_K * ts), lambda i, j: (i * nj + j, 0, 0),
                                  memory_space=pltpu.SMEM),
                     pl.BlockSpec((None, 1, TOP_K * ts),
                                  lambda i, j: (jnp.minimum(i * nj + j + 1, n_steps - 1), 0, 0),
                                  memory_space=pltpu.SMEM)]
        args += [route, idx, idx]
        scratch = [pltpu.VMEM((2, TOP_K * ts, d), F32), pltpu.SemaphoreType.DMA((2,))]
    out_specs = [pl.BlockSpec((None, ts, d), lambda i, j: (i, j, 0))]
    out_shape = [jax.ShapeDtypeStruct((b, s, d), F32)]
    if has_next:
        in_specs.append(pl.BlockSpec((None, None, 3, d), lambda i, j: (l2 + 1, i, 0, 0)))
        args.append(mods)
        out_specs.append(pl.BlockSpec((None, ts, d), lambda i, j: (i, j, 0)))
        out_shape.append(jax.ShapeDtypeStruct((b, s, d), next_dtype))
    sem = ("arbitrary", "arbitrary") if routed else ("parallel", "parallel")
    out = pl.pallas_call(
        functools.partial(_postnorm_kernel, alpha=alpha, routed=routed, has_next=has_next),
        grid=(b, nj),
        in_specs=in_specs,
        out_specs=out_specs,
        out_shape=out_shape,
        scratch_shapes=scratch,
        compiler_params=_cparams(sem),
        name="postnorm",
    )(*args)
    return (out[0], out[1]) if has_next else (out[0], None)


def _rms_rows(a_ref, g_ref):
    a = a_ref[...].astype(F32)
    return (a * lax.rsqrt(jnp.mean(a * a, axis=-1, keepdims=True) + RMS_EPS) * g_ref[...]).astype(BF16)


def _matmul_kernel(a_ref, w_ref, o_ref):
    o_ref[...] = jnp.dot(a_ref[...], w_ref[...], preferred_element_type=F32).astype(o_ref.dtype)


def matmul(a, w, out_dtype, tm=512, tn=None):
    m, k = a.shape
    n = w.shape[1]
    tn = n if tn is None else tn
    return pl.pallas_call(
        _matmul_kernel,
        grid=(m // tm, n // tn),
        in_specs=[pl.BlockSpec((tm, k), lambda i, j: (i, 0)),
                  pl.BlockSpec((k, tn), lambda i, j: (0, j))],
        out_specs=pl.BlockSpec((tm, tn), lambda i, j: (i, j)),
        out_shape=jax.ShapeDtypeStruct((m, n), out_dtype),
        compiler_params=_cparams(("parallel", "parallel")),
        name="matmul",
    )(a, w)


def _outproj_norm_kernel(*refs, n_in, alpha):
    a_refs, w_refs = refs[:n_in], refs[n_in:2 * n_in]
    x_ref, modc_ref, g_ref, b_ref, modn_ref, xo_ref, h_ref = refs[2 * n_in:]
    y = jnp.dot(a_refs[0][...], w_refs[0][...], preferred_element_type=F32)
    for a_ref, w_ref in zip(a_refs[1:], w_refs[1:]):
        y += jnp.dot(a_ref[...], w_ref[...], preferred_element_type=F32)
    _residual_norm(x_ref[...], y, modc_ref, g_ref, b_ref, modn_ref, xo_ref, h_ref, alpha)


def outproj_norm(acts, w, x, mods, l2, ln_g, ln_b, alpha, next_dtype):
    b, s, d = x.shape
    k = acts[0].shape[-1]
    assert all(a.shape[-1] == k for a in acts) and w.shape[0] == k * len(acts)
    ts = 512
    n_in = len(acts)
    row = lambda i, j: (i, j, 0)
    in_specs = ([pl.BlockSpec((None, ts, k), row) for _ in acts]
                + [pl.BlockSpec((k, d), functools.partial(lambda p, i, j: (p, 0), p)) for p in range(n_in)]
                + [pl.BlockSpec((None, ts, d), row),
                   pl.BlockSpec((None, None, 3, d), lambda i, j: (l2, i, 0, 0)),
                   pl.BlockSpec((1, d), lambda i, j: (0, 0)),
                   pl.BlockSpec((1, d), lambda i, j: (0, 0)),
                   pl.BlockSpec((None, None, 3, d), lambda i, j: (l2 + 1, i, 0, 0))])
    out = pl.pallas_call(
        functools.partial(_outproj_norm_kernel, n_in=n_in, alpha=alpha),
        grid=(b, s // ts),
        in_specs=in_specs,
        out_specs=[pl.BlockSpec((None, ts, d), row), pl.BlockSpec((None, ts, d), row)],
        out_shape=[jax.ShapeDtypeStruct((b, s, d), F32), jax.ShapeDtypeStruct((b, s, d), next_dtype)],
        compiler_params=_cparams(("parallel", "parallel")),
        name="outproj_norm",
    )(*acts, *([w] * n_in), x, mods, ln_g.reshape(1, d), ln_b.reshape(1, d), mods)
    return out[0], out[1]


def _inproj_ab_kernel(a_ref, w_ref, ra_ref, rb_ref, o_ref):
    acc = jnp.dot(a_ref[...], w_ref[...], preferred_element_type=F32)
    n = acc.shape[1]
    o_ref[:, :n - LANES] = acc[:, :n - LANES].astype(o_ref.dtype)
    last = acc[:, n - LANES:]
    roped = last * ra_ref[...] + pltpu.roll(last, LANES // 2, 1) * rb_ref[...]
    o_ref[:, n - LANES:] = roped.astype(o_ref.dtype)


def inproj_ab(h, w, rope_a, rope_b, seq):
    m, k = h.shape
    n = w.shape[1]
    tm = 512
    nsb = seq // tm
    return pl.pallas_call(
        _inproj_ab_kernel,
        grid=(m // tm,),
        in_specs=[pl.BlockSpec((tm, k), lambda i: (i, 0)),
                  pl.BlockSpec((k, n), lambda i: (0, 0)),
                  pl.BlockSpec((tm, LANES), lambda i: (i % nsb, 0)),
                  pl.BlockSpec((tm, LANES), lambda i: (i % nsb, 0))],
        out_specs=pl.BlockSpec((tm, n), lambda i: (i, 0)),
        out_shape=jax.ShapeDtypeStruct((m, n), BF16),
        compiler_params=_cparams(("parallel",)),
        name="inproj_ab",
    )(h, w, rope_a, rope_b)


def _uq_kernel(a_ref, g_ref, w_ref, ra_ref, rb_ref, o_ref):
    an = _rms_rows(a_ref, g_ref)
    acc = jnp.dot(an, w_ref[...], preferred_element_type=F32)
    n = acc.shape[1]
    o_ref[...] = (acc * ra_ref[...] + pltpu.roll(acc, n - B_ROPE, 1) * rb_ref[...]).astype(o_ref.dtype)


def uq_proj(p_ab, col_block, g, w, rope_a, rope_b, seq):
    m = p_ab.shape[0]
    k, n = w.shape
    tm = 512
    nsb = seq // tm
    return pl.pallas_call(
        _uq_kernel,
        grid=(m // tm,),
        in_specs=[pl.BlockSpec((tm, k), lambda i: (i, col_block)),
                  pl.BlockSpec((1, k), lambda i: (0, 0)),
                  pl.BlockSpec((k, n), lambda i: (0, 0)),
                  pl.BlockSpec((tm, n), lambda i: (i % nsb, 0)),
                  pl.BlockSpec((tm, n), lambda i: (i % nsb, 0))],
        out_specs=pl.BlockSpec((tm, n), lambda i: (i, 0)),
        out_shape=jax.ShapeDtypeStruct((m, n), BF16),
        compiler_params=_cparams(("parallel",)),
        name="uq_proj",
    )(p_ab, g, w, rope_a, rope_b)


def _ukv_kernel(a_ref, g_ref, w_ref, kr_ref, k_ref, v_ref):
    an = _rms_rows(a_ref, g_ref)
    acc = jnp.dot(an, w_ref[...], preferred_element_type=F32)
    kr = kr_ref[...]
    for h in range(B_HEADS):
        base = h * (B_NOPE + B_V_DIM)
        k_ref[:, 2 * h * LANES:(2 * h + 1) * LANES] = acc[:, base:base + B_NOPE].astype(k_ref.dtype)
        k_ref[:, (2 * h + 1) * LANES:(2 * h + 2) * LANES] = kr
        v_ref[:, h * B_V_DIM:(h + 1) * B_V_DIM] = acc[:, base + B_NOPE:base + B_NOPE + B_V_DIM].astype(v_ref.dtype)


def ukv_proj(p_ab, ckv_block, kr_block, g, w):
    m = p_ab.shape[0]
    k, n = w.shape
    tm = 512
    return pl.pallas_call(
        _ukv_kernel,
        grid=(m // tm,),
        in_specs=[pl.BlockSpec((tm, k), lambda i: (i, ckv_block)),
                  pl.BlockSpec((1, k), lambda i: (0, 0)),
                  pl.BlockSpec((k, n), lambda i: (0, 0)),
                  pl.BlockSpec((tm, LANES), lambda i: (i, kr_block))],
        out_specs=[pl.BlockSpec((tm, B_HEADS * 2 * LANES), lambda i: (i, 0)),
                   pl.BlockSpec((tm, B_HEADS * B_V_DIM), lambda i: (i, 0))],
        out_shape=[jax.ShapeDtypeStruct((m, B_HEADS * 2 * LANES), BF16),
                   jax.ShapeDtypeStruct((m, B_HEADS * B_V_DIM), BF16)],
        compiler_params=_cparams(("parallel",)),
        name="ukv_proj",
    )(p_ab, g, w, p_ab)


_NT = (((1,), (1,)), ((), ()))


def _ones_block(rows):
    return jnp.ones((rows, LANES), BF16)


def _online_softmax_step(s, v_aug, m_ref, acc_ref):
    tk = s.shape[1]
    m_prev = m_ref[...]
    m_new = jnp.maximum(m_prev, jnp.max(s, axis=1, keepdims=True))
    alpha = jnp.exp2(m_prev - m_new)
    p = jnp.exp2(s - jnp.concatenate([m_new] * (tk // LANES), axis=1))
    pv = jnp.dot(p.astype(v_aug.dtype), v_aug, preferred_element_type=F32)
    acc_ref[...] = jnp.concatenate([alpha, alpha], axis=1) * acc_ref[...] + pv
    m_ref[...] = m_new


def _softmax_result(acc_ref):
    acc = acc_ref[...]
    return acc[:, :LANES] / acc[:, LANES:]


def _flash_diff_kernel(slopes_ref, kn_ref, q_ref, eq_ref, k_ref, ek_ref, v_ref, lam_ref, g_ref, o_ref,
                       qaug_sc, m_sc, acc_sc, *, tk, lam_init, splits):
    tq = q_ref.shape[0]
    seq = k_ref.shape[0]
    n_chunks = seq // tk
    qi = pl.program_id(2)
    q0 = qi * tq
    q = q_ref[...]
    eq = eq_ref[...]
    qf = q.astype(F32)
    q_norm = jnp.sqrt(jnp.max(jnp.sum(qf * qf, axis=1, keepdims=True), axis=0, keepdims=True))[0, 0]
    kn_base = (pl.program_id(0) * pl.num_programs(1) + pl.program_id(1)) * n_chunks
    lane = lax.broadcasted_iota(jnp.int32, q.shape, 1)
    zero = jnp.zeros_like(q)
    for mp in range(2):
        qm = jnp.where((lane < A_QK_DIM) == (mp == 0), q, zero)
        qaug_sc[0, mp] = jnp.concatenate([qm, -eq], axis=1)
        qaug_sc[1, mp] = jnp.concatenate([qm, eq], axis=1)
    m_sc[...] = jnp.full(m_sc.shape, NEG_INF, F32)
    acc_sc[...] = jnp.zeros(acc_sc.shape, F32)
    ones_col = _ones_block(tk)

    def chunk(c, side, diagonal):
        k0 = pl.multiple_of(c * tk, tk)
        kc = jnp.concatenate([k_ref[pl.ds(k0, tk), :], ek_ref[pl.ds(k0, tk), :]], axis=1)
        v_aug = jnp.concatenate([v_ref[pl.ds(k0, tk), :], ones_col], axis=1)
        if diagonal:
            rel = (lax.broadcasted_iota(jnp.int32, (tq, tk), 0)
                   - lax.broadcasted_iota(jnp.int32, (tq, tk), 1)) + (q0 - k0)
            corr = (2.0 * slopes_ref[pl.program_id(1)]) * jnp.maximum(rel, 0).astype(F32)
        part = tq // splits
        for hh in range(splits):
            rows = pl.ds(hh * part, part)
            for mp in range(2):
                s = lax.dot_general(qaug_sc[side, mp, rows, :], kc, _NT, preferred_element_type=F32)
                if diagonal:
                    s = s - corr[hh * part:(hh + 1) * part]
                _online_softmax_step(s, v_aug, m_sc.at[mp, rows], acc_sc.at[mp, rows])

    cd = q0 // tk
    chunk(cd, 1, True)
    m_floor = jnp.min(jnp.min(jnp.minimum(m_sc[0], m_sc[1]), axis=1, keepdims=True),
                      axis=0, keepdims=True)[0, 0]
    slope = slopes_ref[pl.program_id(1)]

    def off_diagonal(idx, carry):
        c = idx + (idx >= cd).astype(jnp.int32)
        gap = (jnp.abs(c - cd) - 1) * tk + 1
        bound = q_norm * kn_ref[kn_base + c] - slope * gap.astype(F32)

        @pl.when(bound - m_floor > EXP2_UNDERFLOW)
        def _():
            chunk(c, (c > cd).astype(jnp.int32), False)
        return carry

    lax.fori_loop(0, n_chunks - 1, off_diagonal, 0, unroll=2)

    o0 = _softmax_result(acc_sc.at[0])
    o1 = _softmax_result(acc_sc.at[1])
    lam = lam_ref[...]
    lam_full = (jnp.exp(jnp.sum(lam[0:1] * lam[1:2], axis=1, keepdims=True))
                - jnp.exp(jnp.sum(lam[2:3] * lam[3:4], axis=1, keepdims=True)) + lam_init)
    o = o0 - lam_full * o1
    on = o * lax.rsqrt(jnp.mean(o * o, axis=-1, keepdims=True) + RMS_EPS) * g_ref[...]
    o_ref[...] = (on * (1.0 - lam_init)).astype(o_ref.dtype)


def flash_diff(p_ab, eq, ek, slopes2, k_norms, lam, diff_g, lam_init, bsz, seq):
    tq, tk = FLASH_TK, FLASH_TK
    nh = A_HEADS
    return pl.pallas_call(
        functools.partial(_flash_diff_kernel, tk=tk, lam_init=lam_init, splits=4),
        grid=(bsz, nh, seq // tq),
        in_specs=[pl.BlockSpec(memory_space=pltpu.SMEM),
                  pl.BlockSpec(memory_space=pltpu.SMEM),
                  pl.BlockSpec((None, tq, LANES), lambda b, h, i: (b, i, h)),
                  pl.BlockSpec((None, tq, LANES), lambda b, h, i: (h, i, 0)),
                  pl.BlockSpec((None, seq, LANES), lambda b, h, i: (b, 0, nh + h)),
                  pl.BlockSpec((None, seq, LANES), lambda b, h, i: (h, 0, 0)),
                  pl.BlockSpec((None, seq, LANES), lambda b, h, i: (b, 0, 2 * nh + h)),
                  pl.BlockSpec((4, A_QK_DIM), lambda b, h, i: (0, 0)),
                  pl.BlockSpec((1, A_V_DIM), lambda b, h, i: (0, 0))],
        out_specs=pl.BlockSpec((None, tq, A_V_DIM), lambda b, h, i: (b, i, h)),
        out_shape=jax.ShapeDtypeStruct((bsz, seq, nh * A_V_DIM), BF16),
        scratch_shapes=[pltpu.VMEM((2, 2, tq, 2 * LANES), BF16),
                        pltpu.VMEM((2, tq, LANES), F32),
                        pltpu.VMEM((2, tq, 2 * LANES), F32)],
        compiler_params=_cparams(("parallel", "parallel", "parallel")),
        name="flash_diff",
    )(slopes2, k_norms, p_ab, eq, p_ab, ek, p_ab, lam, diff_g)


def _flash_mla_kernel(q_ref, k_ref, v_ref, o_ref, m_sc, acc_sc, *, tk, splits):
    seq = k_ref.shape[0]
    part = q_ref.shape[0] // splits
    m_sc[...] = jnp.full(m_sc.shape, NEG_INF, F32)
    acc_sc[...] = jnp.zeros(acc_sc.shape, F32)
    ones_col = _ones_block(tk)

    def body(c, carry):
        k0 = pl.multiple_of(c * tk, tk)
        kc = k_ref[pl.ds(k0, tk), :]
        v_aug = jnp.concatenate([v_ref[pl.ds(k0, tk), :], ones_col], axis=1)
        for hh in range(splits):
            rows = pl.ds(hh * part, part)
            s = lax.dot_general(q_ref[rows, :], kc, _NT, preferred_element_type=F32)
            _online_softmax_step(s, v_aug, m_sc.at[rows], acc_sc.at[rows])
        return carry

    lax.fori_loop(0, seq // tk, body, 0, unroll=2)
    o_ref[...] = _softmax_result(acc_sc).astype(o_ref.dtype)


def flash_mla(q_cat, k_cat, v_b, bsz, seq):
    tq, tk = 1024, 1024
    kd = 2 * LANES
    return pl.pallas_call(
        functools.partial(_flash_mla_kernel, tk=tk, splits=4),
        grid=(bsz, B_HEADS, seq // tq),
        in_specs=[pl.BlockSpec((None, tq, kd), lambda b, h, i: (b, i, h)),
                  pl.BlockSpec((None, seq, kd), lambda b, h, i: (b, 0, h)),
                  pl.BlockSpec((None, seq, B_V_DIM), lambda b, h, i: (b, 0, h))],
        out_specs=pl.BlockSpec((None, tq, B_V_DIM), lambda b, h, i: (b, i, h)),
        out_shape=jax.ShapeDtypeStruct((bsz, seq, B_HEADS * B_V_DIM), BF16),
        scratch_shapes=[pltpu.VMEM((tq, LANES), F32), pltpu.VMEM((tq, 2 * LANES), F32)],
        compiler_params=_cparams(("parallel", "parallel", "parallel")),
        name="flash_mla",
    )(q_cat, k_cat, v_b)


def _window_kernel(slopes_ref, sink_ref, q_ref, k_ref, v_ref, o_ref):
    tq = q_ref.shape[0]
    seq = k_ref.shape[0]
    blk = WINDOW
    kw = 3 * blk
    n = pl.program_id(1)
    q0 = pl.program_id(2) * tq
    rel = (lax.broadcasted_iota(jnp.int32, (blk, kw), 1) - lax.broadcasted_iota(jnp.int32, (blk, kw), 0))
    ones_blk = _ones_block(kw)
    heads = [n * C_GROUP + g for g in range(C_GROUP)]
    slope_rows = jnp.concatenate([jnp.full((blk, kw), slopes_ref[hd], F32) for hd in heads], axis=0)
    sink_rows = jnp.concatenate([jnp.full((blk, LANES), sink_ref[hd], F32) for hd in heads], axis=0)
    for sb in range(tq // blk):
        t0 = q0 + sb * blk
        start = pl.multiple_of(jnp.clip(t0 - blk, 0, seq - kw), blk)
        kc = k_ref[pl.ds(start, kw), :]
        v_aug = jnp.concatenate([v_ref[pl.ds(start, kw), :], ones_blk], axis=1)
        dist = jnp.abs(rel + (start - t0))
        dist_rows = jnp.concatenate([dist] * C_GROUP, axis=0)
        qs = jnp.concatenate([q_ref[sb * blk:(sb + 1) * blk, g * C_HEAD_DIM:(g + 1) * C_HEAD_DIM]
                              for g in range(C_GROUP)], axis=0)
        s = lax.dot_general(qs, kc, _NT, preferred_element_type=F32) - slope_rows * dist_rows.astype(F32)
        s = jnp.where(dist_rows <= WINDOW, s, NEG_INF)
        m = jnp.maximum(jnp.max(s, axis=1, keepdims=True), sink_rows)
        e = jnp.exp(s - jnp.concatenate([m] * (kw // LANES), axis=1))
        pv = jnp.dot(e.astype(v_aug.dtype), v_aug, preferred_element_type=F32)
        o = pv[:, :C_HEAD_DIM] / (pv[:, C_HEAD_DIM:] + jnp.exp(sink_rows - m))
        for g in range(C_GROUP):
            o_ref[sb * blk:(sb + 1) * blk, g * C_HEAD_DIM:(g + 1) * C_HEAD_DIM] = (
                o[g * blk:(g + 1) * blk].astype(o_ref.dtype))


def window_attn(p_c, slopes, sink, bsz, seq):
    tq = 512
    gw = C_GROUP * C_HEAD_DIM
    nq = C_HEADS * C_HEAD_DIM // LANES
    return pl.pallas_call(
        _window_kernel,
        grid=(bsz, C_KV_HEADS, seq // tq),
        in_specs=[pl.BlockSpec(memory_space=pltpu.SMEM),
                  pl.BlockSpec(memory_space=pltpu.SMEM),
                  pl.BlockSpec((None, tq, gw), lambda b, n, i: (b, i, n)),
                  pl.BlockSpec((None, seq, C_HEAD_DIM), lambda b, n, i: (b, 0, nq + n)),
                  pl.BlockSpec((None, seq, C_HEAD_DIM), lambda b, n, i: (b, 0, nq + C_KV_HEADS + n))],
        out_specs=pl.BlockSpec((None, tq, gw), lambda b, n, i: (b, i, n)),
        out_shape=jax.ShapeDtypeStruct((bsz, seq, C_HEADS * C_HEAD_DIM), BF16),
        compiler_params=_cparams(("parallel", "parallel", "parallel")),
        name="window_attn",
    )(slopes, sink, p_c, p_c, p_c)


def _swiglu_step(x, wg_ref, wu_ref, wd_ref, acc_ref):
    g = jnp.dot(x, wg_ref[...], preferred_element_type=F32)
    u = jnp.dot(x, wu_ref[...], preferred_element_type=F32)
    a = (g * jax.nn.sigmoid(g) * u).astype(BF16)
    acc_ref[...] += jnp.dot(a, wd_ref[...], preferred_element_type=F32)


def _ffn_kernel(x_ref, wg_ref, wu_ref, wd_ref, o_ref, acc_ref):
    j = pl.program_id(1)

    @pl.when(j == 0)
    def _():
        acc_ref[...] = jnp.zeros(acc_ref.shape, F32)

    _swiglu_step(x_ref[...], wg_ref, wu_ref, wd_ref, acc_ref)

    @pl.when(j == pl.num_programs(1) - 1)
    def _():
        o_ref[...] = acc_ref[...].astype(o_ref.dtype)


def ffn(h, wg, wu, wd):
    m, d = h.shape
    f = wg.shape[1]
    tm, tf = 1024, 512
    return pl.pallas_call(
        _ffn_kernel,
        grid=(m // tm, f // tf),
        in_specs=[pl.BlockSpec((tm, d), lambda i, j: (i, 0)),
                  pl.BlockSpec((d, tf), lambda i, j: (0, j)),
                  pl.BlockSpec((d, tf), lambda i, j: (0, j)),
                  pl.BlockSpec((tf, d), lambda i, j: (j, 0))],
        out_specs=pl.BlockSpec((tm, d), lambda i, j: (i, 0)),
        out_shape=jax.ShapeDtypeStruct((m, d), F32),
        scratch_shapes=[pltpu.VMEM((tm, d), F32)],
        compiler_params=_cparams(("parallel", "arbitrary")),
        name="ffn",
    )(h, wg, wu, wd)


def _moe_ffn_kernel(be_ref, nu_ref, tok_ref, tokn_ref, h_ref, wg_ref, wu_ref, wd_ref, o_ref,
                    xg_ref, xb_ref, acc_ref, sem):
    del be_ref
    i = pl.program_id(0)
    j = pl.program_id(1)
    rows = xg_ref.shape[0]
    n_used = nu_ref[0]

    def issue_gather(idx_ref):
        def body(r, carry):
            pltpu.make_async_copy(h_ref.at[pl.ds(idx_ref[0, r], 1)], xg_ref.at[pl.ds(r, 1)], sem).start()
            return carry
        lax.fori_loop(0, rows, body, 0, unroll=8)

    @pl.when((j == 0) & (i < n_used))
    def _():
        @pl.when(i == 0)
        def _():
            issue_gather(tok_ref)
        pltpu.make_async_copy(h_ref.at[pl.ds(0, rows)], xg_ref, sem).wait()
        xb_ref[...] = xg_ref[...].astype(BF16)
        acc_ref[...] = jnp.zeros(acc_ref.shape, F32)

    @pl.when((j == 1) & (i + 1 < n_used))
    def _():
        issue_gather(tokn_ref)

    @pl.when(i < n_used)
    def _():
        _swiglu_step(xb_ref[...], wg_ref, wu_ref, wd_ref, acc_ref)

    @pl.when((j == pl.num_programs(1) - 1) & (i < n_used))
    def _():
        o_ref[...] = acc_ref[...]

    @pl.when((j == pl.num_programs(1) - 1) & (i >= n_used))
    def _():
        o_ref[...] = jnp.zeros(o_ref.shape, F32)


def moe_ffn(block_e, n_used, row_tok, h, wg, wu, wd):
    d = h.shape[1]
    r = row_tok.shape[0]
    f = wg.shape[2]
    tm, tf = MOE_ROWS, 256
    nj = f // tf
    assert nj >= 2
    nblk = r // tm
    tok3 = row_tok.reshape(nblk, 1, tm)

    def w_col(i, j, be, nu):
        return jnp.where(i < nu[0], j, 0)

    grid_spec = pltpu.PrefetchScalarGridSpec(
        num_scalar_prefetch=2,
        grid=(nblk, nj),
        in_specs=[pl.BlockSpec((None, 1, tm), lambda i, j, be, nu: (i, 0, 0), memory_space=pltpu.SMEM),
                  pl.BlockSpec((None, 1, tm), lambda i, j, be, nu: (jnp.minimum(i + 1, nblk - 1), 0, 0),
                               memory_space=pltpu.SMEM),
                  pl.BlockSpec(memory_space=pl.ANY),
                  pl.BlockSpec((None, d, tf), lambda i, j, be, nu: (be[i], 0, w_col(i, j, be, nu))),
                  pl.BlockSpec((None, d, tf), lambda i, j, be, nu: (be[i], 0, w_col(i, j, be, nu))),
                  pl.BlockSpec((None, tf, d), lambda i, j, be, nu: (be[i], w_col(i, j, be, nu), 0))],
        out_specs=pl.BlockSpec((tm, d), lambda i, j, be, nu: (i, 0)),
        scratch_shapes=[pltpu.VMEM((tm, d), F32), pltpu.VMEM((tm, d), BF16),
                        pltpu.VMEM((tm, d), F32), pltpu.SemaphoreType.DMA(())],
    )
    return pl.pallas_call(
        _moe_ffn_kernel,
        grid_spec=grid_spec,
        out_shape=jax.ShapeDtypeStruct((r, d), F32),
        compiler_params=_cparams(("arbitrary", "arbitrary")),
        name="moe_ffn",
    )(block_e, n_used, tok3, tok3, h, wg, wu, wd)


def _router_kernel(h_ref, w_ref, b_ref, o_ref):
    logits = jnp.dot(h_ref[...], w_ref[...], preferred_element_type=F32,
                     precision=lax.Precision.HIGHEST) + b_ref[...]
    lane = lax.broadcasted_iota(jnp.int32, logits.shape, 1)
    logits = jnp.where(lane < N_EXPERTS, logits, NEG_INF)
    m1 = jnp.max(logits, axis=1, keepdims=True)
    i1 = jnp.min(jnp.where(logits == m1, lane, LANES), axis=1, keepdims=True)
    rest = jnp.where(lane == i1, NEG_INF, logits)
    m2 = jnp.max(rest, axis=1, keepdims=True)
    i2 = jnp.min(jnp.where(rest == m2, lane, LANES), axis=1, keepdims=True)
    e2 = jnp.exp(m2 - m1)
    g1 = 1.0 / (1.0 + e2)
    g2 = e2 / (1.0 + e2)
    out = jnp.where(lane == 0, i1.astype(F32),
                    jnp.where(lane == 1, i2.astype(F32),
                              jnp.where(lane == 2, g1, jnp.where(lane == 3, g2, 0.0))))
    o_ref[...] = out


def router(h, w_pad, b_pad):
    m, d = h.shape
    tm = 512
    return pl.pallas_call(
        _router_kernel,
        grid=(m // tm,),
        in_specs=[pl.BlockSpec((tm, d), lambda i: (i, 0)),
                  pl.BlockSpec((d, LANES), lambda i: (0, 0)),
                  pl.BlockSpec((1, LANES), lambda i: (0, 0))],
        out_specs=pl.BlockSpec((tm, LANES), lambda i: (i, 0)),
        out_shape=jax.ShapeDtypeStruct((m, LANES), F32),
        compiler_params=_cparams(("parallel",)),
        name="router",
    )(h, w_pad, b_pad)


def _alibi_slopes(n_heads):
    return jnp.asarray(2.0 ** (-8.0 * np.arange(1, n_heads + 1) / n_heads), dtype=F32)


def _bf16_parts(v):
    a = v.astype(BF16).astype(F32)
    b = (v - a).astype(BF16).astype(F32)
    c = (v - a - b).astype(BF16).astype(F32)
    return a, b, c


def _alibi_tables(seq, slopes2):
    nh = slopes2.shape[0]
    pos = jnp.arange(seq, dtype=jnp.int32)
    pos_parts = (((pos // LANES) * LANES).astype(F32), (pos % LANES).astype(F32))
    cols_q, cols_k = [], []
    for sp in _bf16_parts(slopes2):
        sp_col = jnp.broadcast_to(sp[:, None], (nh, seq))
        for pp in pos_parts:
            pp_row = jnp.broadcast_to(pp[None, :], (nh, seq))
            cols_q += [pp_row, sp_col]
            cols_k += [sp_col, -pp_row]
    pad = jnp.zeros((nh, seq, LANES - len(cols_q)), F32)
    eq = jnp.concatenate([jnp.stack(cols_q, axis=-1), pad], axis=-1).astype(BF16)
    ek = jnp.concatenate([jnp.stack(cols_k, axis=-1), pad], axis=-1).astype(BF16)
    return eq, ek


def _key_norm_bounds(p_ab, bsz, seq):
    k = p_ab.reshape(bsz, seq // FLASH_TK, FLASH_TK, -1)[..., A_COLS:2 * A_COLS].astype(F32)
    k = k.reshape(bsz, seq // FLASH_TK, FLASH_TK, A_HEADS, 2 * A_QK_DIM)
    norms = jnp.sqrt(jnp.max(jnp.sum(k * k, axis=-1), axis=2))
    return (1.01 * jnp.transpose(norms, (0, 2, 1))).reshape(-1)


def _rope_tables(seq):
    inv_freq = ROPE_THETA ** (-jnp.arange(0, B_ROPE, 2, dtype=F32) / B_ROPE)
    ang = jnp.arange(seq, dtype=F32)[:, None] * inv_freq[None, :]
    cos, sin = jnp.cos(ang), jnp.sin(ang)
    cos2 = jnp.concatenate([cos, cos], axis=-1)
    sin2 = jnp.concatenate([sin, sin], axis=-1)
    z64 = jnp.zeros((seq, B_ROPE), F32)
    ka = jnp.concatenate([cos2, z64], axis=-1)
    kb = jnp.concatenate([sin2, z64], axis=-1)
    ones = jnp.ones((seq, B_NOPE), F32)
    z128 = jnp.zeros((seq, B_NOPE), F32)
    qa = jnp.tile(jnp.concatenate([ones, cos2, z64], axis=-1), (1, B_HEADS))
    qb = jnp.tile(jnp.concatenate([z128, sin2, z64], axis=-1), (1, B_HEADS))
    return ka, kb, qa, qb


def _rot_cols(w):
    half = w.shape[-1] // 2
    return jnp.concatenate([-w[..., half:], w[..., :half]], axis=-1)


def _routing_tables(flat_e):
    n_assign = flat_e.shape[0]
    onehot = (flat_e[:, None] == jnp.arange(N_EXPERTS, dtype=jnp.int32)[None, :]).astype(jnp.int32)
    csum = jnp.cumsum(onehot, axis=0)
    rank = jnp.sum((csum - onehot) * onehot, axis=1)
    sizes = csum[-1]
    padded = ((sizes + MOE_ROWS - 1) // MOE_ROWS) * MOE_ROWS
    pad_end = jnp.cumsum(padded)
    pad_start = pad_end - padded
    dest = (pad_start[flat_e] + rank).astype(jnp.int32)
    n_blocks = -(-n_assign // MOE_ROWS) + N_EXPERTS
    n_rows = n_blocks * MOE_ROWS
    tok = jnp.arange(n_assign, dtype=jnp.int32) // TOP_K
    row_tok = jnp.zeros((n_rows,), jnp.int32).at[dest].set(tok)
    block_start = jnp.arange(n_blocks, dtype=jnp.int32) * MOE_ROWS
    block_e = jnp.minimum(jnp.searchsorted(pad_end, block_start, side='right'), N_EXPERTS - 1).astype(jnp.int32)
    n_used = (pad_end[-1:] // MOE_ROWS).astype(jnp.int32)
    return dest, row_tok, block_e, n_used


def kernel(x, c, ada_w, ada_b, ln_g, ln_b, ab_w_in, ab_lam, ab_diff_g, ab_q_norm_g, ab_kv_norm_g, ab_w_uq,
           ab_w_ukv, ab_w_out, c_w_in, c_sink, c_w_out, ffn_w_gate, ffn_w_up, ffn_w_down, moe_w_router,
           moe_b_router, moe_w_gate, moe_w_up, moe_w_down):
    bsz, seq, d = x.shape
    depth = ada_w.shape[0]
    n_tok = bsz * seq
    alpha = (2 * depth) ** 0.25

    c_pad = jnp.zeros((8, d), F32).at[:bsz].set(c)
    mods = adaln_all(c_pad, ada_w.reshape(2 * depth, d, 3 * d), ada_b.reshape(2 * depth, 1, 3 * d))
    mods = mods.reshape(2 * depth, 8, 3, d)

    ka, kb, qa, qb = _rope_tables(seq)
    slopes_a2 = _alibi_slopes(A_HEADS) * LOG2E
    eq_a, ek_a = _alibi_tables(seq, slopes_a2)
    slopes_c = _alibi_slopes(C_HEADS)
    scale_a = A_QK_DIM ** -0.5 * LOG2E
    scale_b = (B_NOPE + B_ROPE) ** -0.5 * LOG2E
    scale_c = C_HEAD_DIM ** -0.5

    h = modulate(x, mods, 0, BF16)
    for layer in range(depth):
        i = layer // 2
        l2 = 2 * layer
        last = layer == depth - 1
        if layer % 2 == 0:
            lam_init = 0.8 - 0.6 * math.exp(-0.3 * layer)
            w_in = ab_w_in[i]
            w_kr = w_in[:, 3 * A_COLS + B_Q_LORA + B_KV_LORA:]
            w_in2 = jnp.concatenate([w_in[:, :A_COLS] * scale_a, w_in[:, A_COLS:], _rot_cols(w_kr)],
                                    axis=1).astype(BF16)
            p_ab = inproj_ab(h.reshape(n_tok, d), w_in2, ka, kb, seq)
            o_a = flash_diff(p_ab.reshape(bsz, seq, -1), eq_a, ek_a, slopes_a2, _key_norm_bounds(p_ab, bsz, seq),
                             ab_lam[i], ab_diff_g[i].reshape(1, -1), lam_init, bsz, seq)
            w_uq = ab_w_uq[i].reshape(B_Q_LORA, B_HEADS, B_NOPE + B_ROPE) * scale_b
            w_uq2 = jnp.concatenate([w_uq, _rot_cols(w_uq[..., B_NOPE:])], axis=-1)
            w_uq2 = w_uq2.reshape(B_Q_LORA, B_HEADS * 2 * LANES).astype(BF16)
            q_cat = uq_proj(p_ab, 3 * A_COLS // B_Q_LORA, ab_q_norm_g[i].reshape(1, -1), w_uq2, qa, qb, seq)
            k_cat, v_b = ukv_proj(p_ab, (3 * A_COLS + B_Q_LORA) // B_KV_LORA,
                                  (3 * A_COLS + B_Q_LORA + B_KV_LORA) // LANES,
                                  ab_kv_norm_g[i].reshape(1, -1), ab_w_ukv[i].astype(BF16))
            o_b = flash_mla(q_cat.reshape(bsz, seq, -1), k_cat.reshape(bsz, seq, -1),
                            v_b.reshape(bsz, seq, -1), bsz, seq)
            acts, w_out = [o_a, o_b], ab_w_out[i]
        else:
            w_in = c_w_in[i]
            nqc = C_HEADS * C_HEAD_DIM
            w_in2 = jnp.concatenate([w_in[:, :nqc] * scale_c, w_in[:, nqc:]], axis=1).astype(BF16)
            p_c = matmul(h.reshape(n_tok, d), w_in2, BF16, tn=1024)
            o_c = window_attn(p_c.reshape(bsz, seq, -1), slopes_c, c_sink[i].astype(F32), bsz, seq)
            acts, w_out = [o_c], c_w_out[i]
        moe_next = layer % 2 == 1
        x, h = outproj_norm(acts, w_out.astype(BF16), x, mods, l2, ln_g[layer, 0], ln_b[layer, 0], alpha,
                            F32 if moe_next else BF16)

        if not moe_next:
            y = ffn(h.reshape(n_tok, d), ffn_w_gate[i].astype(BF16), ffn_w_up[i].astype(BF16),
                    ffn_w_down[i].astype(BF16)).reshape(bsz, seq, d)
            route = dest = None
        else:
            hf = h.reshape(n_tok, d)
            w_r = jnp.zeros((d, LANES), F32).at[:, :N_EXPERTS].set(moe_w_router[i])
            b_r = jnp.zeros((1, LANES), F32).at[0, :N_EXPERTS].set(moe_b_router[i])
            route = router(hf, w_r, b_r)
            flat_e = route[:, :TOP_K].astype(jnp.int32).reshape(-1)
            dest, row_tok, block_e, n_used = _routing_tables(flat_e)
            y = moe_ffn(block_e, n_used, row_tok, hf, moe_w_gate[i].astype(BF16), moe_w_up[i].astype(BF16),
                        moe_w_down[i].astype(BF16))
            route = route.reshape(bsz, seq, LANES)
        x, h = postnorm(x, y, mods, l2 + 1, ln_g[layer, 1], ln_b[layer, 1], alpha,
                        route=route, dest=dest, next_dtype=None if last else BF16)
    return x
```

```python
import functools
import math

import numpy as np
import jax
import jax.numpy as jnp
from jax import lax
from jax.experimental import pallas as pl
from jax.experimental.pallas import tpu as pltpu

F32 = jnp.float32
BF16 = jnp.bfloat16

A_HEADS = 4
A_QK_DIM = 64
A_V_DIM = 128
B_HEADS = 4
B_Q_LORA = 512
B_KV_LORA = 256
B_NOPE = 128
B_ROPE = 64
B_V_DIM = 128
ROPE_THETA = 10000.0
C_HEADS = 16
C_KV_HEADS = 4
C_GROUP = C_HEADS // C_KV_HEADS
C_HEAD_DIM = 128
WINDOW = 128
N_EXPERTS = 8
TOP_K = 2
MOE_ROWS = 1024
LN_EPS = 1e-5
RMS_EPS = 1e-6

A_COLS = A_HEADS * 2 * A_QK_DIM
LANES = 128
VMEM_LIMIT = 56 * 1024 * 1024
NEG_INF = float("-inf")
LOG2E = math.log2(math.e)
EXP2_UNDERFLOW = -151.0
FLASH_TK = 1024


def _cparams(sem, vmem=VMEM_LIMIT):
    return pltpu.CompilerParams(dimension_semantics=sem, vmem_limit_bytes=vmem)


def _adaln_kernel(c_ref, w_ref, b_ref, o_ref):
    c = c_ref[...]
    ca = c * jax.nn.sigmoid(c)
    o_ref[...] = jnp.dot(ca, w_ref[...], preferred_element_type=F32,
                         precision=lax.Precision.HIGHEST) + b_ref[...]


def adaln_all(c_pad, ada_w, ada_b):
    l2, d, n = ada_w.shape
    tn = 768
    return pl.pallas_call(
        _adaln_kernel,
        grid=(l2, n // tn),
        in_specs=[pl.BlockSpec((8, d), lambda l, j: (0, 0)),
                  pl.BlockSpec((None, d, tn), lambda l, j: (l, 0, j)),
                  pl.BlockSpec((None, 1, tn), lambda l, j: (l, 0, j))],
        out_specs=pl.BlockSpec((None, 8, tn), lambda l, j: (l, 0, j)),
        out_shape=jax.ShapeDtypeStruct((l2, 8, n), F32),
        compiler_params=_cparams(("parallel", "parallel")),
        name="adaln",
    )(c_pad, ada_w, ada_b)


def _modulate_kernel(x_ref, mod_ref, h_ref):
    x = x_ref[...]
    h_ref[...] = (x * (1.0 + mod_ref[1:2, :]) + mod_ref[0:1, :]).astype(h_ref.dtype)


def modulate(x, mods, l2, out_dtype):
    b, s, d = x.shape
    ts = 512
    return pl.pallas_call(
        _modulate_kernel,
        grid=(b, s // ts),
        in_specs=[pl.BlockSpec((None, ts, d), lambda i, j: (i, j, 0)),
                  pl.BlockSpec((None, None, 3, d), lambda i, j: (l2, i, 0, 0))],
        out_specs=pl.BlockSpec((None, ts, d), lambda i, j: (i, j, 0)),
        out_shape=jax.ShapeDtypeStruct((b, s, d), out_dtype),
        compiler_params=_cparams(("parallel", "parallel")),
        name="modulate",
    )(x, mods)


def _residual_norm(x, y, modc_ref, g_ref, b_ref, modn_ref, xo_ref, h_ref, alpha):
    z = alpha * x + (1.0 + modc_ref[2:3, :]) * y
    mu = jnp.mean(z, axis=-1, keepdims=True)
    zc = z - mu
    var = jnp.mean(zc * zc, axis=-1, keepdims=True)
    xn = zc * lax.rsqrt(var + LN_EPS) * g_ref[...] + b_ref[...]
    xo_ref[...] = xn
    if h_ref is not None:
        h_ref[...] = (xn * (1.0 + modn_ref[1:2, :]) + modn_ref[0:1, :]).astype(h_ref.dtype)


def _postnorm_kernel(*refs, alpha, routed, has_next):
    refs = list(refs)
    x_ref, y_ref, modc_ref, g_ref, b_ref = refs[:5]
    rest = refs[5:]
    if routed:
        route_ref, idx_ref, idxn_ref = rest[:3]
        rest = rest[3:]
    modn_ref = rest.pop(0) if has_next else None
    xo_ref = rest.pop(0)
    h_ref = rest.pop(0) if has_next else None
    x = x_ref[...]
    ts = x.shape[0]
    if routed:
        yg_ref, sem = rest
        step = pl.program_id(0) * pl.num_programs(1) + pl.program_id(1)
        n_steps = pl.num_programs(0) * pl.num_programs(1)
        slot = step % 2

        def issue_gather(idx, dst_slot):
            def body(r, carry):
                pltpu.make_async_copy(y_ref.at[pl.ds(idx[0, r], 1)], yg_ref.at[dst_slot, pl.ds(r, 1)],
                                      sem.at[dst_slot]).start()
                return carry
            lax.fori_loop(0, TOP_K * ts, body, 0, unroll=8)

        @pl.when(step == 0)
        def _():
            issue_gather(idx_ref, 0)

        @pl.when(step + 1 < n_steps)
        def _():
            issue_gather(idxn_ref, 1 - slot)

        pltpu.make_async_copy(y_ref.at[pl.ds(0, TOP_K * ts)], yg_ref.at[slot], sem.at[slot]).wait()
        g1 = route_ref[:, TOP_K:TOP_K + 1]
        g2 = route_ref[:, TOP_K + 1:TOP_K + 2]
        y = g1 * yg_ref[slot, 0:ts, :] + g2 * yg_ref[slot, ts:2 * ts, :]
    else:
        y = y_ref[...].astype(F32)
    _residual_norm(x, y, modc_ref, g_ref, b_ref, modn_ref, xo_ref, h_ref, alpha)


def postnorm(x, y, mods, l2, ln_g, ln_b, alpha, *, route=None, dest=None, next_dtype=None):
    b, s, d = x.shape
    ts = 256
    has_next = next_dtype is not None
    routed = route is not None
    nj = s // ts
    if routed:
        y_spec = pl.BlockSpec(memory_space=pl.ANY)
    else:
        y_spec = pl.BlockSpec((None, ts, d), lambda i, j: (i, j, 0))
    in_specs = [pl.BlockSpec((None, ts, d), lambda i, j: (i, j, 0)),
                y_spec,
                pl.BlockSpec((None, None, 3, d), lambda i, j: (l2, i, 0, 0)),
                pl.BlockSpec((1, d), lambda i, j: (0, 0)),
                pl.BlockSpec((1, d), lambda i, j: (0, 0))]
    args = [x, y, mods, ln_g.reshape(1, d), ln_b.reshape(1, d)]
    scratch = []
    if routed:
        n_steps = b * nj
        idx = dest.reshape(n_steps, ts, TOP_K).transpose(0, 2, 1).reshape(n_steps, 1, TOP_K * ts)
        in_specs += [pl.BlockSpec((None, ts, LANES), lambda i, j: (i, j, 0)),
                     pl.BlockSpec((None, 1, TOP_K * ts), lambda i, j: (i * nj + j, 0, 0),
                                  memory_space=pltpu.SMEM),
                     pl.BlockSpec((None, 1, TOP_K * ts),
                                  lambda i, j: (jnp.minimum(i * nj + j + 1, n_steps - 1), 0, 0),
                                  memory_space=pltpu.SMEM)]
        args += [route, idx, idx]
        scratch = [pltpu.VMEM((2, TOP_K * ts, d), F32), pltpu.SemaphoreType.DMA((2,))]
    out_specs = [pl.BlockSpec((None, ts, d), lambda i, j: (i, j, 0))]
    out_shape = [jax.ShapeDtypeStruct((b, s, d), F32)]
    if has_next:
        in_specs.append(pl.BlockSpec((None, None, 3, d), lambda i, j: (l2 + 1, i, 0, 0)))
        args.append(mods)
        out_specs.append(pl.BlockSpec((None, ts, d), lambda i, j: (i, j, 0)))
        out_shape.append(jax.ShapeDtypeStruct((b, s, d), next_dtype))
    sem = ("arbitrary", "arbitrary") if routed else ("parallel", "parallel")
    out = pl.pallas_call(
        functools.partial(_postnorm_kernel, alpha=alpha, routed=routed, has_next=has_next),
        grid=(b, nj),
        in_specs=in_specs,
        out_specs=out_specs,
        out_shape=out_shape,
        scratch_shapes=scratch,
        compiler_params=_cparams(sem),
        name="postnorm",
    )(*args)
    return (out[0], out[1]) if has_next else (out[0], None)


def _rms_rows(a_ref, g_ref):
    a = a_ref[...].astype(F32)
    return (a * lax.rsqrt(jnp.mean(a * a, axis=-1, keepdims=True) + RMS_EPS) * g_ref[...]).astype(BF16)


def _matmul_kernel(a_ref, w_ref, o_ref):
    o_ref[...] = jnp.dot(a_ref[...], w_ref[...], preferred_element_type=F32).astype(o_ref.dtype)


def matmul(a, w, out_dtype, tm=512, tn=None):
    m, k = a.shape
    n = w.shape[1]
    tn = n if tn is None else tn
    return pl.pallas_call(
        _matmul_kernel,
        grid=(m // tm, n // tn),
        in_specs=[pl.BlockSpec((tm, k), lambda i, j: (i, 0)),
                  pl.BlockSpec((k, tn), lambda i, j: (0, j))],
        out_specs=pl.BlockSpec((tm, tn), lambda i, j: (i, j)),
        out_shape=jax.ShapeDtypeStruct((m, n), out_dtype),
        compiler_params=_cparams(("parallel", "parallel")),
        name="matmul",
    )(a, w)


def _outproj_norm_kernel(*refs, n_in, alpha):
    a_refs, w_refs = refs[:n_in], refs[n_in:2 * n_in]
    x_ref, modc_ref, g_ref, b_ref, modn_ref, xo_ref, h_ref = refs[2 * n_in:]
    y = jnp.dot(a_refs[0][...], w_refs[0][...], preferred_element_type=F32)
    for a_ref, w_ref in zip(a_refs[1:], w_refs[1:]):
        y += jnp.dot(a_ref[...], w_ref[...], preferred_element_type=F32)
    _residual_norm(x_ref[...], y, modc_ref, g_ref, b_ref, modn_ref, xo_ref, h_ref, alpha)


def outproj_norm(acts, w, x, mods, l2, ln_g, ln_b, alpha, next_dtype):
    b, s, d = x.shape
    k = acts[0].shape[-1]
    assert all(a.shape[-1] == k for a in acts) and w.shape[0] == k * len(acts)
    ts = 512
    n_in = len(acts)
    row = lambda i, j: (i, j, 0)
    in_specs = ([pl.BlockSpec((None, ts, k), row) for _ in acts]
                + [pl.BlockSpec((k, d), functools.partial(lambda p, i, j: (p, 0), p)) for p in range(n_in)]
                + [pl.BlockSpec((None, ts, d), row),
                   pl.BlockSpec((None, None, 3, d), lambda i, j: (l2, i, 0, 0)),
                   pl.BlockSpec((1, d), lambda i, j: (0, 0)),
                   pl.BlockSpec((1, d), lambda i, j: (0, 0)),
                   pl.BlockSpec((None, None, 3, d), lambda i, j: (l2 + 1, i, 0, 0))])
    out = pl.pallas_call(
        functools.partial(_outproj_norm_kernel, n_in=n_in, alpha=alpha),
        grid=(b, s // ts),
        in_specs=in_specs,
        out_specs=[pl.BlockSpec((None, ts, d), row), pl.BlockSpec((None, ts, d), row)],
        out_shape=[jax.ShapeDtypeStruct((b, s, d), F32), jax.ShapeDtypeStruct((b, s, d), next_dtype)],
        compiler_params=_cparams(("parallel", "parallel")),
        name="outproj_norm",
    )(*acts, *([w] * n_in), x, mods, ln_g.reshape(1, d), ln_b.reshape(1, d), mods)
    return out[0], out[1]


def _inproj_ab_kernel(a_ref, w_ref, ra_ref, rb_ref, o_ref):
    acc = jnp.dot(a_ref[...], w_ref[...], preferred_element_type=F32)
    n = acc.shape[1]
    o_ref[:, :n - LANES] = acc[:, :n - LANES].astype(o_ref.dtype)
    last = acc[:, n - LANES:]
    roped = last * ra_ref[...] + pltpu.roll(last, LANES // 2, 1) * rb_ref[...]
    o_ref[:, n - LANES:] = roped.astype(o_ref.dtype)


def inproj_ab(h, w, rope_a, rope_b, seq):
    m, k = h.shape
    n = w.shape[1]
    tm = 512
    nsb = seq // tm
    return pl.pallas_call(
        _inproj_ab_kernel,
        grid=(m // tm,),
        in_specs=[pl.BlockSpec((tm, k), lambda i: (i, 0)),
                  pl.BlockSpec((k, n), lambda i: (0, 0)),
                  pl.BlockSpec((tm, LANES), lambda i: (i % nsb, 0)),
                  pl.BlockSpec((tm, LANES), lambda i: (i % nsb, 0))],
        out_specs=pl.BlockSpec((tm, n), lambda i: (i, 0)),
        out_shape=jax.ShapeDtypeStruct((m, n), BF16),
        compiler_params=_cparams(("parallel",)),
        name="inproj_ab",
    )(h, w, rope_a, rope_b)


def _uq_kernel(a_ref, g_ref, w_ref, ra_ref, rb_ref, o_ref):
    an = _rms_rows(a_ref, g_ref)
    acc = jnp.dot(an, w_ref[...], preferred_element_type=F32)
    n = acc.shape[1]
    o_ref[...] = (acc * ra_ref[...] + pltpu.roll(acc, n - B_ROPE, 1) * rb_ref[...]).astype(o_ref.dtype)


def uq_proj(p_ab, col_block, g, w, rope_a, rope_b, seq):
    m = p_ab.shape[0]
    k, n = w.shape
    tm = 512
    nsb = seq // tm
    return pl.pallas_call(
        _uq_kernel,
        grid=(m // tm,),
        in_specs=[pl.BlockSpec((tm, k), lambda i: (i, col_block)),
                  pl.BlockSpec((1, k), lambda i: (0, 0)),
                  pl.BlockSpec((k, n), lambda i: (0, 0)),
                  pl.BlockSpec((tm, n), lambda i: (i % nsb, 0)),
                  pl.BlockSpec((tm, n), lambda i: (i % nsb, 0))],
        out_specs=pl.BlockSpec((tm, n), lambda i: (i, 0)),
        out_shape=jax.ShapeDtypeStruct((m, n), BF16),
        compiler_params=_cparams(("parallel",)),
        name="uq_proj",
    )(p_ab, g, w, rope_a, rope_b)


def _ukv_kernel(a_ref, g_ref, w_ref, kr_ref, k_ref, v_ref):
    an = _rms_rows(a_ref, g_ref)
    acc = jnp.dot(an, w_ref[...], preferred_element_type=F32)
    kr = kr_ref[...]
    for h in range(B_HEADS):
        base = h * (B_NOPE + B_V_DIM)
        k_ref[:, 2 * h * LANES:(2 * h + 1) * LANES] = acc[:, base:base + B_NOPE].astype(k_ref.dtype)
        k_ref[:, (2 * h + 1) * LANES:(2 * h + 2) * LANES] = kr
        v_ref[:, h * B_V_DIM:(h + 1) * B_V_DIM] = acc[:, base + B_NOPE:base + B_NOPE + B_V_DIM].astype(v_ref.dtype)


def ukv_proj(p_ab, ckv_block, kr_block, g, w):
    m = p_ab.shape[0]
    k, n = w.shape
    tm = 512
    return pl.pallas_call(
        _ukv_kernel,
        grid=(m // tm,),
        in_specs=[pl.BlockSpec((tm, k), lambda i: (i, ckv_block)),
                  pl.BlockSpec((1, k), lambda i: (0, 0)),
                  pl.BlockSpec((k, n), lambda i: (0, 0)),
                  pl.BlockSpec((tm, LANES), lambda i: (i, kr_block))],
        out_specs=[pl.BlockSpec((tm, B_HEADS * 2 * LANES), lambda i: (i, 0)),
                   pl.BlockSpec((tm, B_HEADS * B_V_DIM), lambda i: (i, 0))],
        out_shape=[jax.ShapeDtypeStruct((m, B_HEADS * 2 * LANES), BF16),
                   jax.ShapeDtypeStruct((m, B_HEADS * B_V_DIM), BF16)],
        compiler_params=_cparams(("parallel",)),
        name="ukv_proj",
    )(p_ab, g, w, p_ab)


_NT = (((1,), (1,)), ((), ()))


def _ones_block(rows):
    return jnp.ones((rows, LANES), BF16)


def _online_softmax_step(s, v_aug, m_ref, acc_ref):
    tk = s.shape[1]
    m_prev = m_ref[...]
    m_new = jnp.maximum(m_prev, jnp.max(s, axis=1, keepdims=True))
    alpha = jnp.exp2(m_prev - m_new)
    p = jnp.exp2(s - jnp.concatenate([m_new] * (tk // LANES), axis=1))
    pv = jnp.dot(p.astype(v_aug.dtype), v_aug, preferred_element_type=F32)
    acc_ref[...] = jnp.concatenate([alpha, alpha], axis=1) * acc_ref[...] + pv
    m_ref[...] = m_new


def _softmax_result(acc_ref):
    acc = acc_ref[...]
    return acc[:, :LANES] / acc[:, LANES:]


def _flash_diff_kernel(slopes_ref, kn_ref, q_ref, eq_ref, k_ref, ek_ref, v_ref, lam_ref, g_ref, o_ref,
                       qaug_sc, m_sc, acc_sc, *, tk, lam_init, splits):
    tq = q_ref.shape[0]
    seq = k_ref.shape[0]
    n_chunks = seq // tk
    qi = pl.program_id(2)
    q0 = qi * tq
    q = q_ref[...]
    eq = eq_ref[...]
    qf = q.astype(F32)
    q_norm = jnp.sqrt(jnp.max(jnp.sum(qf * qf, axis=1, keepdims=True), axis=0, keepdims=True))[0, 0]
    kn_base = (pl.program_id(0) * pl.num_programs(1) + pl.program_id(1)) * n_chunks
    lane = lax.broadcasted_iota(jnp.int32, q.shape, 1)
    zero = jnp.zeros_like(q)
    for mp in range(2):
        qm = jnp.where((lane < A_QK_DIM) == (mp == 0), q, zero)
        qaug_sc[0, mp] = jnp.concatenate([qm, -eq], axis=1)
        qaug_sc[1, mp] = jnp.concatenate([qm, eq], axis=1)
    m_sc[...] = jnp.full(m_sc.shape, NEG_INF, F32)
    acc_sc[...] = jnp.zeros(acc_sc.shape, F32)
    ones_col = _ones_block(tk)

    def chunk(c, side, diagonal):
        k0 = pl.multiple_of(c * tk, tk)
        kc = jnp.concatenate([k_ref[pl.ds(k0, tk), :], ek_ref[pl.ds(k0, tk), :]], axis=1)
        v_aug = jnp.concatenate([v_ref[pl.ds(k0, tk), :], ones_col], axis=1)
        if diagonal:
            rel = (lax.broadcasted_iota(jnp.int32, (tq, tk), 0)
                   - lax.broadcasted_iota(jnp.int32, (tq, tk), 1)) + (q0 - k0)
            corr = (2.0 * slopes_ref[pl.program_id(1)]) * jnp.maximum(rel, 0).astype(F32)
        part = tq // splits
        for hh in range(splits):
            rows = pl.ds(hh * part, part)
            for mp in range(2):
                s = lax.dot_general(qaug_sc[side, mp, rows, :], kc, _NT, preferred_element_type=F32)
                if diagonal:
                    s = s - corr[hh * part:(hh + 1) * part]
                _online_softmax_step(s, v_aug, m_sc.at[mp, rows], acc_sc.at[mp, rows])

    cd = q0 // tk
    chunk(cd, 1, True)
    m_floor = jnp.min(jnp.min(jnp.minimum(m_sc[0], m_sc[1]), axis=1, keepdims=True),
                      axis=0, keepdims=True)[0, 0]
    slope = slopes_ref[pl.program_id(1)]

    def off_diagonal(idx, carry):
        c = idx + (idx >= cd).astype(jnp.int32)
        gap = (jnp.abs(c - cd) - 1) * tk + 1
        bound = q_norm * kn_ref[kn_base + c] - slope * gap.astype(F32)

        @pl.when(bound - m_floor > EXP2_UNDERFLOW)
        def _():
            chunk(c, (c > cd).astype(jnp.int32), False)
        return carry

    lax.fori_loop(0, n_chunks - 1, off_diagonal, 0, unroll=2)

    o0 = _softmax_result(acc_sc.at[0])
    o1 = _softmax_result(acc_sc.at[1])
    lam = lam_ref[...]
    lam_full = (jnp.exp(jnp.sum(lam[0:1] * lam[1:2], axis=1, keepdims=True))
                - jnp.exp(jnp.sum(lam[2:3] * lam[3:4], axis=1, keepdims=True)) + lam_init)
    o = o0 - lam_full * o1
    on = o * lax.rsqrt(jnp.mean(o * o, axis=-1, keepdims=True) + RMS_EPS) * g_ref[...]
    o_ref[...] = (on * (1.0 - lam_init)).astype(o_ref.dtype)


def flash_diff(p_ab, eq, ek, slopes2, k_norms, lam, diff_g, lam_init, bsz, seq):
    tq, tk = FLASH_TK, FLASH_TK
    nh = A_HEADS
    return pl.pallas_call(
        functools.partial(_flash_diff_kernel, tk=tk, lam_init=lam_init, splits=4),
        grid=(bsz, nh, seq // tq),
        in_specs=[pl.BlockSpec(memory_space=pltpu.SMEM),
                  pl.BlockSpec(memory_space=pltpu.SMEM),
                  pl.BlockSpec((None, tq, LANES), lambda b, h, i: (b, i, h)),
                  pl.BlockSpec((None, tq, LANES), lambda b, h, i: (h, i, 0)),
                  pl.BlockSpec((None, seq, LANES), lambda b, h, i: (b, 0, nh + h)),
                  pl.BlockSpec((None, seq, LANES), lambda b, h, i: (h, 0, 0)),
                  pl.BlockSpec((None, seq, LANES), lambda b, h, i: (b, 0, 2 * nh + h)),
                  pl.BlockSpec((4, A_QK_DIM), lambda b, h, i: (0, 0)),
                  pl.BlockSpec((1, A_V_DIM), lambda b, h, i: (0, 0))],
        out_specs=pl.BlockSpec((None, tq, A_V_DIM), lambda b, h, i: (b, i, h)),
        out_shape=jax.ShapeDtypeStruct((bsz, seq, nh * A_V_DIM), BF16),
        scratch_shapes=[pltpu.VMEM((2, 2, tq, 2 * LANES), BF16),
                        pltpu.VMEM((2, tq, LANES), F32),
                        pltpu.VMEM((2, tq, 2 * LANES), F32)],
        compiler_params=_cparams(("parallel", "parallel", "parallel")),
        name="flash_diff",
    )(slopes2, k_norms, p_ab, eq, p_ab, ek, p_ab, lam, diff_g)


def _flash_mla_kernel(q_ref, k_ref, v_ref, o_ref, m_sc, acc_sc, *, tk, splits):
    seq = k_ref.shape[0]
    part = q_ref.shape[0] // splits
    m_sc[...] = jnp.full(m_sc.shape, NEG_INF, F32)
    acc_sc[...] = jnp.zeros(acc_sc.shape, F32)
    ones_col = _ones_block(tk)

    def body(c, carry):
        k0 = pl.multiple_of(c * tk, tk)
        kc = k_ref[pl.ds(k0, tk), :]
        v_aug = jnp.concatenate([v_ref[pl.ds(k0, tk), :], ones_col], axis=1)
        for hh in range(splits):
            rows = pl.ds(hh * part, part)
            s = lax.dot_general(q_ref[rows, :], kc, _NT, preferred_element_type=F32)
            _online_softmax_step(s, v_aug, m_sc.at[rows], acc_sc.at[rows])
        return carry

    lax.fori_loop(0, seq // tk, body, 0, unroll=2)
    o_ref[...] = _softmax_result(acc_sc).astype(o_ref.dtype)


def flash_mla(q_cat, k_cat, v_b, bsz, seq):
    tq, tk = 1024, 1024
    kd = 2 * LANES
    return pl.pallas_call(
        functools.partial(_flash_mla_kernel, tk=tk, splits=4),
        grid=(bsz, B_HEADS, seq // tq),
        in_specs=[pl.BlockSpec((None, tq, kd), lambda b, h, i: (b, i, h)),
                  pl.BlockSpec((None, seq, kd), lambda b, h, i: (b, 0, h)),
                  pl.BlockSpec((None, seq, B_V_DIM), lambda b, h, i: (b, 0, h))],
        out_specs=pl.BlockSpec((None, tq, B_V_DIM), lambda b, h, i: (b, i, h)),
        out_shape=jax.ShapeDtypeStruct((bsz, seq, B_HEADS * B_V_DIM), BF16),
        scratch_shapes=[pltpu.VMEM((tq, LANES), F32), pltpu.VMEM((tq, 2 * LANES), F32)],
        compiler_params=_cparams(("parallel", "parallel", "parallel")),
        name="flash_mla",
    )(q_cat, k_cat, v_b)


def _window_kernel(slopes_ref, sink_ref, q_ref, k_ref, v_ref, o_ref):
    tq = q_ref.shape[0]
    seq = k_ref.shape[0]
    blk = WINDOW
    kw = 3 * blk
    n = pl.program_id(1)
    q0 = pl.program_id(2) * tq
    rel = (lax.broadcasted_iota(jnp.int32, (blk, kw), 1) - lax.broadcasted_iota(jnp.int32, (blk, kw), 0))
    ones_blk = _ones_block(kw)
    heads = [n * C_GROUP + g for g in range(C_GROUP)]
    slope_rows = jnp.concatenate([jnp.full((blk, kw), slopes_ref[hd], F32) for hd in heads], axis=0)
    sink_rows = jnp.concatenate([jnp.full((blk, LANES), sink_ref[hd], F32) for hd in heads], axis=0)
    for sb in range(tq // blk):
        t0 = q0 + sb * blk
        start = pl.multiple_of(jnp.clip(t0 - blk, 0, seq - kw), blk)
        kc = k_ref[pl.ds(start, kw), :]
        v_aug = jnp.concatenate([v_ref[pl.ds(start, kw), :], ones_blk], axis=1)
        dist = jnp.abs(rel + (start - t0))
        dist_rows = jnp.concatenate([dist] * C_GROUP, axis=0)
        qs = jnp.concatenate([q_ref[sb * blk:(sb + 1) * blk, g * C_HEAD_DIM:(g + 1) * C_HEAD_DIM]
                              for g in range(C_GROUP)], axis=0)
        s = lax.dot_general(qs, kc, _NT, preferred_element_type=F32) - slope_rows * dist_rows.astype(F32)
        s = jnp.where(dist_rows <= WINDOW, s, NEG_INF)
        m = jnp.maximum(jnp.max(s, axis=1, keepdims=True), sink_rows)
        e = jnp.exp(s - jnp.concatenate([m] * (kw // LANES), axis=1))
        pv = jnp.dot(e.astype(v_aug.dtype), v_aug, preferred_element_type=F32)
        o = pv[:, :C_HEAD_DIM] / (pv[:, C_HEAD_DIM:] + jnp.exp(sink_rows - m))
        for g in range(C_GROUP):
            o_ref[sb * blk:(sb + 1) * blk, g * C_HEAD_DIM:(g + 1) * C_HEAD_DIM] = (
                o[g * blk:(g + 1) * blk].astype(o_ref.dtype))


def window_attn(p_c, slopes, sink, bsz, seq):
    tq = 512
    gw = C_GROUP * C_HEAD_DIM
    nq = C_HEADS * C_HEAD_DIM // LANES
    return pl.pallas_call(
        _window_kernel,
        grid=(bsz, C_KV_HEADS, seq // tq),
        in_specs=[pl.BlockSpec(memory_space=pltpu.SMEM),
                  pl.BlockSpec(memory_space=pltpu.SMEM),
                  pl.BlockSpec((None, tq, gw), lambda b, n, i: (b, i, n)),
                  pl.BlockSpec((None, seq, C_HEAD_DIM), lambda b, n, i: (b, 0, nq + n)),
                  pl.BlockSpec((None, seq, C_HEAD_DIM), lambda b, n, i: (b, 0, nq + C_KV_HEADS + n))],
        out_specs=pl.BlockSpec((None, tq, gw), lambda b, n, i: (b, i, n)),
        out_shape=jax.ShapeDtypeStruct((bsz, seq, C_HEADS * C_HEAD_DIM), BF16),
        compiler_params=_cparams(("parallel", "parallel", "parallel")),
        name="window_attn",
    )(slopes, sink, p_c, p_c, p_c)


def _swiglu_step(x, wg_ref, wu_ref, wd_ref, acc_ref):
    g = jnp.dot(x, wg_ref[...].astype(BF16), preferred_element_type=F32)
    u = jnp.dot(x, wu_ref[...].astype(BF16), preferred_element_type=F32)
    a = (g * jax.nn.sigmoid(g) * u).astype(BF16)
    acc_ref[...] += jnp.dot(a, wd_ref[...].astype(BF16), preferred_element_type=F32)


def _ffn_kernel(x_ref, wg_ref, wu_ref, wd_ref, o_ref, acc_ref):
    j = pl.program_id(1)

    @pl.when(j == 0)
    def _():
        acc_ref[...] = jnp.zeros(acc_ref.shape, F32)

    _swiglu_step(x_ref[...], wg_ref, wu_ref, wd_ref, acc_ref)

    @pl.when(j == pl.num_programs(1) - 1)
    def _():
        o_ref[...] = acc_ref[...].astype(o_ref.dtype)


def ffn(h, wg, wu, wd):
    m, d = h.shape
    f = wg.shape[1]
    tm, tf = 1024, 512
    return pl.pallas_call(
        _ffn_kernel,
        grid=(m // tm, f // tf),
        in_specs=[pl.BlockSpec((tm, d), lambda i, j: (i, 0)),
                  pl.BlockSpec((d, tf), lambda i, j: (0, j)),
                  pl.BlockSpec((d, tf), lambda i, j: (0, j)),
                  pl.BlockSpec((tf, d), lambda i, j: (j, 0))],
        out_specs=pl.BlockSpec((tm, d), lambda i, j: (i, 0)),
        out_shape=jax.ShapeDtypeStruct((m, d), F32),
        scratch_shapes=[pltpu.VMEM((tm, d), F32)],
        compiler_params=_cparams(("parallel", "arbitrary")),
        name="ffn",
    )(h, wg, wu, wd)


def _moe_ffn_kernel(be_ref, nu_ref, tok_ref, tokn_ref, h_ref, wg_ref, wu_ref, wd_ref, o_ref,
                    xg_ref, xb_ref, acc_ref, sem):
    del be_ref
    i = pl.program_id(0)
    j = pl.program_id(1)
    rows = xg_ref.shape[0]
    n_used = nu_ref[0]

    def issue_gather(idx_ref):
        def body(r, carry):
            pltpu.make_async_copy(h_ref.at[pl.ds(idx_ref[0, r], 1)], xg_ref.at[pl.ds(r, 1)], sem).start()
            return carry
        lax.fori_loop(0, rows, body, 0, unroll=8)

    @pl.when((j == 0) & (i < n_used))
    def _():
        @pl.when(i == 0)
        def _():
            issue_gather(tok_ref)
        pltpu.make_async_copy(h_ref.at[pl.ds(0, rows)], xg_ref, sem).wait()
        xb_ref[...] = xg_ref[...].astype(BF16)
        acc_ref[...] = jnp.zeros(acc_ref.shape, F32)

    @pl.when((j == 1) & (i + 1 < n_used))
    def _():
        issue_gather(tokn_ref)

    @pl.when(i < n_used)
    def _():
        _swiglu_step(xb_ref[...], wg_ref, wu_ref, wd_ref, acc_ref)

    @pl.when((j == pl.num_programs(1) - 1) & (i < n_used))
    def _():
        o_ref[...] = acc_ref[...]

    @pl.when((j == pl.num_programs(1) - 1) & (i >= n_used))
    def _():
        o_ref[...] = jnp.zeros(o_ref.shape, F32)


def moe_ffn(block_e, n_used, row_tok, h, wg, wu, wd):
    d = h.shape[1]
    r = row_tok.shape[0]
    f = wg.shape[2]
    tm, tf = MOE_ROWS, 256
    nj = f // tf
    assert nj >= 2
    nblk = r // tm
    tok3 = row_tok.reshape(nblk, 1, tm)

    def w_col(i, j, be, nu):
        return jnp.where(i < nu[0], j, 0)

    grid_spec = pltpu.PrefetchScalarGridSpec(
        num_scalar_prefetch=2,
        grid=(nblk, nj),
        in_specs=[pl.BlockSpec((None, 1, tm), lambda i, j, be, nu: (i, 0, 0), memory_space=pltpu.SMEM),
                  pl.BlockSpec((None, 1, tm), lambda i, j, be, nu: (jnp.minimum(i + 1, nblk - 1), 0, 0),
                               memory_space=pltpu.SMEM),
                  pl.BlockSpec(memory_space=pl.ANY),
                  pl.BlockSpec((None, d, tf), lambda i, j, be, nu: (be[i], 0, w_col(i, j, be, nu))),
                  pl.BlockSpec((None, d, tf), lambda i, j, be, nu: (be[i], 0, w_col(i, j, be, nu))),
                  pl.BlockSpec((None, tf, d), lambda i, j, be, nu: (be[i], w_col(i, j, be, nu), 0))],
        out_specs=pl.BlockSpec((tm, d), lambda i, j, be, nu: (i, 0)),
        scratch_shapes=[pltpu.VMEM((tm, d), F32), pltpu.VMEM((tm, d), BF16),
                        pltpu.VMEM((tm, d), F32), pltpu.SemaphoreType.DMA(())],
    )
    return pl.pallas_call(
        _moe_ffn_kernel,
        grid_spec=grid_spec,
        out_shape=jax.ShapeDtypeStruct((r, d), F32),
        compiler_params=_cparams(("arbitrary", "arbitrary")),
        name="moe_ffn",
    )(block_e, n_used, tok3, tok3, h, wg, wu, wd)


def _router_kernel(h_ref, w_ref, b_ref, o_ref):
    logits = jnp.dot(h_ref[...], w_ref[...], preferred_element_type=F32,
                     precision=lax.Precision.HIGHEST) + b_ref[...]
    lane = lax.broadcasted_iota(jnp.int32, logits.shape, 1)
    logits = jnp.where(lane < N_EXPERTS, logits, NEG_INF)
    m1 = jnp.max(logits, axis=1, keepdims=True)
    i1 = jnp.min(jnp.where(logits == m1, lane, LANES), axis=1, keepdims=True)
    rest = jnp.where(lane == i1, NEG_INF, logits)
    m2 = jnp.max(rest, axis=1, keepdims=True)
    i2 = jnp.min(jnp.where(rest == m2, lane, LANES), axis=1, keepdims=True)
    e2 = jnp.exp(m2 - m1)
    g1 = 1.0 / (1.0 + e2)
    g2 = e2 / (1.0 + e2)
    out = jnp.where(lane == 0, i1.astype(F32),
                    jnp.where(lane == 1, i2.astype(F32),
                              jnp.where(lane == 2, g1, jnp.where(lane == 3, g2, 0.0))))
    o_ref[...] = out


def router(h, w_pad, b_pad):
    m, d = h.shape
    tm = 512
    return pl.pallas_call(
        _router_kernel,
        grid=(m // tm,),
        in_specs=[pl.BlockSpec((tm, d), lambda i: (i, 0)),
                  pl.BlockSpec((d, LANES), lambda i: (0, 0)),
                  pl.BlockSpec((1, LANES), lambda i: (0, 0))],
        out_specs=pl.BlockSpec((tm, LANES), lambda i: (i, 0)),
        out_shape=jax.ShapeDtypeStruct((m, LANES), F32),
        compiler_params=_cparams(("parallel",)),
        name="router",
    )(h, w_pad, b_pad)


def _alibi_slopes(n_heads):
    return jnp.asarray(2.0 ** (-8.0 * np.arange(1, n_heads + 1) / n_heads), dtype=F32)


def _bf16_parts(v):
    a = v.astype(BF16).astype(F32)
    b = (v - a).astype(BF16).astype(F32)
    c = (v - a - b).astype(BF16).astype(F32)
    return a, b, c


def _alibi_tables(seq, slopes2):
    nh = slopes2.shape[0]
    pos = jnp.arange(seq, dtype=jnp.int32)
    pos_parts = (((pos // LANES) * LANES).astype(F32), (pos % LANES).astype(F32))
    cols_q, cols_k = [], []
    for sp in _bf16_parts(slopes2):
        sp_col = jnp.broadcast_to(sp[:, None], (nh, seq))
        for pp in pos_parts:
            pp_row = jnp.broadcast_to(pp[None, :], (nh, seq))
            cols_q += [pp_row, sp_col]
            cols_k += [sp_col, -pp_row]
    pad = jnp.zeros((nh, seq, LANES - len(cols_q)), F32)
    eq = jnp.concatenate([jnp.stack(cols_q, axis=-1), pad], axis=-1).astype(BF16)
    ek = jnp.concatenate([jnp.stack(cols_k, axis=-1), pad], axis=-1).astype(BF16)
    return eq, ek


def _key_norm_bounds(p_ab, bsz, seq):
    k = p_ab.reshape(bsz, seq // FLASH_TK, FLASH_TK, -1)[..., A_COLS:2 * A_COLS].astype(F32)
    k = k.reshape(bsz, seq // FLASH_TK, FLASH_TK, A_HEADS, 2 * A_QK_DIM)
    norms = jnp.sqrt(jnp.max(jnp.sum(k * k, axis=-1), axis=2))
    return (1.01 * jnp.transpose(norms, (0, 2, 1))).reshape(-1)


def _rope_tables(seq):
    inv_freq = ROPE_THETA ** (-jnp.arange(0, B_ROPE, 2, dtype=F32) / B_ROPE)
    ang = jnp.arange(seq, dtype=F32)[:, None] * inv_freq[None, :]
    cos, sin = jnp.cos(ang), jnp.sin(ang)
    cos2 = jnp.concatenate([cos, cos], axis=-1)
    sin2 = jnp.concatenate([sin, sin], axis=-1)
    z64 = jnp.zeros((seq, B_ROPE), F32)
    ka = jnp.concatenate([cos2, z64], axis=-1)
    kb = jnp.concatenate([sin2, z64], axis=-1)
    ones = jnp.ones((seq, B_NOPE), F32)
    z128 = jnp.zeros((seq, B_NOPE), F32)
    qa = jnp.tile(jnp.concatenate([ones, cos2, z64], axis=-1), (1, B_HEADS))
    qb = jnp.tile(jnp.concatenate([z128, sin2, z64], axis=-1), (1, B_HEADS))
    return ka, kb, qa, qb


def _rot_cols(w):
    half = w.shape[-1] // 2
    return jnp.concatenate([-w[..., half:], w[..., :half]], axis=-1)


def _routing_tables(flat_e):
    n_assign = flat_e.shape[0]
    onehot = (flat_e[:, None] == jnp.arange(N_EXPERTS, dtype=jnp.int32)[None, :]).astype(jnp.int32)
    csum = jnp.cumsum(onehot, axis=0)
    rank = jnp.sum((csum - onehot) * onehot, axis=1)
    sizes = csum[-1]
    padded = ((sizes + MOE_ROWS - 1) // MOE_ROWS) * MOE_ROWS
    pad_end = jnp.cumsum(padded)
    pad_start = pad_end - padded
    dest = (pad_start[flat_e] + rank).astype(jnp.int32)
    n_blocks = -(-n_assign // MOE_ROWS) + N_EXPERTS
    n_rows = n_blocks * MOE_ROWS
    tok = jnp.arange(n_assign, dtype=jnp.int32) // TOP_K
    row_tok = jnp.zeros((n_rows,), jnp.int32).at[dest].set(tok)
    block_start = jnp.arange(n_blocks, dtype=jnp.int32) * MOE_ROWS
    block_e = jnp.minimum(jnp.searchsorted(pad_end, block_start, side='right'), N_EXPERTS - 1).astype(jnp.int32)
    n_used = (pad_end[-1:] // MOE_ROWS).astype(jnp.int32)
    return dest, row_tok, block_e, n_used


def kernel(x, c, ada_w, ada_b, ln_g, ln_b, ab_w_in, ab_lam, ab_diff_g, ab_q_norm_g, ab_kv_norm_g, ab_w_uq,
           ab_w_ukv, ab_w_out, c_w_in, c_sink, c_w_out, ffn_w_gate, ffn_w_up, ffn_w_down, moe_w_router,
           moe_b_router, moe_w_gate, moe_w_up, moe_w_down):
    bsz, seq, d = x.shape
    depth = ada_w.shape[0]
    n_tok = bsz * seq
    alpha = (2 * depth) ** 0.25

    c_pad = jnp.zeros((8, d), F32).at[:bsz].set(c)
    mods = adaln_all(c_pad, ada_w.reshape(2 * depth, d, 3 * d), ada_b.reshape(2 * depth, 1, 3 * d))
    mods = mods.reshape(2 * depth, 8, 3, d)

    ka, kb, qa, qb = _rope_tables(seq)
    slopes_a2 = _alibi_slopes(A_HEADS) * LOG2E
    eq_a, ek_a = _alibi_tables(seq, slopes_a2)
    slopes_c = _alibi_slopes(C_HEADS)
    scale_a = A_QK_DIM ** -0.5 * LOG2E
    scale_b = (B_NOPE + B_ROPE) ** -0.5 * LOG2E
    scale_c = C_HEAD_DIM ** -0.5

    h = modulate(x, mods, 0, BF16)
    for layer in range(depth):
        i = layer // 2
        l2 = 2 * layer
        last = layer == depth - 1
        if layer % 2 == 0:
            lam_init = 0.8 - 0.6 * math.exp(-0.3 * layer)
            w_in = ab_w_in[i]
            w_kr = w_in[:, 3 * A_COLS + B_Q_LORA + B_KV_LORA:]
            w_in2 = jnp.concatenate([w_in[:, :A_COLS] * scale_a, w_in[:, A_COLS:], _rot_cols(w_kr)],
                                    axis=1).astype(BF16)
            p_ab = inproj_ab(h.reshape(n_tok, d), w_in2, ka, kb, seq)
            o_a = flash_diff(p_ab.reshape(bsz, seq, -1), eq_a, ek_a, slopes_a2, _key_norm_bounds(p_ab, bsz, seq),
                             ab_lam[i], ab_diff_g[i].reshape(1, -1), lam_init, bsz, seq)
            w_uq = ab_w_uq[i].reshape(B_Q_LORA, B_HEADS, B_NOPE + B_ROPE) * scale_b
            w_uq2 = jnp.concatenate([w_uq, _rot_cols(w_uq[..., B_NOPE:])], axis=-1)
            w_uq2 = w_uq2.reshape(B_Q_LORA, B_HEADS * 2 * LANES).astype(BF16)
            q_cat = uq_proj(p_ab, 3 * A_COLS // B_Q_LORA, ab_q_norm_g[i].reshape(1, -1), w_uq2, qa, qb, seq)
            k_cat, v_b = ukv_proj(p_ab, (3 * A_COLS + B_Q_LORA) // B_KV_LORA,
                                  (3 * A_COLS + B_Q_LORA + B_KV_LORA) // LANES,
                                  ab_kv_norm_g[i].reshape(1, -1), ab_w_ukv[i].astype(BF16))
            o_b = flash_mla(q_cat.reshape(bsz, seq, -1), k_cat.reshape(bsz, seq, -1),
                            v_b.reshape(bsz, seq, -1), bsz, seq)
            acts, w_out = [o_a, o_b], ab_w_out[i]
        else:
            w_in = c_w_in[i]
            nqc = C_HEADS * C_HEAD_DIM
            w_in2 = jnp.concatenate([w_in[:, :nqc] * scale_c, w_in[:, nqc:]], axis=1).astype(BF16)
            p_c = matmul(h.reshape(n_tok, d), w_in2, BF16, tn=1024)
            o_c = window_attn(p_c.reshape(bsz, seq, -1), slopes_c, c_sink[i].astype(F32), bsz, seq)
            acts, w_out = [o_c], c_w_out[i]
        moe_next = layer % 2 == 1
        x, h = outproj_norm(acts, w_out.astype(BF16), x, mods, l2, ln_g[layer, 0], ln_b[layer, 0], alpha,
                            F32 if moe_next else BF16)

        if not moe_next:
            y = ffn(h.reshape(n_tok, d), ffn_w_gate[i].astype(BF16), ffn_w_up[i].astype(BF16),
                    ffn_w_down[i].astype(BF16)).reshape(bsz, seq, d)
            route = dest = None
        else:
            hf = h.reshape(n_tok, d)
            w_r = jnp.zeros((d, LANES), F32).at[:, :N_EXPERTS].set(moe_w_router[i])
            b_r = jnp.zeros((1, LANES), F32).at[0, :N_EXPERTS].set(moe_b_router[i])
            route = router(hf, w_r, b_r)
            flat_e = route[:, :TOP_K].astype(jnp.int32).reshape(-1)
            dest, row_tok, block_e, n_used = _routing_tables(flat_e)
            y = moe_ffn(block_e, n_used, row_tok, hf, moe_w_gate[i], moe_w_up[i], moe_w_down[i])
            route = route.reshape(bsz, seq, LANES)
        x, h = postnorm(x, y, mods, l2 + 1, ln_g[layer, 1], ln_b[layer, 1], alpha,
                        route=route, dest=dest, next_dtype=None if last else BF16)
    return x
```

```python
import functools
import math

import numpy as np
import jax
import jax.numpy as jnp
from jax import lax
from jax.experimental import pallas as pl
from jax.experimental.pallas import tpu as pltpu

F32 = jnp.float32
BF16 = jnp.bfloat16

A_HEADS = 4
A_QK_DIM = 64
A_V_DIM = 128
B_HEADS = 4
B_Q_LORA = 512
B_KV_LORA = 256
B_NOPE = 128
B_ROPE = 64
B_V_DIM = 128
ROPE_THETA = 10000.0
C_HEADS = 16
C_KV_HEADS = 4
C_GROUP = C_HEADS // C_KV_HEADS
C_HEAD_DIM = 128
WINDOW = 128
N_EXPERTS = 8
TOP_K = 2
MOE_ROWS = 1024
LN_EPS = 1e-5
RMS_EPS = 1e-6

A_COLS = A_HEADS * 2 * A_QK_DIM
LANES = 128
VMEM_LIMIT = 56 * 1024 * 1024
NEG_INF = float("-inf")
LOG2E = math.log2(math.e)
EXP2_UNDERFLOW = -151.0
FLASH_TK = 1024


def _cparams(sem, vmem=VMEM_LIMIT):
    return pltpu.CompilerParams(dimension_semantics=sem, vmem_limit_bytes=vmem)


def _adaln_kernel(c_ref, w_ref, b_ref, o_ref):
    c = c_ref[...]
    ca = c * jax.nn.sigmoid(c)
    o_ref[...] = jnp.dot(ca, w_ref[...], preferred_element_type=F32,
                         precision=lax.Precision.HIGHEST) + b_ref[...]


def adaln_all(c_pad, ada_w, ada_b):
    l2, d, n = ada_w.shape
    tn = 768
    return pl.pallas_call(
        _adaln_kernel,
        grid=(l2, n // tn),
        in_specs=[pl.BlockSpec((8, d), lambda l, j: (0, 0)),
                  pl.BlockSpec((None, d, tn), lambda l, j: (l, 0, j)),
                  pl.BlockSpec((None, 1, tn), lambda l, j: (l, 0, j))],
        out_specs=pl.BlockSpec((None, 8, tn), lambda l, j: (l, 0, j)),
        out_shape=jax.ShapeDtypeStruct((l2, 8, n), F32),
        compiler_params=_cparams(("parallel", "parallel")),
        name="adaln",
    )(c_pad, ada_w, ada_b)


def _modulate_kernel(x_ref, mod_ref, h_ref):
    x = x_ref[...]
    h_ref[...] = (x * (1.0 + mod_ref[1:2, :]) + mod_ref[0:1, :]).astype(h_ref.dtype)


def modulate(x, mods, l2, out_dtype):
    b, s, d = x.shape
    ts = 512
    return pl.pallas_call(
        _modulate_kernel,
        grid=(b, s // ts),
        in_specs=[pl.BlockSpec((None, ts, d), lambda i, j: (i, j, 0)),
                  pl.BlockSpec((None, None, 3, d), lambda i, j: (l2, i, 0, 0))],
        out_specs=pl.BlockSpec((None, ts, d), lambda i, j: (i, j, 0)),
        out_shape=jax.ShapeDtypeStruct((b, s, d), out_dtype),
        compiler_params=_cparams(("parallel", "parallel")),
        name="modulate",
    )(x, mods)


def _residual_norm(x, y, modc_ref, g_ref, b_ref, modn_ref, xo_ref, h_ref, alpha):
    z = alpha * x + (1.0 + modc_ref[2:3, :]) * y
    mu = jnp.mean(z, axis=-1, keepdims=True)
    zc = z - mu
    var = jnp.mean(zc * zc, axis=-1, keepdims=True)
    xn = zc * lax.rsqrt(var + LN_EPS) * g_ref[...] + b_ref[...]
    xo_ref[...] = xn
    if h_ref is not None:
        h_ref[...] = (xn * (1.0 + modn_ref[1:2, :]) + modn_ref[0:1, :]).astype(h_ref.dtype)


def _postnorm_kernel(*refs, alpha, routed, has_next):
    refs = list(refs)
    x_ref, y_ref, modc_ref, g_ref, b_ref = refs[:5]
    rest = refs[5:]
    if routed:
        route_ref, idx_ref, idxn_ref = rest[:3]
        rest = rest[3:]
    modn_ref = rest.pop(0) if has_next else None
    xo_ref = rest.pop(0)
    h_ref = rest.pop(0) if has_next else None
    x = x_ref[...]
    ts = x.shape[0]
    if routed:
        yg_ref, sem = rest
        step = pl.program_id(0) * pl.num_programs(1) + pl.program_id(1)
        n_steps = pl.num_programs(0) * pl.num_programs(1)
        slot = step % 2

        def issue_gather(idx, dst_slot):
            def body(p, carry):
                for prio in range(2):
                    r = 2 * p + prio
                    pltpu.make_async_copy(y_ref.at[pl.ds(idx[0, r], 1)], yg_ref.at[dst_slot, pl.ds(r, 1)],
                                          sem.at[dst_slot]).start(priority=prio)
                return carry
            lax.fori_loop(0, TOP_K * ts // 2, body, 0, unroll=4)

        @pl.when(step == 0)
        def _():
            issue_gather(idx_ref, 0)

        @pl.when(step + 1 < n_steps)
        def _():
            issue_gather(idxn_ref, 1 - slot)

        pltpu.make_async_copy(y_ref.at[pl.ds(0, TOP_K * ts)], yg_ref.at[slot], sem.at[slot]).wait()
        g1 = route_ref[:, TOP_K:TOP_K + 1]
        g2 = route_ref[:, TOP_K + 1:TOP_K + 2]
        y = g1 * yg_ref[slot, 0:ts, :] + g2 * yg_ref[slot, ts:2 * ts, :]
    else:
        y = y_ref[...].astype(F32)
    _residual_norm(x, y, modc_ref, g_ref, b_ref, modn_ref, xo_ref, h_ref, alpha)


def postnorm(x, y, mods, l2, ln_g, ln_b, alpha, *, route=None, dest=None, next_dtype=None):
    b, s, d = x.shape
    ts = 256
    has_next = next_dtype is not None
    routed = route is not None
    nj = s // ts
    if routed:
        y_spec = pl.BlockSpec(memory_space=pl.ANY)
    else:
        y_spec = pl.BlockSpec((None, ts, d), lambda i, j: (i, j, 0))
    in_specs = [pl.BlockSpec((None, ts, d), lambda i, j: (i, j, 0)),
                y_spec,
                pl.BlockSpec((None, None, 3, d), lambda i, j: (l2, i, 0, 0)),
                pl.BlockSpec((1, d), lambda i, j: (0, 0)),
                pl.BlockSpec((1, d), lambda i, j: (0, 0))]
    args = [x, y, mods, ln_g.reshape(1, d), ln_b.reshape(1, d)]
    scratch = []
    if routed:
        n_steps = b * nj
        idx = dest.reshape(n_steps, ts, TOP_K).transpose(0, 2, 1).reshape(n_steps, 1, TOP_K * ts)
        in_specs += [pl.BlockSpec((None, ts, LANES), lambda i, j: (i, j, 0)),
                     pl.BlockSpec((None, 1, TOP_K * ts), lambda i, j: (i * nj + j, 0, 0),
                                  memory_space=pltpu.SMEM),
                     pl.BlockSpec((None, 1, TOP_K * ts),
                                  lambda i, j: (jnp.minimum(i * nj + j + 1, n_steps - 1), 0, 0),
                                  memory_space=pltpu.SMEM)]
        args += [route, idx, idx]
        scratch = [pltpu.VMEM((2, TOP_K * ts, d), F32), pltpu.SemaphoreType.DMA((2,))]
    out_specs = [pl.BlockSpec((None, ts, d), lambda i, j: (i, j, 0))]
    out_shape = [jax.ShapeDtypeStruct((b, s, d), F32)]
    if has_next:
        in_specs.append(pl.BlockSpec((None, None, 3, d), lambda i, j: (l2 + 1, i, 0, 0)))
        args.append(mods)
        out_specs.append(pl.BlockSpec((None, ts, d), lambda i, j: (i, j, 0)))
        out_shape.append(jax.ShapeDtypeStruct((b, s, d), next_dtype))
    sem = ("arbitrary", "arbitrary") if routed else ("parallel", "parallel")
    out = pl.pallas_call(
        functools.partial(_postnorm_kernel, alpha=alpha, routed=routed, has_next=has_next),
        grid=(b, nj),
        in_specs=in_specs,
        out_specs=out_specs,
        out_shape=out_shape,
        scratch_shapes=scratch,
        compiler_params=_cparams(sem),
        name="postnorm",
    )(*args)
    return (out[0], out[1]) if has_next else (out[0], None)


def _rms_rows(a_ref, g_ref):
    a = a_ref[...].astype(F32)
    return (a * lax.rsqrt(jnp.mean(a * a, axis=-1, keepdims=True) + RMS_EPS) * g_ref[...]).astype(BF16)


def _matmul_kernel(a_ref, w_ref, o_ref):
    o_ref[...] = jnp.dot(a_ref[...], w_ref[...], preferred_element_type=F32).astype(o_ref.dtype)


def matmul(a, w, out_dtype, tm=512, tn=None):
    m, k = a.shape
    n = w.shape[1]
    tn = n if tn is None else tn
    return pl.pallas_call(
        _matmul_kernel,
        grid=(m // tm, n // tn),
        in_specs=[pl.BlockSpec((tm, k), lambda i, j: (i, 0)),
                  pl.BlockSpec((k, tn), lambda i, j: (0, j))],
        out_specs=pl.BlockSpec((tm, tn), lambda i, j: (i, j)),
        out_shape=jax.ShapeDtypeStruct((m, n), out_dtype),
        compiler_params=_cparams(("parallel", "parallel")),
        name="matmul",
    )(a, w)


def _outproj_norm_kernel(*refs, n_in, alpha):
    a_refs, w_refs = refs[:n_in], refs[n_in:2 * n_in]
    x_ref, modc_ref, g_ref, b_ref, modn_ref, xo_ref, h_ref = refs[2 * n_in:]
    y = jnp.dot(a_refs[0][...], w_refs[0][...], preferred_element_type=F32)
    for a_ref, w_ref in zip(a_refs[1:], w_refs[1:]):
        y += jnp.dot(a_ref[...], w_ref[...], preferred_element_type=F32)
    _residual_norm(x_ref[...], y, modc_ref, g_ref, b_ref, modn_ref, xo_ref, h_ref, alpha)


def outproj_norm(acts, w, x, mods, l2, ln_g, ln_b, alpha, next_dtype):
    b, s, d = x.shape
    k = acts[0].shape[-1]
    assert all(a.shape[-1] == k for a in acts) and w.shape[0] == k * len(acts)
    ts = 512
    n_in = len(acts)
    row = lambda i, j: (i, j, 0)
    in_specs = ([pl.BlockSpec((None, ts, k), row) for _ in acts]
                + [pl.BlockSpec((k, d), functools.partial(lambda p, i, j: (p, 0), p)) for p in range(n_in)]
                + [pl.BlockSpec((None, ts, d), row),
                   pl.BlockSpec((None, None, 3, d), lambda i, j: (l2, i, 0, 0)),
                   pl.BlockSpec((1, d), lambda i, j: (0, 0)),
                   pl.BlockSpec((1, d), lambda i, j: (0, 0)),
                   pl.BlockSpec((None, None, 3, d), lambda i, j: (l2 + 1, i, 0, 0))])
    out = pl.pallas_call(
        functools.partial(_outproj_norm_kernel, n_in=n_in, alpha=alpha),
        grid=(b, s // ts),
        in_specs=in_specs,
        out_specs=[pl.BlockSpec((None, ts, d), row), pl.BlockSpec((None, ts, d), row)],
        out_shape=[jax.ShapeDtypeStruct((b, s, d), F32), jax.ShapeDtypeStruct((b, s, d), next_dtype)],
        compiler_params=_cparams(("parallel", "parallel")),
        name="outproj_norm",
    )(*acts, *([w] * n_in), x, mods, ln_g.reshape(1, d), ln_b.reshape(1, d), mods)
    return out[0], out[1]


def _inproj_ab_kernel(a_ref, w_ref, ra_ref, rb_ref, o_ref):
    acc = jnp.dot(a_ref[...], w_ref[...], preferred_element_type=F32)
    n = acc.shape[1]
    o_ref[:, :n - LANES] = acc[:, :n - LANES].astype(o_ref.dtype)
    last = acc[:, n - LANES:]
    roped = last * ra_ref[...] + pltpu.roll(last, LANES // 2, 1) * rb_ref[...]
    o_ref[:, n - LANES:] = roped.astype(o_ref.dtype)


def inproj_ab(h, w, rope_a, rope_b, seq):
    m, k = h.shape
    n = w.shape[1]
    tm = 512
    nsb = seq // tm
    return pl.pallas_call(
        _inproj_ab_kernel,
        grid=(m // tm,),
        in_specs=[pl.BlockSpec((tm, k), lambda i: (i, 0)),
                  pl.BlockSpec((k, n), lambda i: (0, 0)),
                  pl.BlockSpec((tm, LANES), lambda i: (i % nsb, 0)),
                  pl.BlockSpec((tm, LANES), lambda i: (i % nsb, 0))],
        out_specs=pl.BlockSpec((tm, n), lambda i: (i, 0)),
        out_shape=jax.ShapeDtypeStruct((m, n), BF16),
        compiler_params=_cparams(("parallel",)),
        name="inproj_ab",
    )(h, w, rope_a, rope_b)


def _uq_kernel(a_ref, g_ref, w_ref, ra_ref, rb_ref, o_ref):
    an = _rms_rows(a_ref, g_ref)
    acc = jnp.dot(an, w_ref[...], preferred_element_type=F32)
    n = acc.shape[1]
    o_ref[...] = (acc * ra_ref[...] + pltpu.roll(acc, n - B_ROPE, 1) * rb_ref[...]).astype(o_ref.dtype)


def uq_proj(p_ab, col_block, g, w, rope_a, rope_b, seq):
    m = p_ab.shape[0]
    k, n = w.shape
    tm = 512
    nsb = seq // tm
    return pl.pallas_call(
        _uq_kernel,
        grid=(m // tm,),
        in_specs=[pl.BlockSpec((tm, k), lambda i: (i, col_block)),
                  pl.BlockSpec((1, k), lambda i: (0, 0)),
                  pl.BlockSpec((k, n), lambda i: (0, 0)),
                  pl.BlockSpec((tm, n), lambda i: (i % nsb, 0)),
                  pl.BlockSpec((tm, n), lambda i: (i % nsb, 0))],
        out_specs=pl.BlockSpec((tm, n), lambda i: (i, 0)),
        out_shape=jax.ShapeDtypeStruct((m, n), BF16),
        compiler_params=_cparams(("parallel",)),
        name="uq_proj",
    )(p_ab, g, w, rope_a, rope_b)


def _ukv_kernel(a_ref, g_ref, w_ref, kr_ref, k_ref, v_ref):
    an = _rms_rows(a_ref, g_ref)
    acc = jnp.dot(an, w_ref[...], preferred_element_type=F32)
    kr = kr_ref[...]
    for h in range(B_HEADS):
        base = h * (B_NOPE + B_V_DIM)
        k_ref[:, 2 * h * LANES:(2 * h + 1) * LANES] = acc[:, base:base + B_NOPE].astype(k_ref.dtype)
        k_ref[:, (2 * h + 1) * LANES:(2 * h + 2) * LANES] = kr
        v_ref[:, h * B_V_DIM:(h + 1) * B_V_DIM] = acc[:, base + B_NOPE:base + B_NOPE + B_V_DIM].astype(v_ref.dtype)


def ukv_proj(p_ab, ckv_block, kr_block, g, w):
    m = p_ab.shape[0]
    k, n = w.shape
    tm = 512
    return pl.pallas_call(
        _ukv_kernel,
        grid=(m // tm,),
        in_specs=[pl.BlockSpec((tm, k), lambda i: (i, ckv_block)),
                  pl.BlockSpec((1, k), lambda i: (0, 0)),
                  pl.BlockSpec((k, n), lambda i: (0, 0)),
                  pl.BlockSpec((tm, LANES), lambda i: (i, kr_block))],
        out_specs=[pl.BlockSpec((tm, B_HEADS * 2 * LANES), lambda i: (i, 0)),
                   pl.BlockSpec((tm, B_HEADS * B_V_DIM), lambda i: (i, 0))],
        out_shape=[jax.ShapeDtypeStruct((m, B_HEADS * 2 * LANES), BF16),
                   jax.ShapeDtypeStruct((m, B_HEADS * B_V_DIM), BF16)],
        compiler_params=_cparams(("parallel",)),
        name="ukv_proj",
    )(p_ab, g, w, p_ab)


_NT = (((1,), (1,)), ((), ()))


def _ones_block(rows):
    return jnp.ones((rows, LANES), BF16)


def _online_softmax_step(s, v_aug, m_ref, acc_ref):
    tk = s.shape[1]
    m_prev = m_ref[...]
    m_new = jnp.maximum(m_prev, jnp.max(s, axis=1, keepdims=True))
    alpha = jnp.exp2(m_prev - m_new)
    p = jnp.exp2(s - jnp.concatenate([m_new] * (tk // LANES), axis=1))
    pv = jnp.dot(p.astype(v_aug.dtype), v_aug, preferred_element_type=F32)
    acc_ref[...] = jnp.concatenate([alpha, alpha], axis=1) * acc_ref[...] + pv
    m_ref[...] = m_new


def _softmax_result(acc_ref):
    acc = acc_ref[...]
    return acc[:, :LANES] / acc[:, LANES:]


def _flash_diff_kernel(slopes_ref, kn_ref, q_ref, eq_ref, k_ref, ek_ref, v_ref, lam_ref, g_ref, o_ref,
                       qaug_sc, m_sc, acc_sc, *, tk, lam_init, splits):
    tq = q_ref.shape[0]
    seq = k_ref.shape[0]
    n_chunks = seq // tk
    qi = pl.program_id(2)
    q0 = qi * tq
    q = q_ref[...]
    eq = eq_ref[...]
    qf = q.astype(F32)
    q_norm = jnp.sqrt(jnp.max(jnp.sum(qf * qf, axis=1, keepdims=True), axis=0, keepdims=True))[0, 0]
    kn_base = (pl.program_id(0) * pl.num_programs(1) + pl.program_id(1)) * n_chunks
    lane = lax.broadcasted_iota(jnp.int32, q.shape, 1)
    zero = jnp.zeros_like(q)
    for mp in range(2):
        qm = jnp.where((lane < A_QK_DIM) == (mp == 0), q, zero)
        qaug_sc[0, mp] = jnp.concatenate([qm, -eq], axis=1)
        qaug_sc[1, mp] = jnp.concatenate([qm, eq], axis=1)
    m_sc[...] = jnp.full(m_sc.shape, NEG_INF, F32)
    acc_sc[...] = jnp.zeros(acc_sc.shape, F32)
    ones_col = _ones_block(tk)

    def chunk(c, side, diagonal):
        k0 = pl.multiple_of(c * tk, tk)
        kc = jnp.concatenate([k_ref[pl.ds(k0, tk), :], ek_ref[pl.ds(k0, tk), :]], axis=1)
        v_aug = jnp.concatenate([v_ref[pl.ds(k0, tk), :], ones_col], axis=1)
        if diagonal:
            rel = (lax.broadcasted_iota(jnp.int32, (tq, tk), 0)
                   - lax.broadcasted_iota(jnp.int32, (tq, tk), 1)) + (q0 - k0)
            corr = (2.0 * slopes_ref[pl.program_id(1)]) * jnp.maximum(rel, 0).astype(F32)
        part = tq // splits
        for hh in range(splits):
            rows = pl.ds(hh * part, part)
            for mp in range(2):
                s = lax.dot_general(qaug_sc[side, mp, rows, :], kc, _NT, preferred_element_type=F32)
                if diagonal:
                    s = s - corr[hh * part:(hh + 1) * part]
                _online_softmax_step(s, v_aug, m_sc.at[mp, rows], acc_sc.at[mp, rows])

    cd = q0 // tk
    chunk(cd, 1, True)
    m_floor = jnp.min(jnp.min(jnp.minimum(m_sc[0], m_sc[1]), axis=1, keepdims=True),
                      axis=0, keepdims=True)[0, 0]
    slope = slopes_ref[pl.program_id(1)]

    def off_diagonal(idx, carry):
        c = idx + (idx >= cd).astype(jnp.int32)
        gap = (jnp.abs(c - cd) - 1) * tk + 1
        bound = q_norm * kn_ref[kn_base + c] - slope * gap.astype(F32)

        @pl.when(bound - m_floor > EXP2_UNDERFLOW)
        def _():
            chunk(c, (c > cd).astype(jnp.int32), False)
        return carry

    lax.fori_loop(0, n_chunks - 1, off_diagonal, 0, unroll=2)

    o0 = _softmax_result(acc_sc.at[0])
    o1 = _softmax_result(acc_sc.at[1])
    lam = lam_ref[...]
    lam_full = (jnp.exp(jnp.sum(lam[0:1] * lam[1:2], axis=1, keepdims=True))
                - jnp.exp(jnp.sum(lam[2:3] * lam[3:4], axis=1, keepdims=True)) + lam_init)
    o = o0 - lam_full * o1
    on = o * lax.rsqrt(jnp.mean(o * o, axis=-1, keepdims=True) + RMS_EPS) * g_ref[...]
    o_ref[...] = (on * (1.0 - lam_init)).astype(o_ref.dtype)


def flash_diff(p_ab, eq, ek, slopes2, k_norms, lam, diff_g, lam_init, bsz, seq):
    tq, tk = FLASH_TK, FLASH_TK
    nh = A_HEADS
    return pl.pallas_call(
        functools.partial(_flash_diff_kernel, tk=tk, lam_init=lam_init, splits=4),
        grid=(bsz, nh, seq // tq),
        in_specs=[pl.BlockSpec(memory_space=pltpu.SMEM),
                  pl.BlockSpec(memory_space=pltpu.SMEM),
                  pl.BlockSpec((None, tq, LANES), lambda b, h, i: (b, i, h)),
                  pl.BlockSpec((None, tq, LANES), lambda b, h, i: (h, i, 0)),
                  pl.BlockSpec((None, seq, LANES), lambda b, h, i: (b, 0, nh + h)),
                  pl.BlockSpec((None, seq, LANES), lambda b, h, i: (h, 0, 0)),
                  pl.BlockSpec((None, seq, LANES), lambda b, h, i: (b, 0, 2 * nh + h)),
                  pl.BlockSpec((4, A_QK_DIM), lambda b, h, i: (0, 0)),
                  pl.BlockSpec((1, A_V_DIM), lambda b, h, i: (0, 0))],
        out_specs=pl.BlockSpec((None, tq, A_V_DIM), lambda b, h, i: (b, i, h)),
        out_shape=jax.ShapeDtypeStruct((bsz, seq, nh * A_V_DIM), BF16),
        scratch_shapes=[pltpu.VMEM((2, 2, tq, 2 * LANES), BF16),
                        pltpu.VMEM((2, tq, LANES), F32),
                        pltpu.VMEM((2, tq, 2 * LANES), F32)],
        compiler_params=_cparams(("parallel", "parallel", "parallel")),
        name="flash_diff",
    )(slopes2, k_norms, p_ab, eq, p_ab, ek, p_ab, lam, diff_g)


def _flash_mla_kernel(q_ref, k_ref, v_ref, o_ref, m_sc, acc_sc, *, tk, splits):
    seq = k_ref.shape[0]
    part = q_ref.shape[0] // splits
    m_sc[...] = jnp.full(m_sc.shape, NEG_INF, F32)
    acc_sc[...] = jnp.zeros(acc_sc.shape, F32)
    ones_col = _ones_block(tk)

    def body(c, carry):
        k0 = pl.multiple_of(c * tk, tk)
        kc = k_ref[pl.ds(k0, tk), :]
        v_aug = jnp.concatenate([v_ref[pl.ds(k0, tk), :], ones_col], axis=1)
        for hh in range(splits):
            rows = pl.ds(hh * part, part)
            s = lax.dot_general(q_ref[rows, :], kc, _NT, preferred_element_type=F32)
            _online_softmax_step(s, v_aug, m_sc.at[rows], acc_sc.at[rows])
        return carry

    lax.fori_loop(0, seq // tk, body, 0, unroll=2)
    o_ref[...] = _softmax_result(acc_sc).astype(o_ref.dtype)


def flash_mla(q_cat, k_cat, v_b, bsz, seq):
    tq, tk = 1024, 1024
    kd = 2 * LANES
    return pl.pallas_call(
        functools.partial(_flash_mla_kernel, tk=tk, splits=4),
        grid=(bsz, B_HEADS, seq // tq),
        in_specs=[pl.BlockSpec((None, tq, kd), lambda b, h, i: (b, i, h)),
                  pl.BlockSpec((None, seq, kd), lambda b, h, i: (b, 0, h)),
                  pl.BlockSpec((None, seq, B_V_DIM), lambda b, h, i: (b, 0, h))],
        out_specs=pl.BlockSpec((None, tq, B_V_DIM), lambda b, h, i: (b, i, h)),
        out_shape=jax.ShapeDtypeStruct((bsz, seq, B_HEADS * B_V_DIM), BF16),
        scratch_shapes=[pltpu.VMEM((tq, LANES), F32), pltpu.VMEM((tq, 2 * LANES), F32)],
        compiler_params=_cparams(("parallel", "parallel", "parallel")),
        name="flash_mla",
    )(q_cat, k_cat, v_b)


def _window_kernel(slopes_ref, sink_ref, q_ref, k_ref, v_ref, o_ref):
    tq = q_ref.shape[0]
    seq = k_ref.shape[0]
    blk = WINDOW
    kw = 3 * blk
    n = pl.program_id(1)
    q0 = pl.program_id(2) * tq
    rel = (lax.broadcasted_iota(jnp.int32, (blk, kw), 1) - lax.broadcasted_iota(jnp.int32, (blk, kw), 0))
    ones_blk = _ones_block(kw)
    heads = [n * C_GROUP + g for g in range(C_GROUP)]
    slope_rows = jnp.concatenate([jnp.full((blk, kw), slopes_ref[hd], F32) for hd in heads], axis=0)
    sink_rows = jnp.concatenate([jnp.full((blk, LANES), sink_ref[hd], F32) for hd in heads], axis=0)
    for sb in range(tq // blk):
        t0 = q0 + sb * blk
        start = pl.multiple_of(jnp.clip(t0 - blk, 0, seq - kw), blk)
        kc = k_ref[pl.ds(start, kw), :]
        v_aug = jnp.concatenate([v_ref[pl.ds(start, kw), :], ones_blk], axis=1)
        dist = jnp.abs(rel + (start - t0))
        dist_rows = jnp.concatenate([dist] * C_GROUP, axis=0)
        qs = jnp.concatenate([q_ref[sb * blk:(sb + 1) * blk, g * C_HEAD_DIM:(g + 1) * C_HEAD_DIM]
                              for g in range(C_GROUP)], axis=0)
        s = lax.dot_general(qs, kc, _NT, preferred_element_type=F32) - slope_rows * dist_rows.astype(F32)
        s = jnp.where(dist_rows <= WINDOW, s, NEG_INF)
        m = jnp.maximum(jnp.max(s, axis=1, keepdims=True), sink_rows)
        e = jnp.exp(s - jnp.concatenate([m] * (kw // LANES), axis=1))
        pv = jnp.dot(e.astype(v_aug.dtype), v_aug, preferred_element_type=F32)
        o = pv[:, :C_HEAD_DIM] / (pv[:, C_HEAD_DIM:] + jnp.exp(sink_rows - m))
        for g in range(C_GROUP):
            o_ref[sb * blk:(sb + 1) * blk, g * C_HEAD_DIM:(g + 1) * C_HEAD_DIM] = (
                o[g * blk:(g + 1) * blk].astype(o_ref.dtype))


def window_attn(p_c, slopes, sink, bsz, seq):
    tq = 512
    gw = C_GROUP * C_HEAD_DIM
    nq = C_HEADS * C_HEAD_DIM // LANES
    return pl.pallas_call(
        _window_kernel,
        grid=(bsz, C_KV_HEADS, seq // tq),
        in_specs=[pl.BlockSpec(memory_space=pltpu.SMEM),
                  pl.BlockSpec(memory_space=pltpu.SMEM),
                  pl.BlockSpec((None, tq, gw), lambda b, n, i: (b, i, n)),
                  pl.BlockSpec((None, seq, C_HEAD_DIM), lambda b, n, i: (b, 0, nq + n)),
                  pl.BlockSpec((None, seq, C_HEAD_DIM), lambda b, n, i: (b, 0, nq + C_KV_HEADS + n))],
        out_specs=pl.BlockSpec((None, tq, gw), lambda b, n, i: (b, i, n)),
        out_shape=jax.ShapeDtypeStruct((bsz, seq, C_HEADS * C_HEAD_DIM), BF16),
        compiler_params=_cparams(("parallel", "parallel", "parallel")),
        name="window_attn",
    )(slopes, sink, p_c, p_c, p_c)


def _swiglu_step(x, wg_ref, wu_ref, wd_ref, acc_ref):
    g = jnp.dot(x, wg_ref[...].astype(BF16), preferred_element_type=F32)
    u = jnp.dot(x, wu_ref[...].astype(BF16), preferred_element_type=F32)
    a = (g * jax.nn.sigmoid(g) * u).astype(BF16)
    acc_ref[...] += jnp.dot(a, wd_ref[...].astype(BF16), preferred_element_type=F32)


def _ffn_kernel(x_ref, wg_ref, wu_ref, wd_ref, o_ref, acc_ref):
    j = pl.program_id(1)

    @pl.when(j == 0)
    def _():
        acc_ref[...] = jnp.zeros(acc_ref.shape, F32)

    _swiglu_step(x_ref[...], wg_ref, wu_ref, wd_ref, acc_ref)

    @pl.when(j == pl.num_programs(1) - 1)
    def _():
        o_ref[...] = acc_ref[...].astype(o_ref.dtype)


def ffn(h, wg, wu, wd):
    m, d = h.shape
    f = wg.shape[1]
    tm, tf = 1024, 512
    return pl.pallas_call(
        _ffn_kernel,
        grid=(m // tm, f // tf),
        in_specs=[pl.BlockSpec((tm, d), lambda i, j: (i, 0)),
                  pl.BlockSpec((d, tf), lambda i, j: (0, j)),
                  pl.BlockSpec((d, tf), lambda i, j: (0, j)),
                  pl.BlockSpec((tf, d), lambda i, j: (j, 0))],
        out_specs=pl.BlockSpec((tm, d), lambda i, j: (i, 0)),
        out_shape=jax.ShapeDtypeStruct((m, d), F32),
        scratch_shapes=[pltpu.VMEM((tm, d), F32)],
        compiler_params=_cparams(("parallel", "arbitrary")),
        name="ffn",
    )(h, wg, wu, wd)


def _moe_ffn_kernel(be_ref, nu_ref, tok_ref, tokn_ref, h_ref, wg_ref, wu_ref, wd_ref, o_ref,
                    xg_ref, xb_ref, acc_ref, sem):
    del be_ref
    i = pl.program_id(0)
    j = pl.program_id(1)
    rows = xg_ref.shape[0]
    n_used = nu_ref[0]

    def issue_gather(idx_ref):
        def body(p, carry):
            for prio in range(2):
                r = 2 * p + prio
                pltpu.make_async_copy(h_ref.at[pl.ds(idx_ref[0, r], 1)], xg_ref.at[pl.ds(r, 1)],
                                      sem).start(priority=prio)
            return carry
        lax.fori_loop(0, rows // 2, body, 0, unroll=4)

    @pl.when((j == 0) & (i < n_used))
    def _():
        @pl.when(i == 0)
        def _():
            issue_gather(tok_ref)
        pltpu.make_async_copy(h_ref.at[pl.ds(0, rows)], xg_ref, sem).wait()
        xb_ref[...] = xg_ref[...].astype(BF16)
        acc_ref[...] = jnp.zeros(acc_ref.shape, F32)

    @pl.when((j == 1) & (i + 1 < n_used))
    def _():
        issue_gather(tokn_ref)

    @pl.when(i < n_used)
    def _():
        _swiglu_step(xb_ref[...], wg_ref, wu_ref, wd_ref, acc_ref)

    @pl.when((j == pl.num_programs(1) - 1) & (i < n_used))
    def _():
        o_ref[...] = acc_ref[...]

    @pl.when((j == pl.num_programs(1) - 1) & (i >= n_used))
    def _():
        o_ref[...] = jnp.zeros(o_ref.shape, F32)


def moe_ffn(block_e, n_used, row_tok, h, wg, wu, wd):
    d = h.shape[1]
    r = row_tok.shape[0]
    f = wg.shape[2]
    tm, tf = MOE_ROWS, 256
    nj = f // tf
    assert nj >= 2
    nblk = r // tm
    tok3 = row_tok.reshape(nblk, 1, tm)

    def w_col(i, j, be, nu):
        return jnp.where(i < nu[0], j, 0)

    grid_spec = pltpu.PrefetchScalarGridSpec(
        num_scalar_prefetch=2,
        grid=(nblk, nj),
        in_specs=[pl.BlockSpec((None, 1, tm), lambda i, j, be, nu: (i, 0, 0), memory_space=pltpu.SMEM),
                  pl.BlockSpec((None, 1, tm), lambda i, j, be, nu: (jnp.minimum(i + 1, nblk - 1), 0, 0),
                               memory_space=pltpu.SMEM),
                  pl.BlockSpec(memory_space=pl.ANY),
                  pl.BlockSpec((None, d, tf), lambda i, j, be, nu: (be[i], 0, w_col(i, j, be, nu))),
                  pl.BlockSpec((None, d, tf), lambda i, j, be, nu: (be[i], 0, w_col(i, j, be, nu))),
                  pl.BlockSpec((None, tf, d), lambda i, j, be, nu: (be[i], w_col(i, j, be, nu), 0))],
        out_specs=pl.BlockSpec((tm, d), lambda i, j, be, nu: (i, 0)),
        scratch_shapes=[pltpu.VMEM((tm, d), F32), pltpu.VMEM((tm, d), BF16),
                        pltpu.VMEM((tm, d), F32), pltpu.SemaphoreType.DMA(())],
    )
    return pl.pallas_call(
        _moe_ffn_kernel,
        grid_spec=grid_spec,
        out_shape=jax.ShapeDtypeStruct((r, d), F32),
        compiler_params=_cparams(("arbitrary", "arbitrary")),
        name="moe_ffn",
    )(block_e, n_used, tok3, tok3, h, wg, wu, wd)


def _router_kernel(h_ref, w_ref, b_ref, o_ref):
    logits = jnp.dot(h_ref[...], w_ref[...], preferred_element_type=F32,
                     precision=lax.Precision.HIGHEST) + b_ref[...]
    lane = lax.broadcasted_iota(jnp.int32, logits.shape, 1)
    logits = jnp.where(lane < N_EXPERTS, logits, NEG_INF)
    m1 = jnp.max(logits, axis=1, keepdims=True)
    i1 = jnp.min(jnp.where(logits == m1, lane, LANES), axis=1, keepdims=True)
    rest = jnp.where(lane == i1, NEG_INF, logits)
    m2 = jnp.max(rest, axis=1, keepdims=True)
    i2 = jnp.min(jnp.where(rest == m2, lane, LANES), axis=1, keepdims=True)
    e2 = jnp.exp(m2 - m1)
    g1 = 1.0 / (1.0 + e2)
    g2 = e2 / (1.0 + e2)
    out = jnp.where(lane == 0, i1.astype(F32),
                    jnp.where(lane == 1, i2.astype(F32),
                              jnp.where(lane == 2, g1, jnp.where(lane == 3, g2, 0.0))))
    o_ref[...] = out


def router(h, w_pad, b_pad):
    m, d = h.shape
    tm = 512
    return pl.pallas_call(
        _router_kernel,
        grid=(m // tm,),
        in_specs=[pl.BlockSpec((tm, d), lambda i: (i, 0)),
                  pl.BlockSpec((d, LANES), lambda i: (0, 0)),
                  pl.BlockSpec((1, LANES), lambda i: (0, 0))],
        out_specs=pl.BlockSpec((tm, LANES), lambda i: (i, 0)),
        out_shape=jax.ShapeDtypeStruct((m, LANES), F32),
        compiler_params=_cparams(("parallel",)),
        name="router",
    )(h, w_pad, b_pad)


def _alibi_slopes(n_heads):
    return jnp.asarray(2.0 ** (-8.0 * np.arange(1, n_heads + 1) / n_heads), dtype=F32)


def _bf16_parts(v):
    a = v.astype(BF16).astype(F32)
    b = (v - a).astype(BF16).astype(F32)
    c = (v - a - b).astype(BF16).astype(F32)
    return a, b, c


def _alibi_tables(seq, slopes2):
    nh = slopes2.shape[0]
    pos = jnp.arange(seq, dtype=jnp.int32)
    pos_parts = (((pos // LANES) * LANES).astype(F32), (pos % LANES).astype(F32))
    cols_q, cols_k = [], []
    for sp in _bf16_parts(slopes2):
        sp_col = jnp.broadcast_to(sp[:, None], (nh, seq))
        for pp in pos_parts:
            pp_row = jnp.broadcast_to(pp[None, :], (nh, seq))
            cols_q += [pp_row, sp_col]
            cols_k += [sp_col, -pp_row]
    pad = jnp.zeros((nh, seq, LANES - len(cols_q)), F32)
    eq = jnp.concatenate([jnp.stack(cols_q, axis=-1), pad], axis=-1).astype(BF16)
    ek = jnp.concatenate([jnp.stack(cols_k, axis=-1), pad], axis=-1).astype(BF16)
    return eq, ek


def _key_norm_bounds(p_ab, bsz, seq):
    k = p_ab.reshape(bsz, seq // FLASH_TK, FLASH_TK, -1)[..., A_COLS:2 * A_COLS].astype(F32)
    k = k.reshape(bsz, seq // FLASH_TK, FLASH_TK, A_HEADS, 2 * A_QK_DIM)
    norms = jnp.sqrt(jnp.max(jnp.sum(k * k, axis=-1), axis=2))
    return (1.01 * jnp.transpose(norms, (0, 2, 1))).reshape(-1)


def _rope_tables(seq):
    inv_freq = ROPE_THETA ** (-jnp.arange(0, B_ROPE, 2, dtype=F32) / B_ROPE)
    ang = jnp.arange(seq, dtype=F32)[:, None] * inv_freq[None, :]
    cos, sin = jnp.cos(ang), jnp.sin(ang)
    cos2 = jnp.concatenate([cos, cos], axis=-1)
    sin2 = jnp.concatenate([sin, sin], axis=-1)
    z64 = jnp.zeros((seq, B_ROPE), F32)
    ka = jnp.concatenate([cos2, z64], axis=-1)
    kb = jnp.concatenate([sin2, z64], axis=-1)
    ones = jnp.ones((seq, B_NOPE), F32)
    z128 = jnp.zeros((seq, B_NOPE), F32)
    qa = jnp.tile(jnp.concatenate([ones, cos2, z64], axis=-1), (1, B_HEADS))
    qb = jnp.tile(jnp.concatenate([z128, sin2, z64], axis=-1), (1, B_HEADS))
    return ka, kb, qa, qb


def _rot_cols(w):
    half = w.shape[-1] // 2
    return jnp.concatenate([-w[..., half:], w[..., :half]], axis=-1)


def _routing_tables(flat_e):
    n_assign = flat_e.shape[0]
    onehot = (flat_e[:, None] == jnp.arange(N_EXPERTS, dtype=jnp.int32)[None, :]).astype(jnp.int32)
    csum = jnp.cumsum(onehot, axis=0)
    rank = jnp.sum((csum - onehot) * onehot, axis=1)
    sizes = csum[-1]
    padded = ((sizes + MOE_ROWS - 1) // MOE_ROWS) * MOE_ROWS
    pad_end = jnp.cumsum(padded)
    pad_start = pad_end - padded
    dest = (pad_start[flat_e] + rank).astype(jnp.int32)
    n_blocks = -(-n_assign // MOE_ROWS) + N_EXPERTS
    n_rows = n_blocks * MOE_ROWS
    tok = jnp.arange(n_assign, dtype=jnp.int32) // TOP_K
    row_tok = jnp.zeros((n_rows,), jnp.int32).at[dest].set(tok)
    block_start = jnp.arange(n_blocks, dtype=jnp.int32) * MOE_ROWS
    block_e = jnp.minimum(jnp.searchsorted(pad_end, block_start, side='right'), N_EXPERTS - 1).astype(jnp.int32)
    n_used = (pad_end[-1:] // MOE_ROWS).astype(jnp.int32)
    return dest, row_tok, block_e, n_used


def kernel(x, c, ada_w, ada_b, ln_g, ln_b, ab_w_in, ab_lam, ab_diff_g, ab_q_norm_g, ab_kv_norm_g, ab_w_uq,
           ab_w_ukv, ab_w_out, c_w_in, c_sink, c_w_out, ffn_w_gate, ffn_w_up, ffn_w_down, moe_w_router,
           moe_b_router, moe_w_gate, moe_w_up, moe_w_down):
    bsz, seq, d = x.shape
    depth = ada_w.shape[0]
    n_tok = bsz * seq
    alpha = (2 * depth) ** 0.25

    c_pad = jnp.zeros((8, d), F32).at[:bsz].set(c)
    mods = adaln_all(c_pad, ada_w.reshape(2 * depth, d, 3 * d), ada_b.reshape(2 * depth, 1, 3 * d))
    mods = mods.reshape(2 * depth, 8, 3, d)

    ka, kb, qa, qb = _rope_tables(seq)
    slopes_a2 = _alibi_slopes(A_HEADS) * LOG2E
    eq_a, ek_a = _alibi_tables(seq, slopes_a2)
    slopes_c = _alibi_slopes(C_HEADS)
    scale_a = A_QK_DIM ** -0.5 * LOG2E
    scale_b = (B_NOPE + B_ROPE) ** -0.5 * LOG2E
    scale_c = C_HEAD_DIM ** -0.5

    h = modulate(x, mods, 0, BF16)
    for layer in range(depth):
        i = layer // 2
        l2 = 2 * layer
        last = layer == depth - 1
        if layer % 2 == 0:
            lam_init = 0.8 - 0.6 * math.exp(-0.3 * layer)
            w_in = ab_w_in[i]
            w_kr = w_in[:, 3 * A_COLS + B_Q_LORA + B_KV_LORA:]
            w_in2 = jnp.concatenate([w_in[:, :A_COLS] * scale_a, w_in[:, A_COLS:], _rot_cols(w_kr)],
                                    axis=1).astype(BF16)
            p_ab = inproj_ab(h.reshape(n_tok, d), w_in2, ka, kb, seq)
            o_a = flash_diff(p_ab.reshape(bsz, seq, -1), eq_a, ek_a, slopes_a2, _key_norm_bounds(p_ab, bsz, seq),
                             ab_lam[i], ab_diff_g[i].reshape(1, -1), lam_init, bsz, seq)
            w_uq = ab_w_uq[i].reshape(B_Q_LORA, B_HEADS, B_NOPE + B_ROPE) * scale_b
            w_uq2 = jnp.concatenate([w_uq, _rot_cols(w_uq[..., B_NOPE:])], axis=-1)
            w_uq2 = w_uq2.reshape(B_Q_LORA, B_HEADS * 2 * LANES).astype(BF16)
            q_cat = uq_proj(p_ab, 3 * A_COLS // B_Q_LORA, ab_q_norm_g[i].reshape(1, -1), w_uq2, qa, qb, seq)
            k_cat, v_b = ukv_proj(p_ab, (3 * A_COLS + B_Q_LORA) // B_KV_LORA,
                                  (3 * A_COLS + B_Q_LORA + B_KV_LORA) // LANES,
                                  ab_kv_norm_g[i].reshape(1, -1), ab_w_ukv[i].astype(BF16))
            o_b = flash_mla(q_cat.reshape(bsz, seq, -1), k_cat.reshape(bsz, seq, -1),
                            v_b.reshape(bsz, seq, -1), bsz, seq)
            acts, w_out = [o_a, o_b], ab_w_out[i]
        else:
            w_in = c_w_in[i]
            nqc = C_HEADS * C_HEAD_DIM
            w_in2 = jnp.concatenate([w_in[:, :nqc] * scale_c, w_in[:, nqc:]], axis=1).astype(BF16)
            p_c = matmul(h.reshape(n_tok, d), w_in2, BF16, tn=1024)
            o_c = window_attn(p_c.reshape(bsz, seq, -1), slopes_c, c_sink[i].astype(F32), bsz, seq)
            acts, w_out = [o_c], c_w_out[i]
        moe_next = layer % 2 == 1
        x, h = outproj_norm(acts, w_out.astype(BF16), x, mods, l2, ln_g[layer, 0], ln_b[layer, 0], alpha,
                            F32 if moe_next else BF16)

        if not moe_next:
            y = ffn(h.reshape(n_tok, d), ffn_w_gate[i].astype(BF16), ffn_w_up[i].astype(BF16),
                    ffn_w_down[i].astype(BF16)).reshape(bsz, seq, d)
            route = dest = None
        else:
            hf = h.reshape(n_tok, d)
            w_r = jnp.zeros((d, LANES), F32).at[:, :N_EXPERTS].set(moe_w_router[i])
            b_r = jnp.zeros((1, LANES), F32).at[0, :N_EXPERTS].set(moe_b_router[i])
            route = router(hf, w_r, b_r)
            flat_e = route[:, :TOP_K].astype(jnp.int32).reshape(-1)
            dest, row_tok, block_e, n_used = _routing_tables(flat_e)
            y = moe_ffn(block_e, n_used, row_tok, hf, moe_w_gate[i], moe_w_up[i], moe_w_down[i])
            route = route.reshape(bsz, seq, LANES)
        x, h = postnorm(x, y, mods, l2 + 1, ln_g[layer, 1], ln_b[layer, 1], alpha,
                        route=route, dest=dest, next_dtype=None if last else BF16)
    return x
```
